```python
import math
import jax, jax.numpy as jnp
from jax import lax
import numpy as np

D_MODEL = 2048
BATCH = 4
SEQ = 4096
DEPTH = 1

MIX_WIDTH = D_MODEL
DA_WIDTH = MIX_WIDTH // 2
DA_HEAD_DIM = 64
DA_HEADS = DA_WIDTH // (2 * DA_HEAD_DIM)
RW_WIDTH = MIX_WIDTH - DA_WIDTH
RW_HEAD_DIM = 64
RW_HEADS = RW_WIDTH // RW_HEAD_DIM
RW_DECAY_RANK = 64
RW_ICLR_RANK = 64
RW_GATE_RANK = 128
RW_COLS = 3 * RW_WIDTH + RW_DECAY_RANK + RW_ICLR_RANK + RW_GATE_RANK
IN_COLS = 3 * DA_WIDTH + RW_COLS
DA_SPLITS = [DA_WIDTH, 2 * DA_WIDTH, 3 * DA_WIDTH]
RW_SPLITS = [RW_WIDTH, 2 * RW_WIDTH, 3 * RW_WIDTH, 3 * RW_WIDTH + RW_DECAY_RANK, 3 * RW_WIDTH + RW_DECAY_RANK + RW_ICLR_RANK]
ROPE_THETA = 10000.0
Q_BLOCK = 128
N_GROUPS = 8
EXPERTS_PER_GROUP = 8
N_EXPERTS = N_GROUPS * EXPERTS_PER_GROUP
TOP_K = 2
D_EXPERT = D_MODEL // 4
EXPERT_BLOCK = 128
NORM_EPS = 1e-6
SUBLN_EPS = 1e-5
RW_GN_EPS = 64e-5

kernel_name = 'hybrid_diffattn_rwkv7_hmoe'


def rmsnorm(x, w, eps):
    xf = x.astype(jnp.float32)
    y = xf * lax.rsqrt(jnp.mean(xf * xf, axis=-1, keepdims=True) + eps)
    return (y * w.astype(jnp.float32)).astype(x.dtype)


def rope(x, pos):
    half = x.shape[-1] // 2
    inv = ROPE_THETA ** (-jnp.arange(half, dtype=jnp.float32) / half)
    ang = pos.astype(jnp.float32)[:, None] * inv[None, :]
    cos = jnp.cos(ang)[:, None, :]
    sin = jnp.sin(ang)[:, None, :]
    xf = x.astype(jnp.float32)
    x1, x2 = xf[..., :half], xf[..., half:]
    return jnp.concatenate([x1 * cos - x2 * sin, x2 * cos + x1 * sin], axis=-1).astype(x.dtype)


def diff_attention(q, k, v, lam, lam_init, subln_w):
    B, S = q.shape[0], q.shape[1]
    H, d = DA_HEADS, DA_HEAD_DIM
    nblk = S // Q_BLOCK
    scale = d ** -0.5
    qb = q.reshape(B, nblk, Q_BLOCK, 2 * H, d).transpose(1, 0, 2, 3, 4)
    kpos = jnp.arange(S)

    def block(args):
        qi, i = args
        s = jnp.einsum('bqhd,bkhd->bhqk', qi, k, preferred_element_type=jnp.float32) * scale
        qpos = i * Q_BLOCK + jnp.arange(Q_BLOCK)
        s = jnp.where(kpos[None, :] <= qpos[:, None], s, -jnp.inf)
        p = jax.nn.softmax(s, axis=-1).reshape(B, H, 2, Q_BLOCK, S)
        a = p[:, :, 0] - lam * p[:, :, 1]
        return jnp.einsum('bhqk,bkhe->bqhe', a.astype(v.dtype), v)

    o = lax.map(block, (qb, jnp.arange(nblk)))
    o = o.transpose(1, 0, 2, 3, 4).reshape(B, S, H, 2 * d)
    o = rmsnorm(o, subln_w, SUBLN_EPS).astype(jnp.float32) * (1.0 - lam_init)
    return o.reshape(B, S, DA_WIDTH).astype(v.dtype)


def wkv7_scan(r, w, k, v, a, b):
    B, S, H, N = r.shape

    def step(state, inp):
        r_t, w_t, k_t, v_t, a_t, b_t = inp
        sa = jnp.einsum('bhij,bhj->bhi', state, a_t)
        state = state * w_t[:, :, None, :] + sa[..., None] * b_t[:, :, None, :] + v_t[..., None] * k_t[:, :, None, :]
        y = jnp.einsum('bhij,bhj->bhi', state, r_t)
        return state, y

    xs = tuple(t.transpose(1, 0, 2, 3) for t in (r, w, k, v, a, b))
    s0 = jnp.zeros((B, H, N, N), jnp.float32)
    _, y = lax.scan(step, s0, xs)
    return y.transpose(1, 0, 2, 3)


def rwkv7_time_mix(z, mu, w0, w_up, a0, a_up, g_up, k_k, k_a, r_k, lnx_w, lnx_b):
    B, S, _ = z.shape
    H, N = RW_HEADS, RW_HEAD_DIM
    z_prev = jnp.pad(z, ((0, 0), (1, 0), (0, 0)))[:, :S]
    z = z + (z_prev - z) * mu
    r, k, v, wd, ad, gd = jnp.split(z, RW_SPLITS, axis=-1)
    w = -jax.nn.softplus(-(w0 + jnp.tanh(wd) @ w_up)) - 0.5
    decay = jnp.exp(-jnp.exp(w.astype(jnp.float32)))
    a = jax.nn.sigmoid(a0 + ad @ a_up)
    g = jax.nn.sigmoid(gd) @ g_up
    heads = lambda t: t.reshape(B, S, H, N).astype(jnp.float32)
    kk = heads(k * k_k)
    kk = kk / jnp.maximum(jnp.sqrt(jnp.sum(kk * kk, axis=-1, keepdims=True)), 1e-12)
    k = k * (1 + (a - 1) * k_a)
    r, k, v, a = heads(r), heads(k), heads(v), heads(a)
    y = wkv7_scan(r, heads(decay), k, v, -kk, kk * a)
    mean = jnp.mean(y, axis=-1, keepdims=True)
    var = jnp.mean(jnp.square(y - mean), axis=-1, keepdims=True)
    y = (y - mean) * lax.rsqrt(var + RW_GN_EPS)
    y = y * lnx_w.astype(jnp.float32).reshape(H, N) + lnx_b.astype(jnp.float32).reshape(H, N)
    y = y + jnp.sum(r * k * r_k.astype(jnp.float32), axis=-1, keepdims=True) * v
    return (y.reshape(B, S, RW_WIDTH) * g.astype(jnp.float32)).astype(z.dtype)


def grouped_expert_ffn(xt, eid, gate, w1, w3, w2):
    T, D = xt.shape
    M = T * TOP_K
    e_flat = eid.reshape(M)
    order = jnp.argsort(e_flat)
    e_sorted = e_flat[order]
    counts = jnp.zeros((N_EXPERTS,), jnp.int32).at[e_flat].add(1)
    start = jnp.cumsum(counts) - counts
    padded = (counts + EXPERT_BLOCK - 1) // EXPERT_BLOCK * EXPERT_BLOCK
    pend = jnp.cumsum(padded)
    pstart = pend - padded
    dest = pstart[e_sorted] + (jnp.arange(M) - start[e_sorted])
    n_blocks = -(-M // EXPERT_BLOCK) + N_EXPERTS
    P = n_blocks * EXPERT_BLOCK
    tok = jnp.full((P,), T, jnp.int32).at[dest].set((order // TOP_K).astype(jnp.int32))
    gbuf = jnp.zeros((P,), jnp.float32).at[dest].set(gate.reshape(M)[order])
    blk_start = jnp.arange(n_blocks) * EXPERT_BLOCK
    blk_e = jnp.minimum(jnp.sum(blk_start[:, None] >= pend[None, :], axis=1), N_EXPERTS - 1)
    x_pad = jnp.concatenate([xt, jnp.zeros((1, D), xt.dtype)], axis=0)
    xb = x_pad[tok].reshape(n_blocks, EXPERT_BLOCK, D)

    def block(args):
        xi, e = args
        h = jax.nn.silu(xi @ w1[e]) * (xi @ w3[e])
        return h @ w2[e]

    yb = lax.map(block, (xb, blk_e)).reshape(P, D)
    yb = yb * gbuf[:, None].astype(yb.dtype)
    return jnp.zeros((T + 1, D), yb.dtype).at[tok].add(yb)[:T]


def hier_moe(xn, wg, bg, we, be, w1, w3, w2):
    B, S, D = xn.shape
    T = B * S
    xt = xn.reshape(T, D)
    pg = jax.nn.softmax(jnp.dot(xt, wg, preferred_element_type=jnp.float32) + bg.astype(jnp.float32), axis=-1)
    g_sel = jnp.argmax(pg, axis=-1)
    p_g = jnp.take_along_axis(pg, g_sel[:, None], axis=1)
    le = (jnp.dot(xt, we, preferred_element_type=jnp.float32) + be.astype(jnp.float32)).reshape(T, N_GROUPS, EXPERTS_PER_GROUP)
    le = jnp.take_along_axis(le, g_sel[:, None, None], axis=1)[:, 0]
    pe = jax.nn.softmax(le, axis=-1)
    top_p, top_i = lax.top_k(pe, TOP_K)
    gate = p_g * top_p / jnp.sum(top_p, axis=-1, keepdims=True)
    eid = g_sel[:, None].astype(jnp.int32) * EXPERTS_PER_GROUP + top_i.astype(jnp.int32)
    return grouped_expert_ffn(xt, eid, gate, w1, w3, w2).reshape(B, S, D)


def setup_inputs(seed: int = 0) -> dict:
    key = jax.random.key(seed)
    ks = jax.random.split(key, 30)
    f32 = jnp.float32
    L = DEPTH

    def nrm(k, shape, scale):
        return jax.random.normal(k, shape, f32) * scale

    return {
        'x': nrm(ks[0], (BATCH, SEQ, D_MODEL), 1.0),
        'norm1_w': 1.0 + nrm(ks[1], (L, D_MODEL), 0.02),
        'w_in': nrm(ks[2], (L, D_MODEL, IN_COLS), D_MODEL ** -0.5),
        'lam_q1': nrm(ks[3], (L, DA_HEAD_DIM), 0.1),
        'lam_k1': nrm(ks[4], (L, DA_HEAD_DIM), 0.1),
        'lam_q2': nrm(ks[5], (L, DA_HEAD_DIM), 0.1),
        'lam_k2': nrm(ks[6], (L, DA_HEAD_DIM), 0.1),
        'subln_w': 1.0 + nrm(ks[7], (L, 2 * DA_HEAD_DIM), 0.02),
        'rw_mu': jax.random.uniform(ks[8], (L, RW_COLS), f32),
        'rw_w0': jax.random.uniform(ks[9], (L, RW_WIDTH), f32, -6.0, -1.0),
        'rw_w_up': nrm(ks[10], (L, RW_DECAY_RANK, RW_WIDTH), 0.1),
        'rw_a0': nrm(ks[11], (L, RW_WIDTH), 0.1),
        'rw_a_up': nrm(ks[12], (L, RW_ICLR_RANK, RW_WIDTH), 0.5 * RW_ICLR_RANK ** -0.5),
        'rw_g_up': nrm(ks[13], (L, RW_GATE_RANK, RW_WIDTH), RW_GATE_RANK ** -0.5),
        'rw_k_k': 0.85 + nrm(ks[14], (L, RW_WIDTH), 0.05),
        'rw_k_a': 1.0 + nrm(ks[15], (L, RW_WIDTH), 0.05),
        'rw_r_k': nrm(ks[16], (L, RW_HEADS, RW_HEAD_DIM), 0.1),
        'rw_lnx_w': 1.0 + nrm(ks[17], (L, RW_WIDTH), 0.02),
        'rw_lnx_b': nrm(ks[18], (L, RW_WIDTH), 0.02),
        'w_out': nrm(ks[19], (L, MIX_WIDTH, D_MODEL), MIX_WIDTH ** -0.5),
        'norm2_w': 1.0 + nrm(ks[20], (L, D_MODEL), 0.02),
        'router_group_w': nrm(ks[21], (L, D_MODEL, N_GROUPS), D_MODEL ** -0.5),
        'router_group_b': nrm(ks[22], (L, N_GROUPS), 0.01),
        'router_expert_w': nrm(ks[23], (L, D_MODEL, N_EXPERTS), D_MODEL ** -0.5),
        'router_expert_b': nrm(ks[24], (L, N_EXPERTS), 0.01),
        'moe_w1': nrm(ks[25], (L, N_EXPERTS, D_MODEL, D_EXPERT), D_MODEL ** -0.5),
        'moe_w3': nrm(ks[26], (L, N_EXPERTS, D_MODEL, D_EXPERT), D_MODEL ** -0.5),
        'moe_w2': nrm(ks[27], (L, N_EXPERTS, D_EXPERT, D_MODEL), D_EXPERT ** -0.5),
        'final_norm_w': 1.0 + nrm(ks[28], (D_MODEL,), 0.02),
    }


def reference(x, norm1_w, w_in, lam_q1, lam_k1, lam_q2, lam_k2, subln_w, rw_mu, rw_w0, rw_w_up, rw_a0, rw_a_up, rw_g_up, rw_k_k, rw_k_a, rw_r_k, rw_lnx_w, rw_lnx_b, w_out, norm2_w, router_group_w, router_group_b, router_expert_w, router_expert_b, moe_w1, moe_w3, moe_w2, final_norm_w):
    B, S, _ = x.shape
    pos = jnp.arange(S)
    h = x
    for l in range(DEPTH):
        xn = rmsnorm(h, norm1_w[l], NORM_EPS)
        z = xn @ w_in[l]
        dq, dk, dv, zr = jnp.split(z, DA_SPLITS, axis=-1)
        q = rope(dq.reshape(B, S, 2 * DA_HEADS, DA_HEAD_DIM), pos)
        k = rope(dk.reshape(B, S, 2 * DA_HEADS, DA_HEAD_DIM), pos)
        v = dv.reshape(B, S, DA_HEADS, 2 * DA_HEAD_DIM)
        lam_init = 0.8 - 0.6 * math.exp(-0.3 * l)
        lam = (jnp.exp(jnp.sum(lam_q1[l].astype(jnp.float32) * lam_k1[l].astype(jnp.float32)))
               - jnp.exp(jnp.sum(lam_q2[l].astype(jnp.float32) * lam_k2[l].astype(jnp.float32))) + lam_init)
        o_da = diff_attention(q, k, v, lam, lam_init, subln_w[l])
        o_rw = rwkv7_time_mix(zr, rw_mu[l], rw_w0[l], rw_w_up[l], rw_a0[l], rw_a_up[l], rw_g_up[l],
                              rw_k_k[l], rw_k_a[l], rw_r_k[l], rw_lnx_w[l], rw_lnx_b[l])
        h = h + jnp.concatenate([o_da, o_rw], axis=-1) @ w_out[l]
        xn2 = rmsnorm(h, norm2_w[l], NORM_EPS)
        h = h + hier_moe(xn2, router_group_w[l], router_group_b[l], router_expert_w[l], router_expert_b[l],
                         moe_w1[l], moe_w3[l], moe_w2[l])
    return rmsnorm(h, final_norm_w, NORM_EPS)
```

```python
import functools
import math

import jax
import jax.numpy as jnp
from jax import lax
from jax.experimental import pallas as pl
from jax.experimental.pallas import tpu as pltpu

F32 = jnp.float32
BF16 = jnp.bfloat16

DA_HEAD_DIM = 64
RW_HEAD_DIM = 64
RW_DECAY_RANK = 64
RW_ICLR_RANK = 64
RW_GATE_RANK = 128
ROPE_THETA = 10000.0
N_GROUPS = 8
EXPERTS_PER_GROUP = 8
N_EXPERTS = N_GROUPS * EXPERTS_PER_GROUP
TOP_K = 2
NORM_EPS = 1e-6
SUBLN_EPS = 1e-5
RW_GN_EPS = 64e-5
LAM_INIT = 0.8 - 0.6 * math.exp(-0.3 * 0)

V7X_LANES = 128
V7X_VMEM_LIMIT = 56 * 1024 * 1024
NEG_BIG = -1e30


def _cparams(sem):
    return pltpu.CompilerParams(dimension_semantics=sem, vmem_limit_bytes=V7X_VMEM_LIMIT)


def _inproj_kernel(*refs, n_rope, tn, eps):
    if n_rope:
        x_ref, nw_ref, w_ref, cos_ref, sin_ref, o_ref, xn_ref = refs
    else:
        x_ref, nw_ref, w_ref, o_ref, xn_ref = refs
    j = pl.program_id(1)

    @pl.when(j == 0)
    def _norm():
        x = x_ref[...]
        ms = jnp.mean(x * x, axis=-1, keepdims=True)
        xn_ref[...] = (x * lax.rsqrt(ms + eps) * nw_ref[...]).astype(BF16)

    acc = jnp.dot(xn_ref[...], w_ref[...], preferred_element_type=F32)
    if not n_rope:
        o_ref[...] = acc.astype(o_ref.dtype)
        return

    @pl.when(j < n_rope)
    def _rope():
        lane = lax.broadcasted_iota(jnp.int32, acc.shape, 1)
        first_half = (lane % DA_HEAD_DIM) < (DA_HEAD_DIM // 2)
        partner = jnp.where(first_half,
                            pltpu.roll(acc, tn - DA_HEAD_DIM // 2, 1),
                            pltpu.roll(acc, DA_HEAD_DIM // 2, 1))
        o_ref[...] = (acc * cos_ref[...] + partner * sin_ref[...]).astype(o_ref.dtype)

    @pl.when(j >= n_rope)
    def _plain():
        o_ref[...] = acc.astype(o_ref.dtype)


def _inproj(x2, nw, w, *, seq, tm, tn, out_dtype, rope_cols=0, cos=None, sin=None):
    T, D = x2.shape
    N = w.shape[1]
    assert T % tm == 0 and N % tn == 0 and seq % tm == 0 and rope_cols % tn == 0
    n_rope = rope_cols // tn
    in_specs = [
        pl.BlockSpec((tm, D), lambda i, j: (i, 0)),
        pl.BlockSpec((1, D), lambda i, j: (0, 0)),
        pl.BlockSpec((D, tn), lambda i, j: (0, j)),
    ]
    args = [x2, nw.reshape(1, D), w]
    if n_rope:
        ns = seq // tm
        in_specs += [pl.BlockSpec((tm, tn), lambda i, j: (i % ns, 0)),
                     pl.BlockSpec((tm, tn), lambda i, j: (i % ns, 0))]
        args += [cos, sin]
    return pl.pallas_call(
        functools.partial(_inproj_kernel, n_rope=n_rope, tn=tn, eps=NORM_EPS),
        grid=(T // tm, N // tn),
        in_specs=in_specs,
        out_specs=pl.BlockSpec((tm, tn), lambda i, j: (i, j)),
        out_shape=jax.ShapeDtypeStruct((T, N), out_dtype),
        scratch_shapes=[pltpu.VMEM((tm, D), BF16)],
        compiler_params=_cparams(("arbitrary", "arbitrary")),
        name="inproj_rope" if n_rope else "inproj",
    )(*args)


def _rope_tables(seq, width):
    half = DA_HEAD_DIM // 2
    inv = ROPE_THETA ** (-jnp.arange(half, dtype=F32) / half)
    ang = jnp.arange(seq, dtype=F32)[:, None] * inv[None, :]
    cos = jnp.cos(ang)
    sin = jnp.sin(ang)
    cos_h = jnp.concatenate([cos, cos], axis=-1)
    sin_h = jnp.concatenate([-sin, sin], axis=-1)
    reps = width // DA_HEAD_DIM
    return jnp.tile(cos_h, (1, reps)), jnp.tile(sin_h, (1, reps))


def _attn_kernel(lam_ref, q_ref, k_ref, v_ref, sw_ref, o_ref, qs_ref, m_ref, l_ref, acc_ref, *, tq, tk):
    qi = pl.program_id(2)
    d = DA_HEAD_DIM
    q = q_ref[...]
    lane = lax.broadcasted_iota(jnp.int32, q.shape, 1)
    zero = jnp.zeros_like(q)
    qs_ref[0:tq, :] = jnp.where(lane < d, q, zero)
    qs_ref[tq:2 * tq, :] = jnp.where(lane >= d, q, zero)
    m_ref[...] = jnp.full(m_ref.shape, NEG_BIG, F32)
    l_ref[...] = jnp.zeros(l_ref.shape, F32)
    acc_ref[...] = jnp.zeros(acc_ref.shape, F32)

    def step(j, masked):
        k = k_ref[pl.ds(pl.multiple_of(j * tk, tk), tk), :]
        v = v_ref[pl.ds(pl.multiple_of(j * tk, tk), tk), :]
        s = lax.dot_general(qs_ref[...], k, (((1,), (1,)), ((), ())), preferred_element_type=F32)
        if masked:
            row = lax.broadcasted_iota(jnp.int32, s.shape, 0)
            col = lax.broadcasted_iota(jnp.int32, s.shape, 1)
            qpos = jnp.where(row >= tq, row - tq, row)
            s = jnp.where(col <= qpos, s, NEG_BIG)
        m_old = m_ref[...]
        m_new = jnp.maximum(m_old, jnp.max(s, axis=-1, keepdims=True))
        alpha = jnp.exp(m_old - m_new)
        p = jnp.exp(s - m_new)
        l_ref[...] = alpha * l_ref[...] + jnp.sum(p, axis=-1, keepdims=True)
        acc_ref[...] = alpha * acc_ref[...] + jnp.dot(p.astype(BF16), v, preferred_element_type=F32)
        m_ref[...] = m_new

    n_full = qi * (tq // tk)

    def body(j, c):
        step(j, False)
        return c

    lax.fori_loop(0, n_full, body, 0)
    assert tq == tk
    step(n_full, True)

    o = acc_ref[...] / l_ref[...]
    od = o[0:tq, :] - lam_ref[0] * o[tq:2 * tq, :]
    ms = jnp.mean(od * od, axis=-1, keepdims=True)
    o_ref[...] = ((od * lax.rsqrt(ms + SUBLN_EPS) * sw_ref[...]) * (1.0 - LAM_INIT)).astype(o_ref.dtype)


def _diff_attention(zda, lam, subln_w, *, batch, seq, n_heads, tq, tk):
    T = zda.shape[0]
    hw = 2 * DA_HEAD_DIM
    nq = seq // tq
    return pl.pallas_call(
        functools.partial(_attn_kernel, tq=tq, tk=tk),
        grid=(batch, n_heads, nq),
        in_specs=[
            pl.BlockSpec(memory_space=pltpu.SMEM),
            pl.BlockSpec((tq, hw), lambda b, h, i: (b * nq + i, h)),
            pl.BlockSpec((seq, hw), lambda b, h, i: (b, n_heads + h)),
            pl.BlockSpec((seq, hw), lambda b, h, i: (b, 2 * n_heads + h)),
            pl.BlockSpec((1, hw), lambda b, h, i: (0, 0)),
        ],
        out_specs=pl.BlockSpec((tq, hw), lambda b, h, i: (b * nq + i, h)),
        out_shape=jax.ShapeDtypeStruct((T, n_heads * hw), BF16),
        scratch_shapes=[
            pltpu.VMEM((2 * tq, hw), BF16),
            pltpu.VMEM((2 * tq, 1), F32),
            pltpu.VMEM((2 * tq, 1), F32),
            pltpu.VMEM((2 * tq, hw), F32),
        ],
        compiler_params=_cparams(("arbitrary", "arbitrary", "arbitrary")),
        name="diff_attn",
    )(lam.reshape(1), zda, zda, zda, subln_w.reshape(1, hw))


RW_CHUNK = 64
RW_PAIR = 2 * RW_HEAD_DIM
RW_INV_BLOCK = 16


def _mm(a, b):
    return jnp.dot(a.astype(BF16), b.astype(BF16), preferred_element_type=F32)


def _mm_nt(a, b):
    return lax.dot_general(a.astype(BF16), b.astype(BF16), (((1,), (1,)), ((), ())),
                           preferred_element_type=F32)


def _mm_tn(a, b):
    return lax.dot_general(a.astype(BF16), b.astype(BF16), (((0,), (0,)), ((), ())),
                           preferred_element_type=F32)


def _mm_split(x, e):
    hi = x.astype(BF16)
    lo = (x - hi.astype(F32)).astype(BF16)
    return (jnp.dot(hi, e, preferred_element_type=F32) + jnp.dot(lo, e, preferred_element_type=F32))


def _sigmoid(x):
    return 1.0 / (1.0 + jnp.exp(-x))


def _softplus(x):
    return jnp.maximum(x, 0.0) + jnp.log(1.0 + jnp.exp(-jnp.abs(x)))


def _unit_lower_inverse(a, eye, diag_blk):
    ad = jnp.where(diag_blk, a, 0.0)
    ao = a - ad
    a2 = _mm(ad, ad)
    a4 = _mm(a2, a2)
    a8 = _mm(a4, a4)
    td = eye + ad
    td = td + _mm(td, a2)
    td = td + _mm(td, a4)
    td = td + _mm(td, a8)
    n1 = _mm(td, ao)
    n2 = _mm(n1, n1)
    x = td + _mm(n2, td)
    return x + _mm(n1, x)


def _rwkv_kernel(zr_ref, zk_ref, zv_ref, zwa_ref, zg_ref, mur_ref, muk_ref, muv_ref, muwa_ref, mug_ref,
                 vec_ref, ww_ref, wa_ref, gup_ref, o_ref,
                 cr_ref, ck_ref, cv_ref, cwa_ref, cg_ref, state_ref, *, L, PW):
    t = pl.program_id(2)
    C = RW_CHUNK
    PL = RW_PAIR

    @pl.when(t == 0)
    def _reset():
        for c_ref in (cr_ref, ck_ref, cv_ref, cwa_ref, cg_ref):
            c_ref[...] = jnp.zeros(c_ref.shape, F32)
        state_ref[...] = jnp.zeros(state_ref.shape, F32)

    def shift_mix(z_ref, mu_ref, c_ref):
        z = z_ref[...]
        row = lax.broadcasted_iota(jnp.int32, z.shape, 0)
        zprev = jnp.where(row == 0, c_ref[0:1, :], pltpu.roll(z, 1, 0))
        c_ref[0:1, :] = z[L - 1:L, :]
        return z + (zprev - z) * mu_ref[...]

    r_all = shift_mix(zr_ref, mur_ref, cr_ref)
    k_all = shift_mix(zk_ref, muk_ref, ck_ref)
    v_all = shift_mix(zv_ref, muv_ref, cv_ref)
    zwa = shift_mix(zwa_ref, muwa_ref, cwa_ref)
    zg = shift_mix(zg_ref, mug_ref, cg_ref)

    w0, a0, k_k, k_a, r_k, lnx_w, lnx_b = (vec_ref[i:i + 1, :] for i in range(7))
    w_pre = w0 + jnp.dot(jnp.tanh(zwa).astype(BF16), ww_ref[...], preferred_element_type=F32)
    logdec_all = -jnp.exp(-_softplus(-w_pre) - 0.5)
    a_all = _sigmoid(a0 + jnp.dot(zwa.astype(BF16), wa_ref[...], preferred_element_type=F32))
    g_all = jnp.dot(_sigmoid(zg).astype(BF16), gup_ref[...], preferred_element_type=F32)

    ri = lax.broadcasted_iota(jnp.int32, (PL, PL), 0)
    ci = lax.broadcasted_iota(jnp.int32, (PL, PL), 1)
    eye = (ri == ci).astype(F32)
    strict = ci < ri
    incl = ci <= ri
    diag_blk = (ri // RW_INV_BLOCK) == (ci // RW_INV_BLOCK)
    seg_ones = ((ri // RW_HEAD_DIM) == (ci // RW_HEAD_DIM)).astype(BF16)
    rc = lax.broadcasted_iota(jnp.int32, (C, C), 0)
    cc = lax.broadcasted_iota(jnp.int32, (C, C), 1)
    tri_incl = (cc <= rc).astype(BF16)
    lane = lax.broadcasted_iota(jnp.int32, (C, PL), 1)
    head0 = lane < RW_HEAD_DIM

    def stack(x):
        return jnp.concatenate([jnp.where(head0, x, 0.0), jnp.where(head0, 0.0, x)], axis=0)

    for pi in range(PW):
        sl = slice(pi * PL, (pi + 1) * PL)
        r = r_all[:, sl]
        k = k_all[:, sl]
        v = v_all[:, sl]
        a = a_all[:, sl]
        logdec = logdec_all[:, sl]
        kk = k * k_k[:, sl]
        kkn = kk / jnp.maximum(jnp.sqrt(_mm_split(kk * kk, seg_ones)), 1e-12)
        kf = k * (1.0 + (a - 1.0) * k_a[:, sl])
        a_s = -kkn
        b_s = kkn * a
        bonus = _mm_split(r * kf * r_k[:, sl], seg_ones) * v

        ys = []
        for c in range(L // C):
            rs = slice(c * C, (c + 1) * C)
            ld = logdec[rs]
            ld_hi = ld.astype(BF16)
            ld_lo = (ld - ld_hi.astype(F32)).astype(BF16)
            cum = (jnp.dot(tri_incl, ld_hi, preferred_element_type=F32)
                   + jnp.dot(tri_incl, ld_lo, preferred_element_type=F32))
            tot = cum[C - 1:C, :]
            p_inc = jnp.exp(cum)
            p_inv = jnp.exp(-cum)
            p_prev = jnp.exp(cum - ld)
            p_end = jnp.exp(tot - cum)
            p_tot = jnp.exp(tot)
            rt_s = stack(r[rs] * p_inc)
            at_s = stack(a_s[rs] * p_prev)
            kt_s = stack(kf[rs] * p_inv)
            bt_s = stack(b_s[rs] * p_inv)
            kh_s = stack(kf[rs] * p_end)
            bh_s = stack(b_s[rs] * p_end)
            v_s = stack(v[rs])

            m1 = _mm_nt(jnp.concatenate([at_s, rt_s], axis=0), jnp.concatenate([bt_s, kt_s], axis=0))
            a_ab = jnp.where(strict, m1[0:PL, 0:PL], 0.0)
            a_ak = jnp.where(strict, m1[0:PL, PL:2 * PL], 0.0)
            a_rb = jnp.where(incl, m1[PL:2 * PL, 0:PL], 0.0)
            a_rk = jnp.where(incl, m1[PL:2 * PL, PL:2 * PL], 0.0)
            tinv = _unit_lower_inverse(a_ab, eye, diag_blk)
            wu = _mm(tinv, jnp.concatenate([at_s, _mm(a_ak, v_s)], axis=1))

            s0 = state_ref[pi]
            u = _mm_nt(wu[:, 0:PL], s0) + wu[:, PL:2 * PL]
            y2 = _mm_nt(rt_s, s0) + _mm(a_rb, u) + _mm(a_rk, v_s)
            ys.append(y2[0:C] + y2[C:2 * C])
            state_ref[pi] = s0 * p_tot + _mm_tn(jnp.concatenate([u, v_s], axis=0),
                                                jnp.concatenate([bh_s, kh_s], axis=0))

        y = jnp.concatenate(ys, axis=0) if len(ys) > 1 else ys[0]
        inv_n = 1.0 / RW_HEAD_DIM
        mean = _mm_split(y, seg_ones) * inv_n
        yc = y - mean
        var = _mm_split(yc * yc, seg_ones) * inv_n
        yn = yc * lax.rsqrt(var + RW_GN_EPS) * lnx_w[:, sl] + lnx_b[:, sl]
        o_ref[:, sl] = ((yn + bonus) * g_all[:, sl]).astype(o_ref.dtype)


def _rwkv7(zrw, mu, w0, w_up, a0, a_up, g_up, k_k, k_a, r_k, lnx_w, lnx_b, *, batch, seq, n_heads, L, PW):
    T = zrw.shape[0]
    HC = n_heads * RW_HEAD_DIM
    PWL = PW * RW_PAIR
    assert HC % PWL == 0 and seq % L == 0 and L % RW_CHUNK == 0
    ng = HC // PWL
    nt = seq // L
    lora_w = RW_DECAY_RANK + RW_ICLR_RANK
    assert lora_w == V7X_LANES and RW_GATE_RANK == V7X_LANES
    vecs = jnp.stack([w0, a0, k_k, k_a, r_k.reshape(HC), lnx_w, lnx_b, jnp.zeros((HC,), F32)], axis=0)
    ww = jnp.concatenate([w_up, jnp.zeros((RW_ICLR_RANK, HC), F32)], axis=0).astype(BF16)
    wa = jnp.concatenate([jnp.zeros((RW_DECAY_RANK, HC), F32), a_up], axis=0).astype(BF16)
    gw = g_up.astype(BF16)
    mu2 = mu.reshape(1, -1)
    cb = HC // PWL
    wa_blk = 3 * HC // V7X_LANES
    zrow = lambda b, g, t: b * nt + t
    big = lambda sec: pl.BlockSpec((L, PWL), lambda b, g, t: (zrow(b, g, t), sec * cb + g))
    small = lambda off: pl.BlockSpec((L, V7X_LANES), lambda b, g, t: (zrow(b, g, t), wa_blk + off))
    mu_big = lambda sec: pl.BlockSpec((1, PWL), lambda b, g, t: (0, sec * cb + g))
    mu_small = lambda off: pl.BlockSpec((1, V7X_LANES), lambda b, g, t: (0, wa_blk + off))
    wspec = pl.BlockSpec((V7X_LANES, PWL), lambda b, g, t: (0, g))
    return pl.pallas_call(
        functools.partial(_rwkv_kernel, L=L, PW=PW),
        grid=(batch, ng, nt),
        in_specs=[big(0), big(1), big(2), small(0), small(1),
                  mu_big(0), mu_big(1), mu_big(2), mu_small(0), mu_small(1),
                  pl.BlockSpec((8, PWL), lambda b, g, t: (0, g)),
                  wspec, wspec, wspec],
        out_specs=pl.BlockSpec((L, PWL), lambda b, g, t: (zrow(b, g, t), g)),
        out_shape=jax.ShapeDtypeStruct((T, HC), BF16),
        scratch_shapes=[pltpu.VMEM((8, PWL), F32), pltpu.VMEM((8, PWL), F32), pltpu.VMEM((8, PWL), F32),
                        pltpu.VMEM((8, V7X_LANES), F32), pltpu.VMEM((8, V7X_LANES), F32),
                        pltpu.VMEM((PW, RW_PAIR, RW_PAIR), F32)],
        compiler_params=_cparams(("arbitrary", "arbitrary", "arbitrary")),
        name="rwkv7_chunked",
    )(zrw, zrw, zrw, zrw, zrw, mu2, mu2, mu2, mu2, mu2, vecs, ww, wa, gw)


def _route(lg):
    lane_i = lax.broadcasted_iota(jnp.int32, lg.shape, 1)
    lane = lane_i.astype(F32)
    is_g = lane_i < N_GROUPS
    mg = jnp.max(jnp.where(is_g, lg, NEG_BIG), axis=-1, keepdims=True)
    eg = jnp.exp(jnp.where(is_g, lg - mg, NEG_BIG))
    pg = eg / jnp.sum(eg, axis=-1, keepdims=True)
    p_g = jnp.max(pg, axis=-1, keepdims=True)
    g_sel = jnp.min(jnp.where(is_g & (pg == p_g), lane, 1e9), axis=-1, keepdims=True)
    grp = (jnp.right_shift(lane_i, 3) - 1).astype(F32)
    is_e = (lane_i >= N_GROUPS) & (lane_i < N_GROUPS + N_EXPERTS) & (grp == g_sel)
    me = jnp.max(jnp.where(is_e, lg, NEG_BIG), axis=-1, keepdims=True)
    ee = jnp.exp(jnp.where(is_e, lg - me, NEG_BIG))
    pe = ee / jnp.sum(ee, axis=-1, keepdims=True)
    p1 = jnp.max(jnp.where(is_e, pe, -1.0), axis=-1, keepdims=True)
    i1 = jnp.min(jnp.where(is_e & (pe == p1), lane, 1e9), axis=-1, keepdims=True)
    rest = is_e & (lane != i1)
    p2 = jnp.max(jnp.where(rest, pe, -1.0), axis=-1, keepdims=True)
    i2 = jnp.min(jnp.where(rest & (pe == p2), lane, 1e9), axis=-1, keepdims=True)
    den = p1 + p2
    eid = jnp.where(lane_i == 0, i1 - N_GROUPS, jnp.where(lane_i == 1, i2 - N_GROUPS, 0.0)).astype(jnp.int32)
    gate = jnp.where(lane_i == 0, p_g * p1 / den, jnp.where(lane_i == 1, p_g * p2 / den, 0.0))
    return eid, gate


def _outproj_kernel(oda_ref, orw_ref, wt_ref, wb_ref, x_ref, n2_ref, rhi_ref, rlo_ref, rb_ref,
                    h_ref, xn_ref, eid_ref, gate_ref, *, tm, eps):
    acc = (jnp.dot(oda_ref[...], wt_ref[...], preferred_element_type=F32)
           + jnp.dot(orw_ref[...], wb_ref[...], preferred_element_type=F32))
    h = x_ref[...] + acc
    h_ref[...] = h
    ms = jnp.mean(h * h, axis=-1, keepdims=True)
    xn = h * lax.rsqrt(ms + eps) * n2_ref[...]
    nl = xn.shape[1] // V7X_LANES
    for s in range(nl):
        xn_ref[pl.ds(s, tm, stride=nl), :] = xn[:, s * V7X_LANES:(s + 1) * V7X_LANES]
    hi = xn.astype(BF16)
    lo = (xn - hi.astype(F32)).astype(BF16)
    lg = (jnp.dot(hi, rhi_ref[...], preferred_element_type=F32)
          + jnp.dot(lo, rhi_ref[...], preferred_element_type=F32)
          + jnp.dot(hi, rlo_ref[...], preferred_element_type=F32)) + rb_ref[...]
    eid, gate = _route(lg)
    eid_ref[...] = eid
    gate_ref[...] = gate


def _outproj_route(oda, orw, w_out, x2, n2, wg, bg, we, be, *, tm):
    T, D = x2.shape
    w1 = oda.shape[1]
    w2 = orw.shape[1]
    nl = D // V7X_LANES
    wt = w_out[:w1].astype(BF16)
    wb = w_out[w1:].astype(BF16)
    pad = V7X_LANES - N_GROUPS - N_EXPERTS
    wr = jnp.concatenate([wg, we, jnp.zeros((D, pad), F32)], axis=1)
    rb = jnp.concatenate([bg, be, jnp.zeros((pad,), F32)]).reshape(1, V7X_LANES)
    rhi = wr.astype(BF16)
    rlo = (wr - rhi.astype(F32)).astype(BF16)
    const = lambda shape: pl.BlockSpec(shape, lambda i: (0, 0))
    return pl.pallas_call(
        functools.partial(_outproj_kernel, tm=tm, eps=NORM_EPS),
        grid=(T // tm,),
        in_specs=[pl.BlockSpec((tm, w1), lambda i: (i, 0)), pl.BlockSpec((tm, w2), lambda i: (i, 0)),
                  const((w1, D)), const((w2, D)), pl.BlockSpec((tm, D), lambda i: (i, 0)), const((1, D)),
                  const((D, V7X_LANES)), const((D, V7X_LANES)), const((1, V7X_LANES))],
        out_specs=[pl.BlockSpec((tm, D), lambda i: (i, 0)),
                   pl.BlockSpec((tm * nl, V7X_LANES), lambda i: (i, 0)),
                   pl.BlockSpec((tm, V7X_LANES), lambda i: (i, 0)),
                   pl.BlockSpec((tm, V7X_LANES), lambda i: (i, 0))],
        out_shape=[jax.ShapeDtypeStruct((T, D), F32),
                   jax.ShapeDtypeStruct((T * nl, V7X_LANES), F32),
                   jax.ShapeDtypeStruct((T, V7X_LANES), jnp.int32),
                   jax.ShapeDtypeStruct((T, V7X_LANES), F32)],
        compiler_params=_cparams(("arbitrary",)),
        name="outproj_route",
    )(oda, orw, wt, wb, x2, n2.reshape(1, D), rhi, rlo, rb)


def _moe_kernel(blk_e_ref, nused_ref, tok_ref, dst_ref, xn_hbm, w1_ref, w3_ref, w2_ref, ys_hbm,
                xbuf, ybuf, w1b, w3b, w2b, sem_in, sem_out, *, bm, nl, n_slots):
    i = pl.program_id(0)

    def row_in(r):
        src = xn_hbm.at[pl.ds(pl.multiple_of(tok_ref[0, r] * nl, nl), nl)]
        return pltpu.make_async_copy(src, xbuf.at[pl.ds(pl.multiple_of(r * nl, nl), nl)], sem_in)

    def row_out(r):
        dst = ys_hbm.at[pl.ds(pl.multiple_of(dst_ref[0, r] * nl, nl), nl)]
        return pltpu.make_async_copy(ybuf.at[pl.ds(pl.multiple_of(r * nl, nl), nl)], dst, sem_out)

    @pl.when(i < nused_ref[0])
    def _active():
        def start_in(r, c):
            row_in(r).start()
            return c
        lax.fori_loop(0, bm, start_in, 0)

        first = jnp.logical_or(i == 0, blk_e_ref[i] != blk_e_ref[jnp.maximum(i - 1, 0)])

        @pl.when(first)
        def _cast():
            w1b[...] = w1_ref[...].astype(BF16)
            w3b[...] = w3_ref[...].astype(BF16)
            w2b[...] = w2_ref[...].astype(BF16)

        def wait_in(r, c):
            row_in(r).wait()
            return c
        lax.fori_loop(0, bm, wait_in, 0)

        x = jnp.concatenate([xbuf[pl.ds(s, bm, stride=nl), :] for s in range(nl)], axis=1).astype(BF16)
        h1 = jnp.dot(x, w1b[...], preferred_element_type=F32)
        h3 = jnp.dot(x, w3b[...], preferred_element_type=F32)
        hh = (h1 * _sigmoid(h1) * h3).astype(BF16)
        y = jnp.dot(hh, w2b[...], preferred_element_type=F32)
        for s in range(nl):
            ybuf[pl.ds(s, bm, stride=nl), :] = y[:, s * V7X_LANES:(s + 1) * V7X_LANES]

        def start_out(r, c):
            @pl.when(dst_ref[0, r] < n_slots)
            def _():
                row_out(r).start()
            return c
        lax.fori_loop(0, bm, start_out, 0)

        def wait_out(r, c):
            @pl.when(dst_ref[0, r] < n_slots)
            def _():
                row_out(r).wait()
            return c
        lax.fori_loop(0, bm, wait_out, 0)


def _moe_ffn(xn_lines, blk_e, n_used, tok_p, dst_p, w1, w3, w2, *, bm, n_slots):
    E, D, DE = w1.shape
    nl = D // V7X_LANES
    nb = blk_e.shape[0]
    grid_spec = pltpu.PrefetchScalarGridSpec(
        num_scalar_prefetch=2,
        grid=(nb,),
        in_specs=[
            pl.BlockSpec((None, 1, bm), lambda i, be, nu: (i, 0, 0), memory_space=pltpu.SMEM),
            pl.BlockSpec((None, 1, bm), lambda i, be, nu: (i, 0, 0), memory_space=pltpu.SMEM),
            pl.BlockSpec(memory_space=pl.ANY),
            pl.BlockSpec((None, D, DE), lambda i, be, nu: (be[i], 0, 0)),
            pl.BlockSpec((None, D, DE), lambda i, be, nu: (be[i], 0, 0)),
            pl.BlockSpec((None, DE, D), lambda i, be, nu: (be[i], 0, 0)),
        ],
        out_specs=pl.BlockSpec(memory_space=pl.ANY),
        scratch_shapes=[pltpu.VMEM((bm * nl, V7X_LANES), F32), pltpu.VMEM((bm * nl, V7X_LANES), F32),
                        pltpu.VMEM((D, DE), BF16), pltpu.VMEM((D, DE), BF16), pltpu.VMEM((DE, D), BF16),
                        pltpu.SemaphoreType.DMA, pltpu.SemaphoreType.DMA],
    )
    return pl.pallas_call(
        functools.partial(_moe_kernel, bm=bm, nl=nl, n_slots=n_slots),
        grid_spec=grid_spec,
        out_shape=jax.ShapeDtypeStruct((n_slots * nl, V7X_LANES), F32),
        compiler_params=_cparams(("arbitrary",)),
        name="moe_ffn",
    )(blk_e, n_used, tok_p.reshape(nb, 1, bm), dst_p.reshape(nb, 1, bm), xn_lines, w1, w3, w2)


def _moe_plan(eid, *, bm):
    T = eid.shape[0]
    M = T * TOP_K
    e_flat = eid.reshape(M)
    order = jnp.argsort(e_flat).astype(jnp.int32)
    e_sorted = e_flat[order]
    counts = jnp.zeros((N_EXPERTS,), jnp.int32).at[e_flat].add(1)
    start = jnp.cumsum(counts) - counts
    padded = (counts + bm - 1) // bm * bm
    pend = jnp.cumsum(padded)
    pstart = pend - padded
    dest = pstart[e_sorted] + (jnp.arange(M, dtype=jnp.int32) - start[e_sorted])
    nb = M // bm + N_EXPERTS
    tok_p = jnp.zeros((nb * bm,), jnp.int32).at[dest].set(order // TOP_K)
    dst_p = jnp.full((nb * bm,), M, jnp.int32).at[dest].set(order)
    blk_start = jnp.arange(nb, dtype=jnp.int32) * bm
    blk_e = jnp.minimum(jnp.sum(blk_start[:, None] >= pend[None, :], axis=1), N_EXPERTS - 1).astype(jnp.int32)
    n_used = (pend[-1] // bm).astype(jnp.int32)
    blk_e = jnp.where(jnp.arange(nb) < n_used, blk_e, blk_e[jnp.maximum(n_used - 1, 0)])
    return blk_e, n_used.reshape(1), tok_p, dst_p


def _combine_kernel(h_ref, ys_ref, gate_ref, fw_ref, o_ref, *, tm, nl, eps):
    gate = gate_ref[...]
    acc = h_ref[...]
    for j in range(TOP_K):
        yj = jnp.concatenate([ys_ref[pl.ds(j * nl + s, tm, stride=TOP_K * nl), :] for s in range(nl)], axis=1)
        acc = acc + yj * gate[:, j:j + 1]
    ms = jnp.mean(acc * acc, axis=-1, keepdims=True)
    o_ref[...] = acc * lax.rsqrt(ms + eps) * fw_ref[...]


def _combine(h, ys, gate, fw, *, tm):
    T, D = h.shape
    nl = D // V7X_LANES
    return pl.pallas_call(
        functools.partial(_combine_kernel, tm=tm, nl=nl, eps=NORM_EPS),
        grid=(T // tm,),
        in_specs=[pl.BlockSpec((tm, D), lambda i: (i, 0)),
                  pl.BlockSpec((tm * TOP_K * nl, V7X_LANES), lambda i: (i, 0)),
                  pl.BlockSpec((tm, V7X_LANES), lambda i: (i, 0)),
                  pl.BlockSpec((1, D), lambda i: (0, 0))],
        out_specs=pl.BlockSpec((tm, D), lambda i: (i, 0)),
        out_shape=jax.ShapeDtypeStruct((T, D), F32),
        compiler_params=_cparams(("arbitrary",)),
        name="combine_norm",
    )(h, ys, gate, fw.reshape(1, D))


def _tiles(T, seq, D):
    pick = lambda n, prefs: next(p for p in prefs if n % p == 0)
    return dict(
        tm_in=pick(seq, (512, 256, 128)),
        tn_da=512,
        tq=pick(seq, (256, 128)),
        rw_L=pick(seq, (256, 128, 64)),
        rw_PW=2,
        tm_out=pick(T, (256, 128)),
        bm=256,
        tm_cmb=pick(T, (256, 128)),
    )


def kernel(x, norm1_w, w_in, lam_q1, lam_k1, lam_q2, lam_k2, subln_w, rw_mu, rw_w0, rw_w_up, rw_a0, rw_a_up, rw_g_up, rw_k_k, rw_k_a, rw_r_k, rw_lnx_w, rw_lnx_b, w_out, norm2_w, router_group_w, router_group_b, router_expert_w, router_expert_b, moe_w1, moe_w3, moe_w2, final_norm_w):
    B, S, D = x.shape
    T = B * S
    depth = w_in.shape[0]
    rw_heads = rw_w0.shape[1] // RW_HEAD_DIM
    rw_cols = rw_mu.shape[1]
    da_cols = w_in.shape[2] - rw_cols
    da_width = da_cols // 3
    da_heads = da_width // (2 * DA_HEAD_DIM)
    tl = _tiles(T, S, D)
    cos, sin = _rope_tables(S, tl["tn_da"])
    colscale = jnp.concatenate([jnp.full((da_width,), DA_HEAD_DIM ** -0.5, F32), jnp.ones((2 * da_width,), F32)])

    h = x.reshape(T, D)
    for l in range(depth):
        assert l == 0, "lam_init is specialised to the first layer"
        w_da = (w_in[l][:, :da_cols] * colscale).astype(BF16)
        w_rw = w_in[l][:, da_cols:].astype(BF16)
        zda = _inproj(h, norm1_w[l], w_da, seq=S, tm=tl["tm_in"], tn=tl["tn_da"], out_dtype=BF16,
                      rope_cols=2 * da_width, cos=cos, sin=sin)
        zrw = _inproj(h, norm1_w[l], w_rw, seq=S, tm=tl["tm_in"], tn=rw_cols // 2, out_dtype=F32)
        lam = (jnp.exp(jnp.sum(lam_q1[l] * lam_k1[l])) - jnp.exp(jnp.sum(lam_q2[l] * lam_k2[l])) + LAM_INIT)
        o_da = _diff_attention(zda, lam.astype(F32), subln_w[l], batch=B, seq=S, n_heads=da_heads,
                               tq=tl["tq"], tk=tl["tq"])
        o_rw = _rwkv7(zrw, rw_mu[l], rw_w0[l], rw_w_up[l], rw_a0[l], rw_a_up[l], rw_g_up[l], rw_k_k[l], rw_k_a[l],
                      rw_r_k[l], rw_lnx_w[l], rw_lnx_b[l], batch=B, seq=S, n_heads=rw_heads,
                      L=tl["rw_L"], PW=tl["rw_PW"])
        h, xn_lines, eid, gate = _outproj_route(o_da, o_rw, w_out[l], h, norm2_w[l], router_group_w[l],
                                                router_group_b[l], router_expert_w[l], router_expert_b[l],
                                                tm=tl["tm_out"])
        blk_e, n_used, tok_p, dst_p = _moe_plan(eid[:, :TOP_K], bm=tl["bm"])
        ys = _moe_ffn(xn_lines, blk_e, n_used, tok_p, dst_p, moe_w1[l], moe_w3[l], moe_w2[l],
                      bm=tl["bm"], n_slots=T * TOP_K)
        assert depth == 1, "the final norm is fused into the last layer's combine"
        out = _combine(h, ys, gate, final_norm_w, tm=tl["tm_cmb"])
    return out.reshape(B, S, D)
```

```python
import functools
import math

import jax
import jax.numpy as jnp
from jax import lax
from jax.experimental import pallas as pl
from jax.experimental.pallas import tpu as pltpu

F32 = jnp.float32
BF16 = jnp.bfloat16

DA_HEAD_DIM = 64
RW_HEAD_DIM = 64
RW_DECAY_RANK = 64
RW_ICLR_RANK = 64
RW_GATE_RANK = 128
ROPE_THETA = 10000.0
N_GROUPS = 8
EXPERTS_PER_GROUP = 8
N_EXPERTS = N_GROUPS * EXPERTS_PER_GROUP
TOP_K = 2
NORM_EPS = 1e-6
SUBLN_EPS = 1e-5
RW_GN_EPS = 64e-5
LAM_INIT = 0.8 - 0.6 * math.exp(-0.3 * 0)

V7X_LANES = 128
V7X_VMEM_LIMIT = 56 * 1024 * 1024
NEG_BIG = -1e30
ATTN_KV_UNROLL = 2


def _cparams(sem):
    return pltpu.CompilerParams(dimension_semantics=sem, vmem_limit_bytes=V7X_VMEM_LIMIT)


def _inproj_kernel(*refs, mode, tn, tkv, eps):
    if mode == "rope":
        x_ref, nw_ref, w_ref, cos_ref, sin_ref, o_ref, xn_ref = refs
    else:
        x_ref, nw_ref, w_ref, o_ref, xn_ref = refs
    j = pl.program_id(1)

    @pl.when(j == 0)
    def _norm():
        x = x_ref[...]
        ms = jnp.mean(x * x, axis=-1, keepdims=True)
        xn_ref[...] = (x * lax.rsqrt(ms + eps) * nw_ref[...]).astype(BF16)

    if mode == "transposed":
        acc_t = lax.dot_general(w_ref[...], xn_ref[...], (((1,), (1,)), ((), ())), preferred_element_type=F32)
        for c in range(acc_t.shape[1] // tkv):
            o_ref[c] = acc_t[:, c * tkv:(c + 1) * tkv].astype(o_ref.dtype)
        return

    acc = jnp.dot(xn_ref[...], w_ref[...], preferred_element_type=F32)
    if mode == "rope":
        lane = lax.broadcasted_iota(jnp.int32, acc.shape, 1)
        first_half = (lane % DA_HEAD_DIM) < (DA_HEAD_DIM // 2)
        partner = jnp.where(first_half,
                            pltpu.roll(acc, tn - DA_HEAD_DIM // 2, 1),
                            pltpu.roll(acc, DA_HEAD_DIM // 2, 1))
        o_ref[...] = (acc * cos_ref[...] + partner * sin_ref[...]).astype(o_ref.dtype)
    else:
        o_ref[...] = acc.astype(o_ref.dtype)


def _inproj(x2, nw, w, *, seq, tm, tn, out_dtype, mode="plain", cos=None, sin=None, tkv=None):
    T, D = x2.shape
    N = w.shape[0] if mode == "transposed" else w.shape[1]
    assert T % tm == 0 and N % tn == 0 and seq % tm == 0
    w_spec = (pl.BlockSpec((tn, D), lambda i, j: (j, 0)) if mode == "transposed"
              else pl.BlockSpec((D, tn), lambda i, j: (0, j)))
    in_specs = [
        pl.BlockSpec((tm, D), lambda i, j: (i, 0)),
        pl.BlockSpec((1, D), lambda i, j: (0, 0)),
        w_spec,
    ]
    args = [x2, nw.reshape(1, D), w]
    if mode == "rope":
        ns = seq // tm
        in_specs += [pl.BlockSpec((tm, tn), lambda i, j: (i % ns, 0)),
                     pl.BlockSpec((tm, tn), lambda i, j: (i % ns, 0))]
        args += [cos, sin]
    if mode == "transposed":
        assert tm % tkv == 0
        out_specs = pl.BlockSpec((tm // tkv, tn, tkv), lambda i, j: (i, j, 0))
        out_shape = jax.ShapeDtypeStruct((T // tkv, N, tkv), out_dtype)
    else:
        out_specs = pl.BlockSpec((tm, tn), lambda i, j: (i, j))
        out_shape = jax.ShapeDtypeStruct((T, N), out_dtype)
    return pl.pallas_call(
        functools.partial(_inproj_kernel, mode=mode, tn=tn, tkv=tkv, eps=NORM_EPS),
        grid=(T // tm, N // tn),
        in_specs=in_specs,
        out_specs=out_specs,
        out_shape=out_shape,
        scratch_shapes=[pltpu.VMEM((tm, D), BF16)],
        compiler_params=_cparams(("arbitrary", "arbitrary")),
        name="inproj_" + mode,
    )(*args)


def _rope_tables(seq, width):
    half = DA_HEAD_DIM // 2
    inv = ROPE_THETA ** (-jnp.arange(half, dtype=F32) / half)
    ang = jnp.arange(seq, dtype=F32)[:, None] * inv[None, :]
    cos = jnp.cos(ang)
    sin = jnp.sin(ang)
    cos_h = jnp.concatenate([cos, cos], axis=-1)
    sin_h = jnp.concatenate([-sin, sin], axis=-1)
    reps = width // DA_HEAD_DIM
    return jnp.tile(cos_h, (1, reps)), jnp.tile(sin_h, (1, reps))


def _attn_kernel(lam_ref, q_ref, k_ref, vt_ref, sw_ref, o_ref, qs_ref, m_ref, l_ref, acc_ref, *, tq, tk):
    qi = pl.program_id(2)
    d = DA_HEAD_DIM
    q = q_ref[...]
    lane = lax.broadcasted_iota(jnp.int32, q.shape, 1)
    zero = jnp.zeros_like(q)
    qs_ref[0:tq, :] = jnp.where(lane < d, q, zero)
    qs_ref[tq:2 * tq, :] = jnp.where(lane >= d, q, zero)
    m_ref[...] = jnp.full(m_ref.shape, NEG_BIG, F32)
    l_ref[...] = jnp.zeros(l_ref.shape, F32)
    acc_ref[...] = jnp.zeros(acc_ref.shape, F32)

    def step(j, masked):
        k = k_ref[pl.ds(pl.multiple_of(j * tk, tk), tk), :]
        s = lax.dot_general(k, qs_ref[...], (((1,), (1,)), ((), ())), preferred_element_type=F32)
        if masked:
            kpos = lax.broadcasted_iota(jnp.int32, s.shape, 0)
            col = lax.broadcasted_iota(jnp.int32, s.shape, 1)
            qpos = jnp.where(col >= tq, col - tq, col)
            s = jnp.where(kpos <= qpos, s, NEG_BIG)
        m_old = m_ref[...]
        m_new = jnp.maximum(m_old, jnp.max(s, axis=0, keepdims=True))
        alpha = jnp.exp(m_old - m_new)
        p = jnp.exp(s - m_new)
        l_ref[...] = alpha * l_ref[...] + jnp.sum(p, axis=0, keepdims=True)
        acc_ref[...] = alpha * acc_ref[...] + jnp.dot(vt_ref[j], p.astype(BF16), preferred_element_type=F32)
        m_ref[...] = m_new

    n_full = qi * (tq // tk)
    unroll = ATTN_KV_UNROLL

    def body(jj, c):
        for u in range(unroll):
            step(jj * unroll + u, False)
        return c

    lax.fori_loop(0, n_full // unroll, body, 0)

    def tail(j, c):
        step(j, False)
        return c

    lax.fori_loop((n_full // unroll) * unroll, n_full, tail, 0)
    assert tq == tk
    step(n_full, True)

    o = acc_ref[...] / l_ref[...]
    od = o[:, 0:tq] - lam_ref[0] * o[:, tq:2 * tq]
    ms = jnp.mean(od * od, axis=0, keepdims=True)
    on = (od * lax.rsqrt(ms + SUBLN_EPS) * sw_ref[...]) * (1.0 - LAM_INIT)
    o_ref[...] = on.T.astype(o_ref.dtype)


def _diff_attention(zqk, vt, lam, subln_w, *, batch, seq, n_heads, tq, tk):
    T = zqk.shape[0]
    hw = 2 * DA_HEAD_DIM
    nq = seq // tq
    nk = seq // tk
    sw_b = jnp.broadcast_to(subln_w.reshape(hw, 1), (hw, tq)).astype(F32)
    return pl.pallas_call(
        functools.partial(_attn_kernel, tq=tq, tk=tk),
        grid=(batch, n_heads, nq),
        in_specs=[
            pl.BlockSpec(memory_space=pltpu.SMEM),
            pl.BlockSpec((tq, hw), lambda b, h, i: (b * nq + i, h)),
            pl.BlockSpec((seq, hw), lambda b, h, i: (b, n_heads + h)),
            pl.BlockSpec((nk, hw, tk), lambda b, h, i: (b, h, 0)),
            pl.BlockSpec((hw, tq), lambda b, h, i: (0, 0)),
        ],
        out_specs=pl.BlockSpec((tq, hw), lambda b, h, i: (b * nq + i, h)),
        out_shape=jax.ShapeDtypeStruct((T, n_heads * hw), BF16),
        scratch_shapes=[
            pltpu.VMEM((2 * tq, hw), BF16),
            pltpu.VMEM((1, 2 * tq), F32),
            pltpu.VMEM((1, 2 * tq), F32),
            pltpu.VMEM((hw, 2 * tq), F32),
        ],
        compiler_params=_cparams(("arbitrary", "arbitrary", "arbitrary")),
        name="diff_attn",
    )(lam.reshape(1), zqk, zqk, vt, sw_b)


RW_CHUNK = 64
RW_PAIR = 2 * RW_HEAD_DIM
RW_INV_BLOCK = 16


def _bmm(a, b):
    return jnp.einsum("bij,bjk->bik", a.astype(BF16), b.astype(BF16), preferred_element_type=F32)


def _bmm_nt(a, b):
    return jnp.einsum("bik,bjk->bij", a.astype(BF16), b.astype(BF16), preferred_element_type=F32)


def _bmm_tn(a, b):
    return jnp.einsum("bti,btj->bij", a.astype(BF16), b.astype(BF16), preferred_element_type=F32)


def _mm_split(x, e):
    hi = x.astype(BF16)
    lo = (x - hi.astype(F32)).astype(BF16)
    return (jnp.dot(hi, e, preferred_element_type=F32) + jnp.dot(lo, e, preferred_element_type=F32))


def _sigmoid(x):
    return 1.0 / (1.0 + jnp.exp(-x))


def _softplus(x):
    return jnp.maximum(x, 0.0) + jnp.log(1.0 + jnp.exp(-jnp.abs(x)))


def _unit_lower_inverse(a, eye, diag_blk):
    ad = jnp.where(diag_blk, a, 0.0)
    ao = a - ad
    a2 = _bmm(ad, ad)
    a4 = _bmm(a2, a2)
    a8 = _bmm(a4, a4)
    td = eye + ad
    td = td + _bmm(td, a2)
    td = td + _bmm(td, a4)
    td = td + _bmm(td, a8)
    n1 = _bmm(td, ao)
    n2 = _bmm(n1, n1)
    x = td + _bmm(n2, td)
    return x + _bmm(n1, x)


def _rwkv_kernel(zr_ref, zk_ref, zv_ref, zwa_ref, zg_ref, mur_ref, muk_ref, muv_ref, muwa_ref, mug_ref,
                 vec_ref, ww_ref, wa_ref, gup_ref, o_ref,
                 cr_ref, ck_ref, cv_ref, cwa_ref, cg_ref, state_ref, *, L, PW):
    t = pl.program_id(2)
    C = RW_CHUNK
    PL = RW_PAIR
    NC = L // C

    @pl.when(t == 0)
    def _reset():
        for c_ref in (cr_ref, ck_ref, cv_ref, cwa_ref, cg_ref):
            c_ref[...] = jnp.zeros(c_ref.shape, F32)
        state_ref[...] = jnp.zeros(state_ref.shape, F32)

    def shift_mix(z_ref, mu_ref, c_ref):
        z = z_ref[...]
        row = lax.broadcasted_iota(jnp.int32, z.shape, 0)
        zprev = jnp.where(row == 0, c_ref[0:1, :], pltpu.roll(z, 1, 0))
        c_ref[0:1, :] = z[L - 1:L, :]
        return z + (zprev - z) * mu_ref[...]

    r = shift_mix(zr_ref, mur_ref, cr_ref)
    k = shift_mix(zk_ref, muk_ref, ck_ref)
    v = shift_mix(zv_ref, muv_ref, cv_ref)
    zwa = shift_mix(zwa_ref, muwa_ref, cwa_ref)
    zg = shift_mix(zg_ref, mug_ref, cg_ref)

    ri = lax.broadcasted_iota(jnp.int32, (PL, PL), 0)
    ci = lax.broadcasted_iota(jnp.int32, (PL, PL), 1)
    eye = (ri == ci).astype(F32)
    strict = ci < ri
    incl = ci <= ri
    diag_blk = (ri // RW_INV_BLOCK) == (ci // RW_INV_BLOCK)
    seg_ones = ((ri // RW_HEAD_DIM) == (ci // RW_HEAD_DIM)).astype(BF16)
    rc = lax.broadcasted_iota(jnp.int32, (C, C), 0)
    cc = lax.broadcasted_iota(jnp.int32, (C, C), 1)
    tri_incl = (cc <= rc).astype(BF16)
    head0 = lax.broadcasted_iota(jnp.int32, (C, PL), 1) < RW_HEAD_DIM
    lanes = [slice(pi * PL, (pi + 1) * PL) for pi in range(PW)]
    rows = [slice(c * C, (c + 1) * C) for c in range(NC)]

    def seg_sum(x):
        return jnp.concatenate([_mm_split(x[:, sl], seg_ones) for sl in lanes], axis=1)

    w0, a0, k_k, k_a, r_k, lnx_w, lnx_b = (vec_ref[i:i + 1, :] for i in range(7))
    w_pre = w0 + jnp.dot(jnp.tanh(zwa).astype(BF16), ww_ref[...], preferred_element_type=F32)
    logdec = -jnp.exp(-_softplus(-w_pre) - 0.5)
    a = _sigmoid(a0 + jnp.dot(zwa.astype(BF16), wa_ref[...], preferred_element_type=F32))
    g = jnp.dot(_sigmoid(zg).astype(BF16), gup_ref[...], preferred_element_type=F32)
    kk = k * k_k
    kkn = kk / jnp.maximum(jnp.sqrt(seg_sum(kk * kk)), 1e-12)
    kf = k * (1.0 + (a - 1.0) * k_a)
    a_s = -kkn
    b_s = kkn * a
    bonus = seg_sum(r * kf * r_k) * v

    ld_hi = logdec.astype(BF16)
    ld_lo = (logdec - ld_hi.astype(F32)).astype(BF16)
    cum_c = [jnp.dot(tri_incl, ld_hi[rs], preferred_element_type=F32)
             + jnp.dot(tri_incl, ld_lo[rs], preferred_element_type=F32) for rs in rows]
    tot_c = [cu[C - 1:C, :] for cu in cum_c]
    cum = jnp.concatenate(cum_c, axis=0)
    tot = jnp.concatenate([jnp.broadcast_to(tc, (C, tc.shape[1])) for tc in tot_c], axis=0)
    p_inv = jnp.exp(-cum)
    p_end = jnp.exp(tot - cum)

    def stack(x):
        out = []
        for rs in rows:
            for sl in lanes:
                blk = x[rs, sl]
                out.append(jnp.concatenate([jnp.where(head0, blk, 0.0), jnp.where(head0, 0.0, blk)], axis=0))
        return jnp.stack(out, axis=0).astype(BF16)

    rt_s = stack(r * jnp.exp(cum))
    at_s = stack(a_s * jnp.exp(cum - logdec))
    kt_s = stack(kf * p_inv)
    bt_s = stack(b_s * p_inv)
    kh_s = stack(kf * p_end)
    bh_s = stack(b_s * p_end)
    v_s = stack(v)

    m1 = _bmm_nt(jnp.concatenate([at_s, rt_s], axis=1), jnp.concatenate([bt_s, kt_s], axis=1))
    a_ab = jnp.where(strict, m1[:, 0:PL, 0:PL], 0.0)
    a_ak = jnp.where(strict, m1[:, 0:PL, PL:2 * PL], 0.0)
    a_rb = jnp.where(incl, m1[:, PL:2 * PL, 0:PL], 0.0)
    a_rk = jnp.where(incl, m1[:, PL:2 * PL, PL:2 * PL], 0.0)
    tinv = _unit_lower_inverse(a_ab, eye, diag_blk)
    wu = _bmm(tinv, jnp.concatenate([at_s, _bmm(a_ak, v_s).astype(BF16)], axis=2))

    s = state_ref[...]
    us, s0s = [], []
    for c in range(NC):
        sel = slice(c * PW, (c + 1) * PW)
        sb = s.astype(BF16)
        u = _bmm_nt(wu[sel, :, 0:PL], sb) + wu[sel, :, PL:2 * PL]
        us.append(u)
        s0s.append(sb)
        p_tot = jnp.stack([jnp.exp(tot_c[c][:, sl]) for sl in lanes], axis=0)
        s = s * p_tot + _bmm_tn(jnp.concatenate([u.astype(BF16), v_s[sel]], axis=1),
                                jnp.concatenate([bh_s[sel], kh_s[sel]], axis=1))
    state_ref[...] = s

    u_all = jnp.concatenate(us, axis=0)
    s0_all = jnp.concatenate(s0s, axis=0)
    y2 = _bmm_nt(rt_s, s0_all) + _bmm(a_rb, u_all) + _bmm(a_rk, v_s)
    y2 = y2[:, 0:C, :] + y2[:, C:2 * C, :]
    y = jnp.concatenate([jnp.concatenate([y2[c * PW + pi] for pi in range(PW)], axis=1) for c in range(NC)], axis=0)

    inv_n = 1.0 / RW_HEAD_DIM
    mean = seg_sum(y) * inv_n
    yc = y - mean
    var = seg_sum(yc * yc) * inv_n
    yn = yc * lax.rsqrt(var + RW_GN_EPS) * lnx_w + lnx_b
    o_ref[...] = ((yn + bonus) * g).astype(o_ref.dtype)


def _rwkv7(zrw, mu, w0, w_up, a0, a_up, g_up, k_k, k_a, r_k, lnx_w, lnx_b, *, batch, seq, n_heads, L, PW):
    T = zrw.shape[0]
    HC = n_heads * RW_HEAD_DIM
    PWL = PW * RW_PAIR
    assert HC % PWL == 0 and seq % L == 0 and L % RW_CHUNK == 0
    ng = HC // PWL
    nt = seq // L
    lora_w = RW_DECAY_RANK + RW_ICLR_RANK
    assert lora_w == V7X_LANES and RW_GATE_RANK == V7X_LANES
    vecs = jnp.stack([w0, a0, k_k, k_a, r_k.reshape(HC), lnx_w, lnx_b, jnp.zeros((HC,), F32)], axis=0)
    ww = jnp.concatenate([w_up, jnp.zeros((RW_ICLR_RANK, HC), F32)], axis=0).astype(BF16)
    wa = jnp.concatenate([jnp.zeros((RW_DECAY_RANK, HC), F32), a_up], axis=0).astype(BF16)
    gw = g_up.astype(BF16)
    mu2 = mu.reshape(1, -1)
    cb = HC // PWL
    wa_blk = 3 * HC // V7X_LANES
    zrow = lambda b, g, t: b * nt + t
    big = lambda sec: pl.BlockSpec((L, PWL), lambda b, g, t: (zrow(b, g, t), sec * cb + g))
    small = lambda off: pl.BlockSpec((L, V7X_LANES), lambda b, g, t: (zrow(b, g, t), wa_blk + off))
    mu_big = lambda sec: pl.BlockSpec((1, PWL), lambda b, g, t: (0, sec * cb + g))
    mu_small = lambda off: pl.BlockSpec((1, V7X_LANES), lambda b, g, t: (0, wa_blk + off))
    wspec = pl.BlockSpec((V7X_LANES, PWL), lambda b, g, t: (0, g))
    return pl.pallas_call(
        functools.partial(_rwkv_kernel, L=L, PW=PW),
        grid=(batch, ng, nt),
        in_specs=[big(0), big(1), big(2), small(0), small(1),
                  mu_big(0), mu_big(1), mu_big(2), mu_small(0), mu_small(1),
                  pl.BlockSpec((8, PWL), lambda b, g, t: (0, g)),
                  wspec, wspec, wspec],
        out_specs=pl.BlockSpec((L, PWL), lambda b, g, t: (zrow(b, g, t), g)),
        out_shape=jax.ShapeDtypeStruct((T, HC), BF16),
        scratch_shapes=[pltpu.VMEM((8, PWL), F32), pltpu.VMEM((8, PWL), F32), pltpu.VMEM((8, PWL), F32),
                        pltpu.VMEM((8, V7X_LANES), F32), pltpu.VMEM((8, V7X_LANES), F32),
                        pltpu.VMEM((PW, RW_PAIR, RW_PAIR), F32)],
        compiler_params=_cparams(("arbitrary", "arbitrary", "arbitrary")),
        name="rwkv7_chunked",
    )(zrw, zrw, zrw, zrw, zrw, mu2, mu2, mu2, mu2, mu2, vecs, ww, wa, gw)


def _route(lg):
    lane_i = lax.broadcasted_iota(jnp.int32, lg.shape, 1)
    lane = lane_i.astype(F32)
    is_g = lane_i < N_GROUPS
    mg = jnp.max(jnp.where(is_g, lg, NEG_BIG), axis=-1, keepdims=True)
    eg = jnp.exp(jnp.where(is_g, lg - mg, NEG_BIG))
    pg = eg / jnp.sum(eg, axis=-1, keepdims=True)
    p_g = jnp.max(pg, axis=-1, keepdims=True)
    g_sel = jnp.min(jnp.where(is_g & (pg == p_g), lane, 1e9), axis=-1, keepdims=True)
    grp = (jnp.right_shift(lane_i, 3) - 1).astype(F32)
    is_e = (lane_i >= N_GROUPS) & (lane_i < N_GROUPS + N_EXPERTS) & (grp == g_sel)
    me = jnp.max(jnp.where(is_e, lg, NEG_BIG), axis=-1, keepdims=True)
    ee = jnp.exp(jnp.where(is_e, lg - me, NEG_BIG))
    pe = ee / jnp.sum(ee, axis=-1, keepdims=True)
    p1 = jnp.max(jnp.where(is_e, pe, -1.0), axis=-1, keepdims=True)
    i1 = jnp.min(jnp.where(is_e & (pe == p1), lane, 1e9), axis=-1, keepdims=True)
    rest = is_e & (lane != i1)
    p2 = jnp.max(jnp.where(rest, pe, -1.0), axis=-1, keepdims=True)
    i2 = jnp.min(jnp.where(rest & (pe == p2), lane, 1e9), axis=-1, keepdims=True)
    den = p1 + p2
    eid = jnp.where(lane_i == 0, i1 - N_GROUPS, jnp.where(lane_i == 1, i2 - N_GROUPS, 0.0)).astype(jnp.int32)
    gate = jnp.where(lane_i == 0, p_g * p1 / den, jnp.where(lane_i == 1, p_g * p2 / den, 0.0))
    return eid, gate


def _outproj_kernel(oda_ref, orw_ref, wt_ref, wb_ref, x_ref, n2_ref, rhi_ref, rlo_ref, rb_ref,
                    h_ref, xn_ref, eid_ref, gate_ref, *, tm, eps):
    acc = (jnp.dot(oda_ref[...], wt_ref[...], preferred_element_type=F32)
           + jnp.dot(orw_ref[...], wb_ref[...], preferred_element_type=F32))
    h = x_ref[...] + acc
    h_ref[...] = h
    ms = jnp.mean(h * h, axis=-1, keepdims=True)
    xn = h * lax.rsqrt(ms + eps) * n2_ref[...]
    nl = xn.shape[1] // V7X_LANES
    for s in range(nl):
        xn_ref[pl.ds(s, tm, stride=nl), :] = xn[:, s * V7X_LANES:(s + 1) * V7X_LANES]
    hi = xn.astype(BF16)
    lo = (xn - hi.astype(F32)).astype(BF16)
    lg = (jnp.dot(hi, rhi_ref[...], preferred_element_type=F32)
          + jnp.dot(lo, rhi_ref[...], preferred_element_type=F32)
          + jnp.dot(hi, rlo_ref[...], preferred_element_type=F32)) + rb_ref[...]
    eid, gate = _route(lg)
    eid_ref[...] = eid
    gate_ref[...] = gate


def _outproj_route(oda, orw, w_out, x2, n2, wg, bg, we, be, *, tm):
    T, D = x2.shape
    w1 = oda.shape[1]
    w2 = orw.shape[1]
    nl = D // V7X_LANES
    wt = w_out[:w1].astype(BF16)
    wb = w_out[w1:].astype(BF16)
    pad = V7X_LANES - N_GROUPS - N_EXPERTS
    wr = jnp.concatenate([wg, we, jnp.zeros((D, pad), F32)], axis=1)
    rb = jnp.concatenate([bg, be, jnp.zeros((pad,), F32)]).reshape(1, V7X_LANES)
    rhi = wr.astype(BF16)
    rlo = (wr - rhi.astype(F32)).astype(BF16)
    const = lambda shape: pl.BlockSpec(shape, lambda i: (0, 0))
    return pl.pallas_call(
        functools.partial(_outproj_kernel, tm=tm, eps=NORM_EPS),
        grid=(T // tm,),
        in_specs=[pl.BlockSpec((tm, w1), lambda i: (i, 0)), pl.BlockSpec((tm, w2), lambda i: (i, 0)),
                  const((w1, D)), const((w2, D)), pl.BlockSpec((tm, D), lambda i: (i, 0)), const((1, D)),
                  const((D, V7X_LANES)), const((D, V7X_LANES)), const((1, V7X_LANES))],
        out_specs=[pl.BlockSpec((tm, D), lambda i: (i, 0)),
                   pl.BlockSpec((tm * nl, V7X_LANES), lambda i: (i, 0)),
                   pl.BlockSpec((tm, V7X_LANES), lambda i: (i, 0)),
                   pl.BlockSpec((tm, V7X_LANES), lambda i: (i, 0))],
        out_shape=[jax.ShapeDtypeStruct((T, D), F32),
                   jax.ShapeDtypeStruct((T * nl, V7X_LANES), F32),
                   jax.ShapeDtypeStruct((T, V7X_LANES), jnp.int32),
                   jax.ShapeDtypeStruct((T, V7X_LANES), F32)],
        compiler_params=_cparams(("arbitrary",)),
        name="outproj_route",
    )(oda, orw, wt, wb, x2, n2.reshape(1, D), rhi, rlo, rb)


MOE_DMA_UNROLL = 8


def _moe_kernel(blk_e_ref, nused_ref, tok_ref, tokn_ref, dst_ref, xn_hbm, w1_ref, w3_ref, w2_ref, ys_hbm,
                xbuf, ybuf, w1b, w3b, w2b, sem_in, sem_out, *, bm, nl, n_slots):
    i = pl.program_id(0)
    n_used = nused_ref[0]
    slot = i % 2

    def gather(ids_ref, dst_slot):
        def body(r, c):
            src = xn_hbm.at[pl.ds(pl.multiple_of(ids_ref[0, r] * nl, nl), nl)]
            pltpu.make_async_copy(src, xbuf.at[dst_slot, pl.ds(pl.multiple_of(r * nl, nl), nl)],
                                  sem_in.at[dst_slot]).start()
            return c
        lax.fori_loop(0, bm, body, 0, unroll=MOE_DMA_UNROLL)

    def wait_gather(src_slot):
        pltpu.make_async_copy(xn_hbm.at[pl.ds(0, bm * nl)], xbuf.at[src_slot], sem_in.at[src_slot]).wait()

    def scatter():
        def body(r, c):
            dst = ys_hbm.at[pl.ds(pl.multiple_of(dst_ref[0, r] * nl, nl), nl)]
            pltpu.make_async_copy(ybuf.at[pl.ds(pl.multiple_of(r * nl, nl), nl)], dst, sem_out).start()
            return c
        lax.fori_loop(0, bm, body, 0, unroll=MOE_DMA_UNROLL)

    def wait_scatter():
        pltpu.make_async_copy(ybuf, ys_hbm.at[pl.ds(0, bm * nl)], sem_out).wait()

    @pl.when(i == 0)
    def _prologue():
        gather(tok_ref, 0)
        ybuf[...] = jnp.zeros(ybuf.shape, F32)
        spare = pltpu.make_async_copy(ybuf, ys_hbm.at[pl.ds(n_slots * nl, bm * nl)], sem_out)
        spare.start()
        spare.wait()

    @pl.when(i + 1 < n_used)
    def _prefetch():
        gather(tokn_ref, 1 - slot)

    @pl.when(i < n_used)
    def _active():
        first = jnp.logical_or(i == 0, blk_e_ref[i] != blk_e_ref[jnp.maximum(i - 1, 0)])

        @pl.when(first)
        def _cast():
            w1b[...] = w1_ref[...].astype(BF16)
            w3b[...] = w3_ref[...].astype(BF16)
            w2b[...] = w2_ref[...].astype(BF16)

        wait_gather(slot)
        x = jnp.concatenate([xbuf[slot, pl.ds(s, bm, stride=nl), :] for s in range(nl)], axis=1).astype(BF16)
        h1 = jnp.dot(x, w1b[...], preferred_element_type=F32)
        h3 = jnp.dot(x, w3b[...], preferred_element_type=F32)
        hh = (h1 * _sigmoid(h1) * h3).astype(BF16)
        y = jnp.dot(hh, w2b[...], preferred_element_type=F32)

        @pl.when(i > 0)
        def _drain_prev():
            wait_scatter()

        for s in range(nl):
            ybuf[pl.ds(s, bm, stride=nl), :] = y[:, s * V7X_LANES:(s + 1) * V7X_LANES]
        scatter()

        @pl.when(i == n_used - 1)
        def _drain_last():
            wait_scatter()


def _moe_ffn(xn_lines, blk_e, n_used, tok_p, dst_p, w1, w3, w2, *, bm, n_slots):
    E, D, DE = w1.shape
    nl = D // V7X_LANES
    nb = blk_e.shape[0]
    ids = lambda imap: pl.BlockSpec((None, 1, bm), imap, memory_space=pltpu.SMEM)
    grid_spec = pltpu.PrefetchScalarGridSpec(
        num_scalar_prefetch=2,
        grid=(nb,),
        in_specs=[
            ids(lambda i, be, nu: (i, 0, 0)),
            ids(lambda i, be, nu: (jnp.minimum(i + 1, nb - 1), 0, 0)),
            ids(lambda i, be, nu: (i, 0, 0)),
            pl.BlockSpec(memory_space=pl.ANY),
            pl.BlockSpec((None, D, DE), lambda i, be, nu: (be[i], 0, 0)),
            pl.BlockSpec((None, D, DE), lambda i, be, nu: (be[i], 0, 0)),
            pl.BlockSpec((None, DE, D), lambda i, be, nu: (be[i], 0, 0)),
        ],
        out_specs=pl.BlockSpec(memory_space=pl.ANY),
        scratch_shapes=[pltpu.VMEM((2, bm * nl, V7X_LANES), F32), pltpu.VMEM((bm * nl, V7X_LANES), F32),
                        pltpu.VMEM((D, DE), BF16), pltpu.VMEM((D, DE), BF16), pltpu.VMEM((DE, D), BF16),
                        pltpu.SemaphoreType.DMA((2,)), pltpu.SemaphoreType.DMA],
    )
    tok3 = tok_p.reshape(nb, 1, bm)
    return pl.pallas_call(
        functools.partial(_moe_kernel, bm=bm, nl=nl, n_slots=n_slots),
        grid_spec=grid_spec,
        out_shape=jax.ShapeDtypeStruct(((n_slots + bm) * nl, V7X_LANES), F32),
        compiler_params=_cparams(("arbitrary",)),
        name="moe_ffn",
    )(blk_e, n_used, tok3, tok3, dst_p.reshape(nb, 1, bm), xn_lines, w1, w3, w2)


def _moe_plan(eid, *, bm):
    T = eid.shape[0]
    M = T * TOP_K
    e_flat = eid.reshape(M)
    order = jnp.argsort(e_flat).astype(jnp.int32)
    e_sorted = e_flat[order]
    counts = jnp.zeros((N_EXPERTS,), jnp.int32).at[e_flat].add(1)
    start = jnp.cumsum(counts) - counts
    padded = (counts + bm - 1) // bm * bm
    pend = jnp.cumsum(padded)
    pstart = pend - padded
    dest = pstart[e_sorted] + (jnp.arange(M, dtype=jnp.int32) - start[e_sorted])
    nb = M // bm + N_EXPERTS
    tok_p = jnp.zeros((nb * bm,), jnp.int32).at[dest].set(order // TOP_K)
    dst_p = (M + jnp.arange(nb * bm, dtype=jnp.int32) % bm).at[dest].set(order)
    blk_start = jnp.arange(nb, dtype=jnp.int32) * bm
    blk_e = jnp.minimum(jnp.sum(blk_start[:, None] >= pend[None, :], axis=1), N_EXPERTS - 1).astype(jnp.int32)
    n_used = (pend[-1] // bm).astype(jnp.int32)
    blk_e = jnp.where(jnp.arange(nb) < n_used, blk_e, blk_e[jnp.maximum(n_used - 1, 0)])
    return blk_e, n_used.reshape(1), tok_p, dst_p


def _combine_kernel(h_ref, ys_ref, gate_ref, fw_ref, o_ref, *, tm, nl, eps):
    gate = gate_ref[...]
    acc = h_ref[...]
    for j in range(TOP_K):
        yj = jnp.concatenate([ys_ref[pl.ds(j * nl + s, tm, stride=TOP_K * nl), :] for s in range(nl)], axis=1)
        acc = acc + yj * gate[:, j:j + 1]
    ms = jnp.mean(acc * acc, axis=-1, keepdims=True)
    o_ref[...] = acc * lax.rsqrt(ms + eps) * fw_ref[...]


def _combine(h, ys, gate, fw, *, tm):
    T, D = h.shape
    nl = D // V7X_LANES
    return pl.pallas_call(
        functools.partial(_combine_kernel, tm=tm, nl=nl, eps=NORM_EPS),
        grid=(T // tm,),
        in_specs=[pl.BlockSpec((tm, D), lambda i: (i, 0)),
                  pl.BlockSpec((tm * TOP_K * nl, V7X_LANES), lambda i: (i, 0)),
                  pl.BlockSpec((tm, V7X_LANES), lambda i: (i, 0)),
                  pl.BlockSpec((1, D), lambda i: (0, 0))],
        out_specs=pl.BlockSpec((tm, D), lambda i: (i, 0)),
        out_shape=jax.ShapeDtypeStruct((T, D), F32),
        compiler_params=_cparams(("arbitrary",)),
        name="combine_norm",
    )(h, ys, gate, fw.reshape(1, D))


def _tiles(T, seq, D):
    pick = lambda n, prefs: next(p for p in prefs if n % p == 0)
    return dict(
        tm_in=pick(seq, (512, 256, 128)),
        tn_da=512,
        tq=pick(seq, (256, 128)),
        rw_L=pick(seq, (128, 64)),
        rw_PW=8,
        tm_out=pick(T, (256, 128)),
        bm=256,
        tm_cmb=pick(T, (256, 128)),
    )


def kernel(x, norm1_w, w_in, lam_q1, lam_k1, lam_q2, lam_k2, subln_w, rw_mu, rw_w0, rw_w_up, rw_a0, rw_a_up, rw_g_up, rw_k_k, rw_k_a, rw_r_k, rw_lnx_w, rw_lnx_b, w_out, norm2_w, router_group_w, router_group_b, router_expert_w, router_expert_b, moe_w1, moe_w3, moe_w2, final_norm_w):
    B, S, D = x.shape
    T = B * S
    depth = w_in.shape[0]
    rw_heads = rw_w0.shape[1] // RW_HEAD_DIM
    rw_cols = rw_mu.shape[1]
    da_cols = w_in.shape[2] - rw_cols
    da_width = da_cols // 3
    da_heads = da_width // (2 * DA_HEAD_DIM)
    tl = _tiles(T, S, D)
    cos, sin = _rope_tables(S, tl["tn_da"])
    colscale = jnp.concatenate([jnp.full((da_width,), DA_HEAD_DIM ** -0.5, F32), jnp.ones((da_width,), F32)])

    h = x.reshape(T, D)
    for l in range(depth):
        assert l == 0, "lam_init is specialised to the first layer"
        w_qk = (w_in[l][:, :2 * da_width] * colscale).astype(BF16)
        w_v = w_in[l][:, 2 * da_width:da_cols].T.astype(BF16)
        w_rw = w_in[l][:, da_cols:].astype(BF16)
        zqk = _inproj(h, norm1_w[l], w_qk, seq=S, tm=tl["tm_in"], tn=tl["tn_da"], out_dtype=BF16,
                      mode="rope", cos=cos, sin=sin)
        vt = _inproj(h, norm1_w[l], w_v, seq=S, tm=tl["tm_in"], tn=tl["tn_da"], out_dtype=BF16,
                     mode="transposed", tkv=tl["tq"])
        zrw = _inproj(h, norm1_w[l], w_rw, seq=S, tm=tl["tm_in"], tn=rw_cols // 2, out_dtype=F32)
        lam = (jnp.exp(jnp.sum(lam_q1[l] * lam_k1[l])) - jnp.exp(jnp.sum(lam_q2[l] * lam_k2[l])) + LAM_INIT)
        o_da = _diff_attention(zqk, vt, lam.astype(F32), subln_w[l], batch=B, seq=S, n_heads=da_heads,
                               tq=tl["tq"], tk=tl["tq"])
        o_rw = _rwkv7(zrw, rw_mu[l], rw_w0[l], rw_w_up[l], rw_a0[l], rw_a_up[l], rw_g_up[l], rw_k_k[l], rw_k_a[l],
                      rw_r_k[l], rw_lnx_w[l], rw_lnx_b[l], batch=B, seq=S, n_heads=rw_heads,
                      L=tl["rw_L"], PW=tl["rw_PW"])
        h, xn_lines, eid, gate = _outproj_route(o_da, o_rw, w_out[l], h, norm2_w[l], router_group_w[l],
                                                router_group_b[l], router_expert_w[l], router_expert_b[l],
                                                tm=tl["tm_out"])
        blk_e, n_used, tok_p, dst_p = _moe_plan(eid[:, :TOP_K], bm=tl["bm"])
        ys = _moe_ffn(xn_lines, blk_e, n_used, tok_p, dst_p, moe_w1[l], moe_w3[l], moe_w2[l],
                      bm=tl["bm"], n_slots=T * TOP_K)
        assert depth == 1, "the final norm is fused into the last layer's combine"
        out = _combine(h, ys, gate, final_norm_w, tm=tl["tm_cmb"])
    return out.reshape(B, S, D)
```

```python
import functools
import math

import jax
import jax.numpy as jnp
from jax import lax
from jax.experimental import pallas as pl
from jax.experimental.pallas import tpu as pltpu

F32 = jnp.float32
BF16 = jnp.bfloat16

DA_HEAD_DIM = 64
RW_HEAD_DIM = 64
RW_DECAY_RANK = 64
RW_ICLR_RANK = 64
RW_GATE_RANK = 128
ROPE_THETA = 10000.0
N_GROUPS = 8
EXPERTS_PER_GROUP = 8
N_EXPERTS = N_GROUPS * EXPERTS_PER_GROUP
TOP_K = 2
NORM_EPS = 1e-6
SUBLN_EPS = 1e-5
RW_GN_EPS = 64e-5
LAM_INIT = 0.8 - 0.6 * math.exp(-0.3 * 0)

V7X_LANES = 128
V7X_VMEM_LIMIT = 56 * 1024 * 1024
NEG_BIG = -1e30
ATTN_LOOKAHEAD = 5


def _cparams(sem):
    return pltpu.CompilerParams(dimension_semantics=sem, vmem_limit_bytes=V7X_VMEM_LIMIT)


def _inproj_kernel(*refs, mode, tn, tkv, eps):
    if mode == "rope":
        x_ref, nw_ref, w_ref, cos_ref, sin_ref, o_ref, xn_ref = refs
    else:
        x_ref, nw_ref, w_ref, o_ref, xn_ref = refs
    j = pl.program_id(1)

    @pl.when(j == 0)
    def _norm():
        x = x_ref[...]
        ms = jnp.mean(x * x, axis=-1, keepdims=True)
        xn_ref[...] = (x * lax.rsqrt(ms + eps) * nw_ref[...]).astype(BF16)

    if mode == "transposed":
        acc_t = lax.dot_general(w_ref[...], xn_ref[...], (((1,), (1,)), ((), ())), preferred_element_type=F32)
        for c in range(acc_t.shape[1] // tkv):
            o_ref[c] = acc_t[:, c * tkv:(c + 1) * tkv].astype(o_ref.dtype)
        return

    acc = jnp.dot(xn_ref[...], w_ref[...], preferred_element_type=F32)
    if mode == "rope":
        lane = lax.broadcasted_iota(jnp.int32, acc.shape, 1)
        first_half = (lane % DA_HEAD_DIM) < (DA_HEAD_DIM // 2)
        partner = jnp.where(first_half,
                            pltpu.roll(acc, tn - DA_HEAD_DIM // 2, 1),
                            pltpu.roll(acc, DA_HEAD_DIM // 2, 1))
        o_ref[...] = (acc * cos_ref[...] + partner * sin_ref[...]).astype(o_ref.dtype)
    else:
        o_ref[...] = acc.astype(o_ref.dtype)


def _inproj(x2, nw, w, *, seq, tm, tn, out_dtype, mode="plain", cos=None, sin=None, tkv=None):
    T, D = x2.shape
    N = w.shape[0] if mode == "transposed" else w.shape[1]
    assert T % tm == 0 and N % tn == 0 and seq % tm == 0
    w_spec = (pl.BlockSpec((tn, D), lambda i, j: (j, 0)) if mode == "transposed"
              else pl.BlockSpec((D, tn), lambda i, j: (0, j)))
    in_specs = [
        pl.BlockSpec((tm, D), lambda i, j: (i, 0)),
        pl.BlockSpec((1, D), lambda i, j: (0, 0)),
        w_spec,
    ]
    args = [x2, nw.reshape(1, D), w]
    if mode == "rope":
        ns = seq // tm
        in_specs += [pl.BlockSpec((tm, tn), lambda i, j: (i % ns, 0)),
                     pl.BlockSpec((tm, tn), lambda i, j: (i % ns, 0))]
        args += [cos, sin]
    if mode == "transposed":
        assert tm % tkv == 0
        out_specs = pl.BlockSpec((tm // tkv, tn, tkv), lambda i, j: (i, j, 0))
        out_shape = jax.ShapeDtypeStruct((T // tkv, N, tkv), out_dtype)
    else:
        out_specs = pl.BlockSpec((tm, tn), lambda i, j: (i, j))
        out_shape = jax.ShapeDtypeStruct((T, N), out_dtype)
    return pl.pallas_call(
        functools.partial(_inproj_kernel, mode=mode, tn=tn, tkv=tkv, eps=NORM_EPS),
        grid=(T // tm, N // tn),
        in_specs=in_specs,
        out_specs=out_specs,
        out_shape=out_shape,
        scratch_shapes=[pltpu.VMEM((tm, D), BF16)],
        compiler_params=_cparams(("arbitrary", "arbitrary")),
        name="inproj_" + mode,
    )(*args)


def _rope_tables(seq, width):
    half = DA_HEAD_DIM // 2
    inv = ROPE_THETA ** (-jnp.arange(half, dtype=F32) / half)
    ang = jnp.arange(seq, dtype=F32)[:, None] * inv[None, :]
    cos = jnp.cos(ang)
    sin = jnp.sin(ang)
    cos_h = jnp.concatenate([cos, cos], axis=-1)
    sin_h = jnp.concatenate([-sin, sin], axis=-1)
    reps = width // DA_HEAD_DIM
    return jnp.tile(cos_h, (1, reps)), jnp.tile(sin_h, (1, reps))


def _attn_kernel(lam_ref, q_ref, k_ref, vt_ref, sw_ref, o_ref, qs_ref, m_ref, l_ref, acc_ref, *, tq, tk):
    qi = pl.program_id(2)
    d = DA_HEAD_DIM
    q = q_ref[...]
    lane = lax.broadcasted_iota(jnp.int32, q.shape, 1)
    zero = jnp.zeros_like(q)
    qs_ref[0:tq, :] = jnp.where(lane < d, q, zero)
    qs_ref[tq:2 * tq, :] = jnp.where(lane >= d, q, zero)
    m_ref[...] = jnp.full(m_ref.shape, NEG_BIG, F32)
    l_ref[...] = jnp.zeros(l_ref.shape, F32)
    acc_ref[...] = jnp.zeros(acc_ref.shape, F32)

    n_diag = tq // tk
    n_full = qi * n_diag

    n_strips = 2 * tq // tk

    def strip_scores(k, si, on_diagonal):
        s = lax.dot_general(k, qs_ref[si * tk:(si + 1) * tk, :], (((1,), (1,)), ((), ())),
                            preferred_element_type=F32)
        if on_diagonal:
            kpos = lax.broadcasted_iota(jnp.int32, s.shape, 0)
            qpos = lax.broadcasted_iota(jnp.int32, s.shape, 1)
            s = jnp.where(kpos <= qpos, s, NEG_BIG)
        return s

    def strip_update(vt, si, s):
        lanes = slice(si * tk, (si + 1) * tk)
        m_old = m_ref[:, lanes]
        m_new = jnp.maximum(m_old, jnp.max(s, axis=0, keepdims=True))
        alpha = jnp.exp2(m_old - m_new)
        p = jnp.exp2(s - m_new)
        l_ref[:, lanes] = alpha * l_ref[:, lanes] + jnp.sum(p, axis=0, keepdims=True)
        acc_ref[:, lanes] = alpha * acc_ref[:, lanes] + jnp.dot(vt, p.astype(BF16), preferred_element_type=F32)
        m_ref[:, lanes] = m_new

    def kv_block(j, work):
        k = k_ref[pl.ds(pl.multiple_of(j * tk, tk), tk), :]
        vt = vt_ref[j]
        pending = [strip_scores(k, si, dg) for si, dg in work[:ATTN_LOOKAHEAD]]
        for n, (si, _) in enumerate(work):
            if n + ATTN_LOOKAHEAD < len(work):
                nsi, ndg = work[n + ATTN_LOOKAHEAD]
                pending.append(strip_scores(k, nsi, ndg))
            strip_update(vt, si, pending[n])

    def full_step(j, c):
        kv_block(j, [(si, False) for si in range(n_strips)])
        return c

    lax.fori_loop(0, n_full, full_step, 0)

    for c in range(n_diag):
        kv_block(n_full + c, [(si, si % n_diag == c) for si in range(n_strips) if si % n_diag >= c])

    o = acc_ref[...] / l_ref[...]
    od = o[:, 0:tq] - lam_ref[0] * o[:, tq:2 * tq]
    ms = jnp.mean(od * od, axis=0, keepdims=True)
    on = (od * lax.rsqrt(ms + SUBLN_EPS) * sw_ref[...]) * (1.0 - LAM_INIT)
    o_ref[...] = on.T.astype(o_ref.dtype)


def _diff_attention(zqk, vt, lam, subln_w, *, batch, seq, n_heads, tq, tk):
    T = zqk.shape[0]
    hw = 2 * DA_HEAD_DIM
    nq = seq // tq
    nk = seq // tk
    sw_b = jnp.broadcast_to(subln_w.reshape(hw, 1), (hw, tq)).astype(F32)
    return pl.pallas_call(
        functools.partial(_attn_kernel, tq=tq, tk=tk),
        grid=(batch, n_heads, nq),
        in_specs=[
            pl.BlockSpec(memory_space=pltpu.SMEM),
            pl.BlockSpec((tq, hw), lambda b, h, i: (b * nq + i, h)),
            pl.BlockSpec((seq, hw), lambda b, h, i: (b, n_heads + h)),
            pl.BlockSpec((nk, hw, tk), lambda b, h, i: (b, h, 0)),
            pl.BlockSpec((hw, tq), lambda b, h, i: (0, 0)),
        ],
        out_specs=pl.BlockSpec((tq, hw), lambda b, h, i: (b * nq + i, h)),
        out_shape=jax.ShapeDtypeStruct((T, n_heads * hw), BF16),
        scratch_shapes=[
            pltpu.VMEM((2 * tq, hw), BF16),
            pltpu.VMEM((1, 2 * tq), F32),
            pltpu.VMEM((1, 2 * tq), F32),
            pltpu.VMEM((hw, 2 * tq), F32),
        ],
        compiler_params=_cparams(("arbitrary", "arbitrary", "arbitrary")),
        name="diff_attn",
    )(lam.reshape(1), zqk, zqk, vt, sw_b)


RW_CHUNK = 64
RW_PAIR = 2 * RW_HEAD_DIM
RW_INV_BLOCK = 16


def _bmm(a, b):
    return jnp.einsum("bij,bjk->bik", a.astype(BF16), b.astype(BF16), preferred_element_type=F32)


def _bmm_nt(a, b):
    return jnp.einsum("bik,bjk->bij", a.astype(BF16), b.astype(BF16), preferred_element_type=F32)


def _bmm_tn(a, b):
    return jnp.einsum("bti,btj->bij", a.astype(BF16), b.astype(BF16), preferred_element_type=F32)


def _mm_split(x, e):
    hi = x.astype(BF16)
    lo = (x - hi.astype(F32)).astype(BF16)
    return (jnp.dot(hi, e, preferred_element_type=F32) + jnp.dot(lo, e, preferred_element_type=F32))


def _sigmoid(x):
    return 1.0 / (1.0 + jnp.exp(-x))


def _softplus(x):
    return jnp.maximum(x, 0.0) + jnp.log(1.0 + jnp.exp(-jnp.abs(x)))


def _unit_lower_inverse(a, eye, diag_blk):
    ad = jnp.where(diag_blk, a, 0.0)
    ao = a - ad
    a2 = _bmm(ad, ad)
    a4 = _bmm(a2, a2)
    a8 = _bmm(a4, a4)
    td = eye + ad
    td = td + _bmm(td, a2)
    td = td + _bmm(td, a4)
    td = td + _bmm(td, a8)
    n1 = _bmm(td, ao)
    n2 = _bmm(n1, n1)
    x = td + _bmm(n2, td)
    return x + _bmm(n1, x)


def _rwkv_kernel(zr_ref, zk_ref, zv_ref, zwa_ref, zg_ref, mur_ref, muk_ref, muv_ref, muwa_ref, mug_ref,
                 vec_ref, ww_ref, wa_ref, gup_ref, o_ref,
                 cr_ref, ck_ref, cv_ref, cwa_ref, cg_ref, state_ref, *, L, PW):
    t = pl.program_id(2)
    C = RW_CHUNK
    PL = RW_PAIR
    NC = L // C

    @pl.when(t == 0)
    def _reset():
        for c_ref in (cr_ref, ck_ref, cv_ref, cwa_ref, cg_ref):
            c_ref[...] = jnp.zeros(c_ref.shape, F32)
        state_ref[...] = jnp.zeros(state_ref.shape, F32)

    def shift_mix(z_ref, mu_ref, c_ref):
        z = z_ref[...]
        row = lax.broadcasted_iota(jnp.int32, z.shape, 0)
        zprev = jnp.where(row == 0, c_ref[0:1, :], pltpu.roll(z, 1, 0))
        c_ref[0:1, :] = z[L - 1:L, :]
        return z + (zprev - z) * mu_ref[...]

    r = shift_mix(zr_ref, mur_ref, cr_ref)
    k = shift_mix(zk_ref, muk_ref, ck_ref)
    v = shift_mix(zv_ref, muv_ref, cv_ref)
    zwa = shift_mix(zwa_ref, muwa_ref, cwa_ref)
    zg = shift_mix(zg_ref, mug_ref, cg_ref)

    ri = lax.broadcasted_iota(jnp.int32, (PL, PL), 0)
    ci = lax.broadcasted_iota(jnp.int32, (PL, PL), 1)
    eye = (ri == ci).astype(F32)
    strict = ci < ri
    incl = ci <= ri
    diag_blk = (ri // RW_INV_BLOCK) == (ci // RW_INV_BLOCK)
    seg_ones = ((ri // RW_HEAD_DIM) == (ci // RW_HEAD_DIM)).astype(BF16)
    rc = lax.broadcasted_iota(jnp.int32, (C, C), 0)
    cc = lax.broadcasted_iota(jnp.int32, (C, C), 1)
    tri_incl = (cc <= rc).astype(BF16)
    head0 = lax.broadcasted_iota(jnp.int32, (C, PL), 1) < RW_HEAD_DIM
    lanes = [slice(pi * PL, (pi + 1) * PL) for pi in range(PW)]
    rows = [slice(c * C, (c + 1) * C) for c in range(NC)]

    def seg_sum(x):
        return jnp.concatenate([_mm_split(x[:, sl], seg_ones) for sl in lanes], axis=1)

    w0, a0, k_k, k_a, r_k, lnx_w, lnx_b = (vec_ref[i:i + 1, :] for i in range(7))
    w_pre = w0 + jnp.dot(jnp.tanh(zwa).astype(BF16), ww_ref[...], preferred_element_type=F32)
    logdec = -jnp.exp(-_softplus(-w_pre) - 0.5)
    a = _sigmoid(a0 + jnp.dot(zwa.astype(BF16), wa_ref[...], preferred_element_type=F32))
    g = jnp.dot(_sigmoid(zg).astype(BF16), gup_ref[...], preferred_element_type=F32)
    kk = k * k_k
    kkn = kk / jnp.maximum(jnp.sqrt(seg_sum(kk * kk)), 1e-12)
    kf = k * (1.0 + (a - 1.0) * k_a)
    a_s = -kkn
    b_s = kkn * a
    bonus = seg_sum(r * kf * r_k) * v

    ld_hi = logdec.astype(BF16)
    ld_lo = (logdec - ld_hi.astype(F32)).astype(BF16)
    cum_c = [jnp.dot(tri_incl, ld_hi[rs], preferred_element_type=F32)
             + jnp.dot(tri_incl, ld_lo[rs], preferred_element_type=F32) for rs in rows]
    tot_c = [cu[C - 1:C, :] for cu in cum_c]
    cum = jnp.concatenate(cum_c, axis=0)
    tot = jnp.concatenate([jnp.broadcast_to(tc, (C, tc.shape[1])) for tc in tot_c], axis=0)
    p_inv = jnp.exp(-cum)
    p_end = jnp.exp(tot - cum)

    def stack(x):
        out = []
        for rs in rows:
            for sl in lanes:
                blk = x[rs, sl]
                out.append(jnp.concatenate([jnp.where(head0, blk, 0.0), jnp.where(head0, 0.0, blk)], axis=0))
        return jnp.stack(out, axis=0).astype(BF16)

    rt_s = stack(r * jnp.exp(cum))
    at_s = stack(a_s * jnp.exp(cum - logdec))
    kt_s = stack(kf * p_inv)
    bt_s = stack(b_s * p_inv)
    kh_s = stack(kf * p_end)
    bh_s = stack(b_s * p_end)
    v_s = stack(v)

    m1 = _bmm_nt(jnp.concatenate([at_s, rt_s], axis=1), jnp.concatenate([bt_s, kt_s], axis=1))
    a_ab = jnp.where(strict, m1[:, 0:PL, 0:PL], 0.0)
    a_ak = jnp.where(strict, m1[:, 0:PL, PL:2 * PL], 0.0)
    a_rb = jnp.where(incl, m1[:, PL:2 * PL, 0:PL], 0.0)
    a_rk = jnp.where(incl, m1[:, PL:2 * PL, PL:2 * PL], 0.0)
    tinv = _unit_lower_inverse(a_ab, eye, diag_blk)
    wu = _bmm(tinv, jnp.concatenate([at_s, _bmm(a_ak, v_s).astype(BF16)], axis=2))

    s = state_ref[...]
    us, s0s = [], []
    for c in range(NC):
        sel = slice(c * PW, (c + 1) * PW)
        sb = s.astype(BF16)
        u = _bmm_nt(wu[sel, :, 0:PL], sb) + wu[sel, :, PL:2 * PL]
        us.append(u)
        s0s.append(sb)
        p_tot = jnp.stack([jnp.exp(tot_c[c][:, sl]) for sl in lanes], axis=0)
        s = s * p_tot + _bmm_tn(jnp.concatenate([u.astype(BF16), v_s[sel]], axis=1),
                                jnp.concatenate([bh_s[sel], kh_s[sel]], axis=1))
    state_ref[...] = s

    u_all = jnp.concatenate(us, axis=0)
    s0_all = jnp.concatenate(s0s, axis=0)
    y2 = _bmm_nt(rt_s, s0_all) + _bmm(a_rb, u_all) + _bmm(a_rk, v_s)
    y2 = y2[:, 0:C, :] + y2[:, C:2 * C, :]
    y = jnp.concatenate([jnp.concatenate([y2[c * PW + pi] for pi in range(PW)], axis=1) for c in range(NC)], axis=0)

    inv_n = 1.0 / RW_HEAD_DIM
    mean = seg_sum(y) * inv_n
    yc = y - mean
    var = seg_sum(yc * yc) * inv_n
    yn = yc * lax.rsqrt(var + RW_GN_EPS) * lnx_w + lnx_b
    o_ref[...] = ((yn + bonus) * g).astype(o_ref.dtype)


def _rwkv7(zrw, mu, w0, w_up, a0, a_up, g_up, k_k, k_a, r_k, lnx_w, lnx_b, *, batch, seq, n_heads, L, PW):
    T = zrw.shape[0]
    HC = n_heads * RW_HEAD_DIM
    PWL = PW * RW_PAIR
    assert HC % PWL == 0 and seq % L == 0 and L % RW_CHUNK == 0
    ng = HC // PWL
    nt = seq // L
    lora_w = RW_DECAY_RANK + RW_ICLR_RANK
    assert lora_w == V7X_LANES and RW_GATE_RANK == V7X_LANES
    vecs = jnp.stack([w0, a0, k_k, k_a, r_k.reshape(HC), lnx_w, lnx_b, jnp.zeros((HC,), F32)], axis=0)
    ww = jnp.concatenate([w_up, jnp.zeros((RW_ICLR_RANK, HC), F32)], axis=0).astype(BF16)
    wa = jnp.concatenate([jnp.zeros((RW_DECAY_RANK, HC), F32), a_up], axis=0).astype(BF16)
    gw = g_up.astype(BF16)
    mu2 = mu.reshape(1, -1)
    cb = HC // PWL
    wa_blk = 3 * HC // V7X_LANES
    zrow = lambda b, g, t: b * nt + t
    big = lambda sec: pl.BlockSpec((L, PWL), lambda b, g, t: (zrow(b, g, t), sec * cb + g))
    small = lambda off: pl.BlockSpec((L, V7X_LANES), lambda b, g, t: (zrow(b, g, t), wa_blk + off))
    mu_big = lambda sec: pl.BlockSpec((1, PWL), lambda b, g, t: (0, sec * cb + g))
    mu_small = lambda off: pl.BlockSpec((1, V7X_LANES), lambda b, g, t: (0, wa_blk + off))
    wspec = pl.BlockSpec((V7X_LANES, PWL), lambda b, g, t: (0, g))
    return pl.pallas_call(
        functools.partial(_rwkv_kernel, L=L, PW=PW),
        grid=(batch, ng, nt),
        in_specs=[big(0), big(1), big(2), small(0), small(1),
                  mu_big(0), mu_big(1), mu_big(2), mu_small(0), mu_small(1),
                  pl.BlockSpec((8, PWL), lambda b, g, t: (0, g)),
                  wspec, wspec, wspec],
        out_specs=pl.BlockSpec((L, PWL), lambda b, g, t: (zrow(b, g, t), g)),
        out_shape=jax.ShapeDtypeStruct((T, HC), BF16),
        scratch_shapes=[pltpu.VMEM((8, PWL), F32), pltpu.VMEM((8, PWL), F32), pltpu.VMEM((8, PWL), F32),
                        pltpu.VMEM((8, V7X_LANES), F32), pltpu.VMEM((8, V7X_LANES), F32),
                        pltpu.VMEM((PW, RW_PAIR, RW_PAIR), F32)],
        compiler_params=_cparams(("arbitrary", "arbitrary", "arbitrary")),
        name="rwkv7_chunked",
    )(zrw, zrw, zrw, zrw, zrw, mu2, mu2, mu2, mu2, mu2, vecs, ww, wa, gw)


def _route(lg):
    lane_i = lax.broadcasted_iota(jnp.int32, lg.shape, 1)
    lane = lane_i.astype(F32)
    is_g = lane_i < N_GROUPS
    mg = jnp.max(jnp.where(is_g, lg, NEG_BIG), axis=-1, keepdims=True)
    eg = jnp.exp(jnp.where(is_g, lg - mg, NEG_BIG))
    pg = eg / jnp.sum(eg, axis=-1, keepdims=True)
    p_g = jnp.max(pg, axis=-1, keepdims=True)
    g_sel = jnp.min(jnp.where(is_g & (pg == p_g), lane, 1e9), axis=-1, keepdims=True)
    grp = (jnp.right_shift(lane_i, 3) - 1).astype(F32)
    is_e = (lane_i >= N_GROUPS) & (lane_i < N_GROUPS + N_EXPERTS) & (grp == g_sel)
    me = jnp.max(jnp.where(is_e, lg, NEG_BIG), axis=-1, keepdims=True)
    ee = jnp.exp(jnp.where(is_e, lg - me, NEG_BIG))
    pe = ee / jnp.sum(ee, axis=-1, keepdims=True)
    p1 = jnp.max(jnp.where(is_e, pe, -1.0), axis=-1, keepdims=True)
    i1 = jnp.min(jnp.where(is_e & (pe == p1), lane, 1e9), axis=-1, keepdims=True)
    rest = is_e & (lane != i1)
    p2 = jnp.max(jnp.where(rest, pe, -1.0), axis=-1, keepdims=True)
    i2 = jnp.min(jnp.where(rest & (pe == p2), lane, 1e9), axis=-1, keepdims=True)
    den = p1 + p2
    eid = jnp.where(lane_i == 0, i1 - N_GROUPS, jnp.where(lane_i == 1, i2 - N_GROUPS, 0.0)).astype(jnp.int32)
    gate = jnp.where(lane_i == 0, p_g * p1 / den, jnp.where(lane_i == 1, p_g * p2 / den, 0.0))
    return eid, gate


def _outproj_kernel(oda_ref, orw_ref, wt_ref, wb_ref, x_ref, n2_ref, rhi_ref, rlo_ref, rb_ref,
                    h_ref, xn_ref, eid_ref, gate_ref, *, tm, eps):
    acc = (jnp.dot(oda_ref[...], wt_ref[...], preferred_element_type=F32)
           + jnp.dot(orw_ref[...], wb_ref[...], preferred_element_type=F32))
    h = x_ref[...] + acc
    h_ref[...] = h
    ms = jnp.mean(h * h, axis=-1, keepdims=True)
    xn = h * lax.rsqrt(ms + eps) * n2_ref[...]
    nl = xn.shape[1] // V7X_LANES
    for s in range(nl):
        xn_ref[pl.ds(s, tm, stride=nl), :] = xn[:, s * V7X_LANES:(s + 1) * V7X_LANES]
    hi = xn.astype(BF16)
    lo = (xn - hi.astype(F32)).astype(BF16)
    lg = (jnp.dot(hi, rhi_ref[...], preferred_element_type=F32)
          + jnp.dot(lo, rhi_ref[...], preferred_element_type=F32)
          + jnp.dot(hi, rlo_ref[...], preferred_element_type=F32)) + rb_ref[...]
    eid, gate = _route(lg)
    eid_ref[...] = eid
    gate_ref[...] = gate


def _outproj_route(oda, orw, w_out, x2, n2, wg, bg, we, be, *, tm):
    T, D = x2.shape
    w1 = oda.shape[1]
    w2 = orw.shape[1]
    nl = D // V7X_LANES
    wt = w_out[:w1].astype(BF16)
    wb = w_out[w1:].astype(BF16)
    pad = V7X_LANES - N_GROUPS - N_EXPERTS
    wr = jnp.concatenate([wg, we, jnp.zeros((D, pad), F32)], axis=1)
    rb = jnp.concatenate([bg, be, jnp.zeros((pad,), F32)]).reshape(1, V7X_LANES)
    rhi = wr.astype(BF16)
    rlo = (wr - rhi.astype(F32)).astype(BF16)
    const = lambda shape: pl.BlockSpec(shape, lambda i: (0, 0))
    return pl.pallas_call(
        functools.partial(_outproj_kernel, tm=tm, eps=NORM_EPS),
        grid=(T // tm,),
        in_specs=[pl.BlockSpec((tm, w1), lambda i: (i, 0)), pl.BlockSpec((tm, w2), lambda i: (i, 0)),
                  const((w1, D)), const((w2, D)), pl.BlockSpec((tm, D), lambda i: (i, 0)), const((1, D)),
                  const((D, V7X_LANES)), const((D, V7X_LANES)), const((1, V7X_LANES))],
        out_specs=[pl.BlockSpec((tm, D), lambda i: (i, 0)),
                   pl.BlockSpec((tm * nl, V7X_LANES), lambda i: (i, 0)),
                   pl.BlockSpec((tm, V7X_LANES), lambda i: (i, 0)),
                   pl.BlockSpec((tm, V7X_LANES), lambda i: (i, 0))],
        out_shape=[jax.ShapeDtypeStruct((T, D), F32),
                   jax.ShapeDtypeStruct((T * nl, V7X_LANES), F32),
                   jax.ShapeDtypeStruct((T, V7X_LANES), jnp.int32),
                   jax.ShapeDtypeStruct((T, V7X_LANES), F32)],
        compiler_params=_cparams(("arbitrary",)),
        name="outproj_route",
    )(oda, orw, wt, wb, x2, n2.reshape(1, D), rhi, rlo, rb)


MOE_DMA_UNROLL = 8


def _moe_kernel(blk_e_ref, nused_ref, tok_ref, tokn_ref, dst_ref, xn_hbm, w1_ref, w3_ref, w2_ref, ys_hbm,
                xbuf, ybuf, w1b, w3b, w2b, sem_in, sem_out, *, bm, nl, n_slots):
    i = pl.program_id(0)
    n_used = nused_ref[0]
    slot = i % 2

    def gather(ids_ref, dst_slot):
        def body(r, c):
            src = xn_hbm.at[pl.ds(pl.multiple_of(ids_ref[0, r] * nl, nl), nl)]
            pltpu.make_async_copy(src, xbuf.at[dst_slot, pl.ds(pl.multiple_of(r * nl, nl), nl)],
                                  sem_in.at[dst_slot]).start()
            return c
        lax.fori_loop(0, bm, body, 0, unroll=MOE_DMA_UNROLL)

    def wait_gather(src_slot):
        pltpu.make_async_copy(xn_hbm.at[pl.ds(0, bm * nl)], xbuf.at[src_slot], sem_in.at[src_slot]).wait()

    def scatter():
        def body(r, c):
            dst = ys_hbm.at[pl.ds(pl.multiple_of(dst_ref[0, r] * nl, nl), nl)]
            pltpu.make_async_copy(ybuf.at[pl.ds(pl.multiple_of(r * nl, nl), nl)], dst, sem_out).start()
            return c
        lax.fori_loop(0, bm, body, 0, unroll=MOE_DMA_UNROLL)

    def wait_scatter():
        pltpu.make_async_copy(ybuf, ys_hbm.at[pl.ds(0, bm * nl)], sem_out).wait()

    @pl.when(i == 0)
    def _prologue():
        gather(tok_ref, 0)
        ybuf[...] = jnp.zeros(ybuf.shape, F32)
        spare = pltpu.make_async_copy(ybuf, ys_hbm.at[pl.ds(n_slots * nl, bm * nl)], sem_out)
        spare.start()
        spare.wait()

    @pl.when(i + 1 < n_used)
    def _prefetch():
        gather(tokn_ref, 1 - slot)

    @pl.when(i < n_used)
    def _active():
        first = jnp.logical_or(i == 0, blk_e_ref[i] != blk_e_ref[jnp.maximum(i - 1, 0)])

        @pl.when(first)
        def _cast():
            w1b[...] = w1_ref[...].astype(BF16)
            w3b[...] = w3_ref[...].astype(BF16)
            w2b[...] = w2_ref[...].astype(BF16)

        wait_gather(slot)
        x = jnp.concatenate([xbuf[slot, pl.ds(s, bm, stride=nl), :] for s in range(nl)], axis=1).astype(BF16)
        h1 = jnp.dot(x, w1b[...], preferred_element_type=F32)
        h3 = jnp.dot(x, w3b[...], preferred_element_type=F32)
        hh = (h1 * _sigmoid(h1) * h3).astype(BF16)
        y = jnp.dot(hh, w2b[...], preferred_element_type=F32)

        @pl.when(i > 0)
        def _drain_prev():
            wait_scatter()

        for s in range(nl):
            ybuf[pl.ds(s, bm, stride=nl), :] = y[:, s * V7X_LANES:(s + 1) * V7X_LANES]
        scatter()

        @pl.when(i == n_used - 1)
        def _drain_last():
            wait_scatter()


def _moe_ffn(xn_lines, blk_e, n_used, tok_p, dst_p, w1, w3, w2, *, bm, n_slots):
    E, D, DE = w1.shape
    nl = D // V7X_LANES
    nb = blk_e.shape[0]
    ids = lambda imap: pl.BlockSpec((None, 1, bm), imap, memory_space=pltpu.SMEM)
    grid_spec = pltpu.PrefetchScalarGridSpec(
        num_scalar_prefetch=2,
        grid=(nb,),
        in_specs=[
            ids(lambda i, be, nu: (i, 0, 0)),
            ids(lambda i, be, nu: (jnp.minimum(i + 1, nb - 1), 0, 0)),
            ids(lambda i, be, nu: (i, 0, 0)),
            pl.BlockSpec(memory_space=pl.ANY),
            pl.BlockSpec((None, D, DE), lambda i, be, nu: (be[i], 0, 0)),
            pl.BlockSpec((None, D, DE), lambda i, be, nu: (be[i], 0, 0)),
            pl.BlockSpec((None, DE, D), lambda i, be, nu: (be[i], 0, 0)),
        ],
        out_specs=pl.BlockSpec(memory_space=pl.ANY),
        scratch_shapes=[pltpu.VMEM((2, bm * nl, V7X_LANES), F32), pltpu.VMEM((bm * nl, V7X_LANES), F32),
                        pltpu.VMEM((D, DE), BF16), pltpu.VMEM((D, DE), BF16), pltpu.VMEM((DE, D), BF16),
                        pltpu.SemaphoreType.DMA((2,)), pltpu.SemaphoreType.DMA],
    )
    tok3 = tok_p.reshape(nb, 1, bm)
    return pl.pallas_call(
        functools.partial(_moe_kernel, bm=bm, nl=nl, n_slots=n_slots),
        grid_spec=grid_spec,
        out_shape=jax.ShapeDtypeStruct(((n_slots + bm) * nl, V7X_LANES), F32),
        compiler_params=_cparams(("arbitrary",)),
        name="moe_ffn",
    )(blk_e, n_used, tok3, tok3, dst_p.reshape(nb, 1, bm), xn_lines, w1, w3, w2)


def _moe_plan(eid, *, bm):
    T = eid.shape[0]
    M = T * TOP_K
    e_flat = eid.reshape(M)
    order = jnp.argsort(e_flat).astype(jnp.int32)
    e_sorted = e_flat[order]
    counts = jnp.zeros((N_EXPERTS,), jnp.int32).at[e_flat].add(1)
    start = jnp.cumsum(counts) - counts
    padded = (counts + bm - 1) // bm * bm
    pend = jnp.cumsum(padded)
    pstart = pend - padded
    dest = pstart[e_sorted] + (jnp.arange(M, dtype=jnp.int32) - start[e_sorted])
    nb = M // bm + N_EXPERTS
    tok_p = jnp.zeros((nb * bm,), jnp.int32).at[dest].set(order // TOP_K)
    dst_p = (M + jnp.arange(nb * bm, dtype=jnp.int32) % bm).at[dest].set(order)
    blk_start = jnp.arange(nb, dtype=jnp.int32) * bm
    blk_e = jnp.minimum(jnp.sum(blk_start[:, None] >= pend[None, :], axis=1), N_EXPERTS - 1).astype(jnp.int32)
    n_used = (pend[-1] // bm).astype(jnp.int32)
    blk_e = jnp.where(jnp.arange(nb) < n_used, blk_e, blk_e[jnp.maximum(n_used - 1, 0)])
    return blk_e, n_used.reshape(1), tok_p, dst_p


def _combine_kernel(h_ref, ys_ref, gate_ref, fw_ref, o_ref, *, tm, nl, eps):
    gate = gate_ref[...]
    acc = h_ref[...]
    for j in range(TOP_K):
        yj = jnp.concatenate([ys_ref[pl.ds(j * nl + s, tm, stride=TOP_K * nl), :] for s in range(nl)], axis=1)
        acc = acc + yj * gate[:, j:j + 1]
    ms = jnp.mean(acc * acc, axis=-1, keepdims=True)
    o_ref[...] = acc * lax.rsqrt(ms + eps) * fw_ref[...]


def _combine(h, ys, gate, fw, *, tm):
    T, D = h.shape
    nl = D // V7X_LANES
    return pl.pallas_call(
        functools.partial(_combine_kernel, tm=tm, nl=nl, eps=NORM_EPS),
        grid=(T // tm,),
        in_specs=[pl.BlockSpec((tm, D), lambda i: (i, 0)),
                  pl.BlockSpec((tm * TOP_K * nl, V7X_LANES), lambda i: (i, 0)),
                  pl.BlockSpec((tm, V7X_LANES), lambda i: (i, 0)),
                  pl.BlockSpec((1, D), lambda i: (0, 0))],
        out_specs=pl.BlockSpec((tm, D), lambda i: (i, 0)),
        out_shape=jax.ShapeDtypeStruct((T, D), F32),
        compiler_params=_cparams(("arbitrary",)),
        name="combine_norm",
    )(h, ys, gate, fw.reshape(1, D))


def _tiles(T, seq, D):
    pick = lambda n, prefs: next(p for p in prefs if n % p == 0)
    return dict(
        tm_in=pick(seq, (512, 256, 128)),
        tn_da=512,
        tq=pick(seq, (1024, 512, 256, 128)),
        tk=pick(seq, (256, 128)),
        rw_L=pick(seq, (128, 64)),
        rw_PW=8,
        tm_out=pick(T, (256, 128)),
        bm=256,
        tm_cmb=pick(T, (256, 128)),
    )


def kernel(x, norm1_w, w_in, lam_q1, lam_k1, lam_q2, lam_k2, subln_w, rw_mu, rw_w0, rw_w_up, rw_a0, rw_a_up, rw_g_up, rw_k_k, rw_k_a, rw_r_k, rw_lnx_w, rw_lnx_b, w_out, norm2_w, router_group_w, router_group_b, router_expert_w, router_expert_b, moe_w1, moe_w3, moe_w2, final_norm_w):
    B, S, D = x.shape
    T = B * S
    depth = w_in.shape[0]
    rw_heads = rw_w0.shape[1] // RW_HEAD_DIM
    rw_cols = rw_mu.shape[1]
    da_cols = w_in.shape[2] - rw_cols
    da_width = da_cols // 3
    da_heads = da_width // (2 * DA_HEAD_DIM)
    tl = _tiles(T, S, D)
    cos, sin = _rope_tables(S, tl["tn_da"])
    q_scale = DA_HEAD_DIM ** -0.5 * math.log2(math.e)
    colscale = jnp.concatenate([jnp.full((da_width,), q_scale, F32), jnp.ones((da_width,), F32)])

    h = x.reshape(T, D)
    for l in range(depth):
        assert l == 0, "lam_init is specialised to the first layer"
        w_qk = (w_in[l][:, :2 * da_width] * colscale).astype(BF16)
        w_v = w_in[l][:, 2 * da_width:da_cols].T.astype(BF16)
        w_rw = w_in[l][:, da_cols:].astype(BF16)
        zqk = _inproj(h, norm1_w[l], w_qk, seq=S, tm=tl["tm_in"], tn=tl["tn_da"], out_dtype=BF16,
                      mode="rope", cos=cos, sin=sin)
        vt = _inproj(h, norm1_w[l], w_v, seq=S, tm=tl["tm_in"], tn=tl["tn_da"], out_dtype=BF16,
                     mode="transposed", tkv=tl["tk"])
        zrw = _inproj(h, norm1_w[l], w_rw, seq=S, tm=tl["tm_in"], tn=rw_cols // 2, out_dtype=F32)
        lam = (jnp.exp(jnp.sum(lam_q1[l] * lam_k1[l])) - jnp.exp(jnp.sum(lam_q2[l] * lam_k2[l])) + LAM_INIT)
        o_da = _diff_attention(zqk, vt, lam.astype(F32), subln_w[l], batch=B, seq=S, n_heads=da_heads,
                               tq=tl["tq"], tk=tl["tk"])
        o_rw = _rwkv7(zrw, rw_mu[l], rw_w0[l], rw_w_up[l], rw_a0[l], rw_a_up[l], rw_g_up[l], rw_k_k[l], rw_k_a[l],
                      rw_r_k[l], rw_lnx_w[l], rw_lnx_b[l], batch=B, seq=S, n_heads=rw_heads,
                      L=tl["rw_L"], PW=tl["rw_PW"])
        h, xn_lines, eid, gate = _outproj_route(o_da, o_rw, w_out[l], h, norm2_w[l], router_group_w[l],
                                                router_group_b[l], router_expert_w[l], router_expert_b[l],
                                                tm=tl["tm_out"])
        blk_e, n_used, tok_p, dst_p = _moe_plan(eid[:, :TOP_K], bm=tl["bm"])
        ys = _moe_ffn(xn_lines, blk_e, n_used, tok_p, dst_p, moe_w1[l], moe_w3[l], moe_w2[l],
                      bm=tl["bm"], n_slots=T * TOP_K)
        assert depth == 1, "the final norm is fused into the last layer's combine"
        out = _combine(h, ys, gate, final_norm_w, tm=tl["tm_cmb"])
    return out.reshape(B, S, D)
```

```python
import functools
import math

import jax
import jax.numpy as jnp
from jax import lax
from jax.experimental import pallas as pl
from jax.experimental.pallas import tpu as pltpu

F32 = jnp.float32
BF16 = jnp.bfloat16

DA_HEAD_DIM = 64
RW_HEAD_DIM = 64
RW_DECAY_RANK = 64
RW_ICLR_RANK = 64
RW_GATE_RANK = 128
ROPE_THETA = 10000.0
N_GROUPS = 8
EXPERTS_PER_GROUP = 8
N_EXPERTS = N_GROUPS * EXPERTS_PER_GROUP
TOP_K = 2
NORM_EPS = 1e-6
SUBLN_EPS = 1e-5
RW_GN_EPS = 64e-5
LAM_INIT = 0.8 - 0.6 * math.exp(-0.3 * 0)

V7X_LANES = 128
V7X_VMEM_LIMIT = 56 * 1024 * 1024
NEG_BIG = -1e30
ATTN_LOOKAHEAD = 5


def _cparams(sem):
    return pltpu.CompilerParams(dimension_semantics=sem, vmem_limit_bytes=V7X_VMEM_LIMIT)


def _inproj_kernel(*refs, mode, tn, tkv, eps):
    if mode == "rope":
        x_ref, nw_ref, w_ref, cos_ref, sin_ref, o_ref, xn_ref = refs
    else:
        x_ref, nw_ref, w_ref, o_ref, xn_ref = refs
    j = pl.program_id(1)

    @pl.when(j == 0)
    def _norm():
        x = x_ref[...]
        ms = jnp.mean(x * x, axis=-1, keepdims=True)
        xn_ref[...] = (x * lax.rsqrt(ms + eps) * nw_ref[...]).astype(BF16)

    if mode == "transposed":
        acc_t = lax.dot_general(w_ref[...], xn_ref[...], (((1,), (1,)), ((), ())), preferred_element_type=F32)
        for c in range(acc_t.shape[1] // tkv):
            o_ref[c] = acc_t[:, c * tkv:(c + 1) * tkv].astype(o_ref.dtype)
        return

    acc = jnp.dot(xn_ref[...], w_ref[...], preferred_element_type=F32)
    if mode == "rope":
        lane = lax.broadcasted_iota(jnp.int32, acc.shape, 1)
        first_half = (lane % DA_HEAD_DIM) < (DA_HEAD_DIM // 2)
        partner = jnp.where(first_half,
                            pltpu.roll(acc, tn - DA_HEAD_DIM // 2, 1),
                            pltpu.roll(acc, DA_HEAD_DIM // 2, 1))
        o_ref[...] = (acc * cos_ref[...] + partner * sin_ref[...]).astype(o_ref.dtype)
    else:
        o_ref[...] = acc.astype(o_ref.dtype)


def _inproj(x2, nw, w, *, seq, tm, tn, out_dtype, mode="plain", cos=None, sin=None, tkv=None):
    T, D = x2.shape
    N = w.shape[0] if mode == "transposed" else w.shape[1]
    assert T % tm == 0 and N % tn == 0 and seq % tm == 0
    w_spec = (pl.BlockSpec((tn, D), lambda i, j: (j, 0)) if mode == "transposed"
              else pl.BlockSpec((D, tn), lambda i, j: (0, j)))
    in_specs = [
        pl.BlockSpec((tm, D), lambda i, j: (i, 0)),
        pl.BlockSpec((1, D), lambda i, j: (0, 0)),
        w_spec,
    ]
    args = [x2, nw.reshape(1, D), w]
    if mode == "rope":
        ns = seq // tm
        in_specs += [pl.BlockSpec((tm, tn), lambda i, j: (i % ns, 0)),
                     pl.BlockSpec((tm, tn), lambda i, j: (i % ns, 0))]
        args += [cos, sin]
    if mode == "transposed":
        assert tm % tkv == 0
        out_specs = pl.BlockSpec((tm // tkv, tn, tkv), lambda i, j: (i, j, 0))
        out_shape = jax.ShapeDtypeStruct((T // tkv, N, tkv), out_dtype)
    else:
        out_specs = pl.BlockSpec((tm, tn), lambda i, j: (i, j))
        out_shape = jax.ShapeDtypeStruct((T, N), out_dtype)
    return pl.pallas_call(
        functools.partial(_inproj_kernel, mode=mode, tn=tn, tkv=tkv, eps=NORM_EPS),
        grid=(T // tm, N // tn),
        in_specs=in_specs,
        out_specs=out_specs,
        out_shape=out_shape,
        scratch_shapes=[pltpu.VMEM((tm, D), BF16)],
        compiler_params=_cparams(("arbitrary", "arbitrary")),
        name="inproj_" + mode,
    )(*args)


def _rope_tables(seq, width):
    half = DA_HEAD_DIM // 2
    inv = ROPE_THETA ** (-jnp.arange(half, dtype=F32) / half)
    ang = jnp.arange(seq, dtype=F32)[:, None] * inv[None, :]
    cos = jnp.cos(ang)
    sin = jnp.sin(ang)
    cos_h = jnp.concatenate([cos, cos], axis=-1)
    sin_h = jnp.concatenate([-sin, sin], axis=-1)
    reps = width // DA_HEAD_DIM
    return jnp.tile(cos_h, (1, reps)), jnp.tile(sin_h, (1, reps))


def _attn_kernel(lam_ref, q_ref, k_ref, vt_ref, sw_ref, o_ref, qs_ref, m_ref, l_ref, acc_ref, *, tq, tk):
    qi = pl.program_id(2)
    d = DA_HEAD_DIM
    q = q_ref[...]
    lane = lax.broadcasted_iota(jnp.int32, q.shape, 1)
    zero = jnp.zeros_like(q)
    qs_ref[0:tq, :] = jnp.where(lane < d, q, zero)
    qs_ref[tq:2 * tq, :] = jnp.where(lane >= d, q, zero)
    m_ref[...] = jnp.full(m_ref.shape, NEG_BIG, F32)
    l_ref[...] = jnp.zeros(l_ref.shape, F32)
    acc_ref[...] = jnp.zeros(acc_ref.shape, F32)

    n_diag = tq // tk
    n_full = qi * n_diag

    n_strips = 2 * tq // tk

    def strip_scores(k, si, on_diagonal):
        s = lax.dot_general(k, qs_ref[si * tk:(si + 1) * tk, :], (((1,), (1,)), ((), ())),
                            preferred_element_type=F32)
        if on_diagonal:
            kpos = lax.broadcasted_iota(jnp.int32, s.shape, 0)
            qpos = lax.broadcasted_iota(jnp.int32, s.shape, 1)
            s = jnp.where(kpos <= qpos, s, NEG_BIG)
        return s

    def strip_update(vt, si, s):
        lanes = slice(si * tk, (si + 1) * tk)
        m_old = m_ref[:, lanes]
        m_new = jnp.maximum(m_old, jnp.max(s, axis=0, keepdims=True))
        alpha = jnp.exp2(m_old - m_new)
        p = jnp.exp2(s - m_new)
        l_ref[:, lanes] = alpha * l_ref[:, lanes] + jnp.sum(p, axis=0, keepdims=True)
        acc_ref[:, lanes] = alpha * acc_ref[:, lanes] + jnp.dot(vt, p.astype(BF16), preferred_element_type=F32)
        m_ref[:, lanes] = m_new

    def kv_block(j, work):
        k = k_ref[pl.ds(pl.multiple_of(j * tk, tk), tk), :]
        vt = vt_ref[j]
        pending = [strip_scores(k, si, dg) for si, dg in work[:ATTN_LOOKAHEAD]]
        for n, (si, _) in enumerate(work):
            if n + ATTN_LOOKAHEAD < len(work):
                nsi, ndg = work[n + ATTN_LOOKAHEAD]
                pending.append(strip_scores(k, nsi, ndg))
            strip_update(vt, si, pending[n])

    def full_step(j, c):
        kv_block(j, [(si, False) for si in range(n_strips)])
        return c

    lax.fori_loop(0, n_full, full_step, 0)

    for c in range(n_diag):
        kv_block(n_full + c, [(si, si % n_diag == c) for si in range(n_strips) if si % n_diag >= c])

    o = acc_ref[...] / l_ref[...]
    od = o[:, 0:tq] - lam_ref[0] * o[:, tq:2 * tq]
    ms = jnp.mean(od * od, axis=0, keepdims=True)
    on = (od * lax.rsqrt(ms + SUBLN_EPS) * sw_ref[...]) * (1.0 - LAM_INIT)
    o_ref[...] = on.T.astype(o_ref.dtype)


def _diff_attention(zqk, vt, lam, subln_w, *, batch, seq, n_heads, tq, tk):
    T = zqk.shape[0]
    hw = 2 * DA_HEAD_DIM
    nq = seq // tq
    nk = seq // tk
    sw_b = jnp.broadcast_to(subln_w.reshape(hw, 1), (hw, tq)).astype(F32)
    return pl.pallas_call(
        functools.partial(_attn_kernel, tq=tq, tk=tk),
        grid=(batch, n_heads, nq),
        in_specs=[
            pl.BlockSpec(memory_space=pltpu.SMEM),
            pl.BlockSpec((tq, hw), lambda b, h, i: (b * nq + i, h)),
            pl.BlockSpec((seq, hw), lambda b, h, i: (b, n_heads + h)),
            pl.BlockSpec((nk, hw, tk), lambda b, h, i: (b, h, 0)),
            pl.BlockSpec((hw, tq), lambda b, h, i: (0, 0)),
        ],
        out_specs=pl.BlockSpec((tq, hw), lambda b, h, i: (b * nq + i, h)),
        out_shape=jax.ShapeDtypeStruct((T, n_heads * hw), BF16),
        scratch_shapes=[
            pltpu.VMEM((2 * tq, hw), BF16),
            pltpu.VMEM((1, 2 * tq), F32),
            pltpu.VMEM((1, 2 * tq), F32),
            pltpu.VMEM((hw, 2 * tq), F32),
        ],
        compiler_params=_cparams(("arbitrary", "arbitrary", "arbitrary")),
        name="diff_attn",
    )(lam.reshape(1), zqk, zqk, vt, sw_b)


RW_CHUNK = 64
RW_PAIR = 2 * RW_HEAD_DIM
RW_INV_BLOCK = 16


def _bmm(a, b):
    return jnp.einsum("bij,bjk->bik", a.astype(BF16), b.astype(BF16), preferred_element_type=F32)


def _bmm_nt(a, b):
    return jnp.einsum("bik,bjk->bij", a.astype(BF16), b.astype(BF16), preferred_element_type=F32)


def _bmm_tn(a, b):
    return jnp.einsum("bti,btj->bij", a.astype(BF16), b.astype(BF16), preferred_element_type=F32)


def _mm_split(x, e):
    hi = x.astype(BF16)
    lo = (x - hi.astype(F32)).astype(BF16)
    return (jnp.dot(hi, e, preferred_element_type=F32) + jnp.dot(lo, e, preferred_element_type=F32))


def _sigmoid(x):
    return 1.0 / (1.0 + jnp.exp(-x))


def _softplus(x):
    return jnp.maximum(x, 0.0) + jnp.log(1.0 + jnp.exp(-jnp.abs(x)))


def _unit_lower_inverse(a, eye, diag_blk):
    ad = jnp.where(diag_blk, a, 0.0)
    ao = a - ad
    a2 = _bmm(ad, ad)
    a4 = _bmm(a2, a2)
    a8 = _bmm(a4, a4)
    td = eye + ad
    td = td + _bmm(td, a2)
    td = td + _bmm(td, a4)
    td = td + _bmm(td, a8)
    n1 = _bmm(td, ao)
    n2 = _bmm(n1, n1)
    x = td + _bmm(n2, td)
    return x + _bmm(n1, x)


def _rwkv_kernel(zr_ref, zk_ref, zv_ref, zwa_ref, zg_ref, mur_ref, muk_ref, muv_ref, muwa_ref, mug_ref,
                 vec_ref, ww_ref, wa_ref, gup_ref, o_ref,
                 cr_ref, ck_ref, cv_ref, cwa_ref, cg_ref, state_ref, *, L, PW):
    t = pl.program_id(2)
    C = RW_CHUNK
    PL = RW_PAIR
    NC = L // C

    @pl.when(t == 0)
    def _reset():
        for c_ref in (cr_ref, ck_ref, cv_ref, cwa_ref, cg_ref):
            c_ref[...] = jnp.zeros(c_ref.shape, F32)
        state_ref[...] = jnp.zeros(state_ref.shape, F32)

    def shift_mix(z_ref, mu_ref, c_ref):
        z = z_ref[...]
        row = lax.broadcasted_iota(jnp.int32, z.shape, 0)
        zprev = jnp.where(row == 0, c_ref[0:1, :], pltpu.roll(z, 1, 0))
        c_ref[0:1, :] = z[L - 1:L, :]
        return z + (zprev - z) * mu_ref[...]

    r = shift_mix(zr_ref, mur_ref, cr_ref)
    k = shift_mix(zk_ref, muk_ref, ck_ref)
    v = shift_mix(zv_ref, muv_ref, cv_ref)
    zwa = shift_mix(zwa_ref, muwa_ref, cwa_ref)
    zg = shift_mix(zg_ref, mug_ref, cg_ref)

    ri = lax.broadcasted_iota(jnp.int32, (PL, PL), 0)
    ci = lax.broadcasted_iota(jnp.int32, (PL, PL), 1)
    eye = (ri == ci).astype(F32)
    strict = ci < ri
    incl = ci <= ri
    diag_blk = (ri // RW_INV_BLOCK) == (ci // RW_INV_BLOCK)
    seg_ones = ((ri // RW_HEAD_DIM) == (ci // RW_HEAD_DIM)).astype(BF16)
    rc = lax.broadcasted_iota(jnp.int32, (C, C), 0)
    cc = lax.broadcasted_iota(jnp.int32, (C, C), 1)
    tri_incl = (cc <= rc).astype(BF16)
    head0 = lax.broadcasted_iota(jnp.int32, (C, PL), 1) < RW_HEAD_DIM
    lanes = [slice(pi * PL, (pi + 1) * PL) for pi in range(PW)]
    rows = [slice(c * C, (c + 1) * C) for c in range(NC)]

    def seg_sum(x):
        return jnp.concatenate([_mm_split(x[:, sl], seg_ones) for sl in lanes], axis=1)

    w0, a0, k_k, k_a, r_k, lnx_w, lnx_b = (vec_ref[i:i + 1, :] for i in range(7))
    w_pre = w0 + jnp.dot(jnp.tanh(zwa).astype(BF16), ww_ref[...], preferred_element_type=F32)
    logdec = -jnp.exp(-_softplus(-w_pre) - 0.5)
    a = _sigmoid(a0 + jnp.dot(zwa.astype(BF16), wa_ref[...], preferred_element_type=F32))
    g = jnp.dot(_sigmoid(zg).astype(BF16), gup_ref[...], preferred_element_type=F32)
    kk = k * k_k
    kkn = kk / jnp.maximum(jnp.sqrt(seg_sum(kk * kk)), 1e-12)
    kf = k * (1.0 + (a - 1.0) * k_a)
    a_s = -kkn
    b_s = kkn * a
    bonus = seg_sum(r * kf * r_k) * v

    ld_hi = logdec.astype(BF16)
    ld_lo = (logdec - ld_hi.astype(F32)).astype(BF16)
    cum_c = [jnp.dot(tri_incl, ld_hi[rs], preferred_element_type=F32)
             + jnp.dot(tri_incl, ld_lo[rs], preferred_element_type=F32) for rs in rows]
    tot_c = [cu[C - 1:C, :] for cu in cum_c]
    cum = jnp.concatenate(cum_c, axis=0)
    tot = jnp.concatenate([jnp.broadcast_to(tc, (C, tc.shape[1])) for tc in tot_c], axis=0)
    p_inv = jnp.exp(-cum)
    p_end = jnp.exp(tot - cum)

    def stack(x):
        out = []
        for rs in rows:
            for sl in lanes:
                blk = x[rs, sl]
                out.append(jnp.concatenate([jnp.where(head0, blk, 0.0), jnp.where(head0, 0.0, blk)], axis=0))
        return jnp.stack(out, axis=0).astype(BF16)

    rt_s = stack(r * jnp.exp(cum))
    at_s = stack(a_s * jnp.exp(cum - logdec))
    kt_s = stack(kf * p_inv)
    bt_s = stack(b_s * p_inv)
    kh_s = stack(kf * p_end)
    bh_s = stack(b_s * p_end)
    v_s = stack(v)

    m1 = _bmm_nt(jnp.concatenate([at_s, rt_s], axis=1), jnp.concatenate([bt_s, kt_s], axis=1))
    a_ab = jnp.where(strict, m1[:, 0:PL, 0:PL], 0.0)
    a_ak = jnp.where(strict, m1[:, 0:PL, PL:2 * PL], 0.0)
    a_rb = jnp.where(incl, m1[:, PL:2 * PL, 0:PL], 0.0)
    a_rk = jnp.where(incl, m1[:, PL:2 * PL, PL:2 * PL], 0.0)
    tinv = _unit_lower_inverse(a_ab, eye, diag_blk)
    wu = _bmm(tinv, jnp.concatenate([at_s, _bmm(a_ak, v_s).astype(BF16)], axis=2))

    s = state_ref[...]
    us, s0s = [], []
    for c in range(NC):
        sel = slice(c * PW, (c + 1) * PW)
        sb = s.astype(BF16)
        u = _bmm_nt(wu[sel, :, 0:PL], sb) + wu[sel, :, PL:2 * PL]
        us.append(u)
        s0s.append(sb)
        p_tot = jnp.stack([jnp.exp(tot_c[c][:, sl]) for sl in lanes], axis=0)
        s = s * p_tot + _bmm_tn(jnp.concatenate([u.astype(BF16), v_s[sel]], axis=1),
                                jnp.concatenate([bh_s[sel], kh_s[sel]], axis=1))
    state_ref[...] = s

    u_all = jnp.concatenate(us, axis=0)
    s0_all = jnp.concatenate(s0s, axis=0)
    y2 = _bmm_nt(rt_s, s0_all) + _bmm(a_rb, u_all) + _bmm(a_rk, v_s)
    y2 = y2[:, 0:C, :] + y2[:, C:2 * C, :]
    y = jnp.concatenate([jnp.concatenate([y2[c * PW + pi] for pi in range(PW)], axis=1) for c in range(NC)], axis=0)

    inv_n = 1.0 / RW_HEAD_DIM
    mean = seg_sum(y) * inv_n
    yc = y - mean
    var = seg_sum(yc * yc) * inv_n
    yn = yc * lax.rsqrt(var + RW_GN_EPS) * lnx_w + lnx_b
    o_ref[...] = ((yn + bonus) * g).astype(o_ref.dtype)


def _rwkv7(zrw, mu, w0, w_up, a0, a_up, g_up, k_k, k_a, r_k, lnx_w, lnx_b, *, batch, seq, n_heads, L, PW):
    T = zrw.shape[0]
    HC = n_heads * RW_HEAD_DIM
    PWL = PW * RW_PAIR
    assert HC % PWL == 0 and seq % L == 0 and L % RW_CHUNK == 0
    ng = HC // PWL
    nt = seq // L
    lora_w = RW_DECAY_RANK + RW_ICLR_RANK
    assert lora_w == V7X_LANES and RW_GATE_RANK == V7X_LANES
    vecs = jnp.stack([w0, a0, k_k, k_a, r_k.reshape(HC), lnx_w, lnx_b, jnp.zeros((HC,), F32)], axis=0)
    ww = jnp.concatenate([w_up, jnp.zeros((RW_ICLR_RANK, HC), F32)], axis=0).astype(BF16)
    wa = jnp.concatenate([jnp.zeros((RW_DECAY_RANK, HC), F32), a_up], axis=0).astype(BF16)
    gw = g_up.astype(BF16)
    mu2 = mu.reshape(1, -1)
    cb = HC // PWL
    wa_blk = 3 * HC // V7X_LANES
    zrow = lambda b, g, t: b * nt + t
    big = lambda sec: pl.BlockSpec((L, PWL), lambda b, g, t: (zrow(b, g, t), sec * cb + g))
    small = lambda off: pl.BlockSpec((L, V7X_LANES), lambda b, g, t: (zrow(b, g, t), wa_blk + off))
    mu_big = lambda sec: pl.BlockSpec((1, PWL), lambda b, g, t: (0, sec * cb + g))
    mu_small = lambda off: pl.BlockSpec((1, V7X_LANES), lambda b, g, t: (0, wa_blk + off))
    wspec = pl.BlockSpec((V7X_LANES, PWL), lambda b, g, t: (0, g))
    return pl.pallas_call(
        functools.partial(_rwkv_kernel, L=L, PW=PW),
        grid=(batch, ng, nt),
        in_specs=[big(0), big(1), big(2), small(0), small(1),
                  mu_big(0), mu_big(1), mu_big(2), mu_small(0), mu_small(1),
                  pl.BlockSpec((8, PWL), lambda b, g, t: (0, g)),
                  wspec, wspec, wspec],
        out_specs=pl.BlockSpec((L, PWL), lambda b, g, t: (zrow(b, g, t), g)),
        out_shape=jax.ShapeDtypeStruct((T, HC), BF16),
        scratch_shapes=[pltpu.VMEM((8, PWL), F32), pltpu.VMEM((8, PWL), F32), pltpu.VMEM((8, PWL), F32),
                        pltpu.VMEM((8, V7X_LANES), F32), pltpu.VMEM((8, V7X_LANES), F32),
                        pltpu.VMEM((PW, RW_PAIR, RW_PAIR), F32)],
        compiler_params=_cparams(("arbitrary", "arbitrary", "arbitrary")),
        name="rwkv7_chunked",
    )(zrw, zrw, zrw, zrw, zrw, mu2, mu2, mu2, mu2, mu2, vecs, ww, wa, gw)


def _route(lg):
    lane_i = lax.broadcasted_iota(jnp.int32, lg.shape, 1)
    lane = lane_i.astype(F32)
    is_g = lane_i < N_GROUPS
    mg = jnp.max(jnp.where(is_g, lg, NEG_BIG), axis=-1, keepdims=True)
    eg = jnp.exp(jnp.where(is_g, lg - mg, NEG_BIG))
    pg = eg / jnp.sum(eg, axis=-1, keepdims=True)
    p_g = jnp.max(pg, axis=-1, keepdims=True)
    g_sel = jnp.min(jnp.where(is_g & (pg == p_g), lane, 1e9), axis=-1, keepdims=True)
    grp = (jnp.right_shift(lane_i, 3) - 1).astype(F32)
    is_e = (lane_i >= N_GROUPS) & (lane_i < N_GROUPS + N_EXPERTS) & (grp == g_sel)
    me = jnp.max(jnp.where(is_e, lg, NEG_BIG), axis=-1, keepdims=True)
    ee = jnp.exp(jnp.where(is_e, lg - me, NEG_BIG))
    pe = ee / jnp.sum(ee, axis=-1, keepdims=True)
    p1 = jnp.max(jnp.where(is_e, pe, -1.0), axis=-1, keepdims=True)
    i1 = jnp.min(jnp.where(is_e & (pe == p1), lane, 1e9), axis=-1, keepdims=True)
    rest = is_e & (lane != i1)
    p2 = jnp.max(jnp.where(rest, pe, -1.0), axis=-1, keepdims=True)
    i2 = jnp.min(jnp.where(rest & (pe == p2), lane, 1e9), axis=-1, keepdims=True)
    den = p1 + p2
    eid = jnp.where(lane_i == 0, i1 - N_GROUPS, jnp.where(lane_i == 1, i2 - N_GROUPS, 0.0)).astype(jnp.int32)
    gate = jnp.where(lane_i == 0, p_g * p1 / den, jnp.where(lane_i == 1, p_g * p2 / den, 0.0))
    return eid, gate


def _pack_rows(x):
    half = x.shape[1] // 2
    hi = lax.bitcast_convert_type(x[:, :half].astype(BF16).astype(F32), jnp.uint32)
    lo = lax.bitcast_convert_type(x[:, half:].astype(BF16).astype(F32), jnp.uint32)
    return hi | (lo >> 16)


def _unpack_rows(w):
    hi = lax.bitcast_convert_type(w & jnp.uint32(0xFFFF0000), F32)
    lo = lax.bitcast_convert_type(w << 16, F32)
    return jnp.concatenate([hi, lo], axis=1)


def _store_lines(ref, x, rows, first=0, stride=None):
    nl = x.shape[1] // V7X_LANES
    for s in range(nl):
        ref[pl.ds(first + s, rows, stride=stride or nl), :] = x[:, s * V7X_LANES:(s + 1) * V7X_LANES]


def _load_lines(ref, rows, nl, first=0, stride=None):
    return jnp.concatenate([ref[pl.ds(first + s, rows, stride=stride or nl), :] for s in range(nl)], axis=1)


def _outproj_kernel(oda_ref, orw_ref, wt_ref, wb_ref, x_ref, n2_ref, rhi_ref, rlo_ref, rb_ref,
                    h_ref, xn_ref, eid_ref, gate_ref, *, tm, eps):
    acc = (jnp.dot(oda_ref[...], wt_ref[...], preferred_element_type=F32)
           + jnp.dot(orw_ref[...], wb_ref[...], preferred_element_type=F32))
    h = x_ref[...] + acc
    h_ref[...] = h
    ms = jnp.mean(h * h, axis=-1, keepdims=True)
    xn = h * lax.rsqrt(ms + eps) * n2_ref[...]
    _store_lines(xn_ref, _pack_rows(xn), tm)
    hi = xn.astype(BF16)
    lo = (xn - hi.astype(F32)).astype(BF16)
    lg = (jnp.dot(hi, rhi_ref[...], preferred_element_type=F32)
          + jnp.dot(lo, rhi_ref[...], preferred_element_type=F32)
          + jnp.dot(hi, rlo_ref[...], preferred_element_type=F32)) + rb_ref[...]
    eid, gate = _route(lg)
    eid_ref[...] = eid
    gate_ref[...] = gate


def _outproj_route(oda, orw, w_out, x2, n2, wg, bg, we, be, *, tm):
    T, D = x2.shape
    w1 = oda.shape[1]
    w2 = orw.shape[1]
    nl = D // 2 // V7X_LANES
    wt = w_out[:w1].astype(BF16)
    wb = w_out[w1:].astype(BF16)
    pad = V7X_LANES - N_GROUPS - N_EXPERTS
    wr = jnp.concatenate([wg, we, jnp.zeros((D, pad), F32)], axis=1)
    rb = jnp.concatenate([bg, be, jnp.zeros((pad,), F32)]).reshape(1, V7X_LANES)
    rhi = wr.astype(BF16)
    rlo = (wr - rhi.astype(F32)).astype(BF16)
    const = lambda shape: pl.BlockSpec(shape, lambda i: (0, 0))
    return pl.pallas_call(
        functools.partial(_outproj_kernel, tm=tm, eps=NORM_EPS),
        grid=(T // tm,),
        in_specs=[pl.BlockSpec((tm, w1), lambda i: (i, 0)), pl.BlockSpec((tm, w2), lambda i: (i, 0)),
                  const((w1, D)), const((w2, D)), pl.BlockSpec((tm, D), lambda i: (i, 0)), const((1, D)),
                  const((D, V7X_LANES)), const((D, V7X_LANES)), const((1, V7X_LANES))],
        out_specs=[pl.BlockSpec((tm, D), lambda i: (i, 0)),
                   pl.BlockSpec((tm * nl, V7X_LANES), lambda i: (i, 0)),
                   pl.BlockSpec((tm, V7X_LANES), lambda i: (i, 0)),
                   pl.BlockSpec((tm, V7X_LANES), lambda i: (i, 0))],
        out_shape=[jax.ShapeDtypeStruct((T, D), F32),
                   jax.ShapeDtypeStruct((T * nl, V7X_LANES), jnp.uint32),
                   jax.ShapeDtypeStruct((T, V7X_LANES), jnp.int32),
                   jax.ShapeDtypeStruct((T, V7X_LANES), F32)],
        compiler_params=_cparams(("arbitrary",)),
        name="outproj_route",
    )(oda, orw, wt, wb, x2, n2.reshape(1, D), rhi, rlo, rb)


MOE_DMA_UNROLL = 8


def _moe_kernel(blk_e_ref, nxt_e_ref, par_ref, nused_ref, tok_ref, tokn_ref, dst_ref,
                xn_hbm, w1_hbm, w3_hbm, w2_hbm, ys_hbm,
                xbuf, ybuf, wf1, wf3, wf2, w1b, w3b, w2b, sem_in, sem_out, sem_w, *, bm, nl, n_slots):
    i = pl.program_id(0)
    n_used = nused_ref[0]
    slot = i % 2

    def gather(ids_ref, dst_slot):
        def body(r, c):
            src = xn_hbm.at[pl.ds(pl.multiple_of(ids_ref[0, r] * nl, nl), nl)]
            pltpu.make_async_copy(src, xbuf.at[dst_slot, pl.ds(pl.multiple_of(r * nl, nl), nl)],
                                  sem_in.at[dst_slot]).start()
            return c
        lax.fori_loop(0, bm, body, 0, unroll=MOE_DMA_UNROLL)

    def wait_gather(src_slot):
        pltpu.make_async_copy(xn_hbm.at[pl.ds(0, bm * nl)], xbuf.at[src_slot], sem_in.at[src_slot]).wait()

    def scatter():
        def body(r, c):
            dst = ys_hbm.at[pl.ds(pl.multiple_of(dst_ref[0, r] * nl, nl), nl)]
            pltpu.make_async_copy(ybuf.at[pl.ds(pl.multiple_of(r * nl, nl), nl)], dst, sem_out).start()
            return c
        lax.fori_loop(0, bm, body, 0, unroll=MOE_DMA_UNROLL)

    def wait_scatter():
        pltpu.make_async_copy(ybuf, ys_hbm.at[pl.ds(0, bm * nl)], sem_out).wait()

    def weight_copies(e, wslot):
        return (pltpu.make_async_copy(w1_hbm.at[e], wf1.at[wslot], sem_w.at[wslot]),
                pltpu.make_async_copy(w3_hbm.at[e], wf3.at[wslot], sem_w.at[wslot]),
                pltpu.make_async_copy(w2_hbm.at[e], wf2.at[wslot], sem_w.at[wslot]))

    @pl.when(i == 0)
    def _prologue():
        for cp in weight_copies(blk_e_ref[0], par_ref[0]):
            cp.start()
        gather(tok_ref, 0)
        ybuf[...] = jnp.zeros(ybuf.shape, ybuf.dtype)
        spare = pltpu.make_async_copy(ybuf, ys_hbm.at[pl.ds(n_slots * nl, bm * nl)], sem_out)
        spare.start()
        spare.wait()

    @pl.when(i + 1 < n_used)
    def _prefetch():
        gather(tokn_ref, 1 - slot)

    @pl.when(i < n_used)
    def _active():
        e = blk_e_ref[i]
        first = jnp.logical_or(i == 0, e != blk_e_ref[jnp.maximum(i - 1, 0)])

        @pl.when(first)
        def _new_expert():
            wslot = par_ref[i]
            for cp in weight_copies(e, wslot):
                cp.wait()
            nxt = nxt_e_ref[i]

            @pl.when(nxt >= 0)
            def _():
                for cp in weight_copies(nxt, 1 - wslot):
                    cp.start()

            w1b[...] = wf1[wslot].astype(BF16)
            w3b[...] = wf3[wslot].astype(BF16)
            w2b[...] = wf2[wslot].astype(BF16)

        wait_gather(slot)
        x = _unpack_rows(_load_lines(xbuf.at[slot], bm, nl)).astype(BF16)
        h1 = jnp.dot(x, w1b[...], preferred_element_type=F32)
        h3 = jnp.dot(x, w3b[...], preferred_element_type=F32)
        hh = (h1 * _sigmoid(h1) * h3).astype(BF16)
        y = jnp.dot(hh, w2b[...], preferred_element_type=F32)

        @pl.when(i > 0)
        def _drain_prev():
            wait_scatter()

        _store_lines(ybuf, _pack_rows(y), bm)
        scatter()

        @pl.when(i == n_used - 1)
        def _drain_last():
            wait_scatter()


def _moe_ffn(xn_lines, plan, w1, w3, w2, *, bm, n_slots):
    blk_e, nxt_e, par, n_used, tok_p, dst_p = plan
    E, D, DE = w1.shape
    nl = D // 2 // V7X_LANES
    nb = blk_e.shape[0]
    ids = lambda imap: pl.BlockSpec((None, 1, bm), imap, memory_space=pltpu.SMEM)
    grid_spec = pltpu.PrefetchScalarGridSpec(
        num_scalar_prefetch=4,
        grid=(nb,),
        in_specs=[
            ids(lambda i, *_: (i, 0, 0)),
            ids(lambda i, *_: (jnp.minimum(i + 1, nb - 1), 0, 0)),
            ids(lambda i, *_: (i, 0, 0)),
            pl.BlockSpec(memory_space=pl.ANY),
            pl.BlockSpec(memory_space=pl.ANY),
            pl.BlockSpec(memory_space=pl.ANY),
            pl.BlockSpec(memory_space=pl.ANY),
        ],
        out_specs=pl.BlockSpec(memory_space=pl.ANY),
        scratch_shapes=[pltpu.VMEM((2, bm * nl, V7X_LANES), jnp.uint32), pltpu.VMEM((bm * nl, V7X_LANES), jnp.uint32),
                        pltpu.VMEM((2, D, DE), F32), pltpu.VMEM((2, D, DE), F32), pltpu.VMEM((2, DE, D), F32),
                        pltpu.VMEM((D, DE), BF16), pltpu.VMEM((D, DE), BF16), pltpu.VMEM((DE, D), BF16),
                        pltpu.SemaphoreType.DMA((2,)), pltpu.SemaphoreType.DMA, pltpu.SemaphoreType.DMA((2,))],
    )
    tok3 = tok_p.reshape(nb, 1, bm)
    return pl.pallas_call(
        functools.partial(_moe_kernel, bm=bm, nl=nl, n_slots=n_slots),
        grid_spec=grid_spec,
        out_shape=jax.ShapeDtypeStruct(((n_slots + bm) * nl, V7X_LANES), jnp.uint32),
        compiler_params=_cparams(("arbitrary",)),
        name="moe_ffn",
    )(blk_e, nxt_e, par, n_used, tok3, tok3, dst_p.reshape(nb, 1, bm), xn_lines, w1, w3, w2)


def _moe_plan(eid, *, bm):
    T = eid.shape[0]
    M = T * TOP_K
    E = N_EXPERTS
    i32 = jnp.int32
    e_flat = eid.reshape(M)
    order = jnp.argsort(e_flat).astype(i32)
    experts = jnp.arange(E, dtype=i32)
    counts = jnp.sum((e_flat[:, None] == experts[None, :]).astype(i32), axis=0)
    start = jnp.cumsum(counts) - counts
    padded = (counts + bm - 1) // bm * bm
    pend = jnp.cumsum(padded)
    pstart = pend - padded
    nb = M // bm + E
    blk = jnp.arange(nb, dtype=i32)
    blk_start = blk * bm
    n_used = (pend[-1] // bm).astype(i32)
    blk_e = jnp.minimum(jnp.sum((blk_start[:, None] >= pend[None, :]).astype(i32), axis=1), E - 1)
    blk_e = jnp.where(blk < n_used, blk_e, blk_e[jnp.maximum(n_used - 1, 0)])
    off = blk_start - pstart[blk_e]
    base = start[blk_e] + off
    valid = jnp.where(blk < n_used, jnp.clip(counts[blk_e] - off, 0, bm), 0)
    r = jnp.arange(bm, dtype=i32)
    slot_p = order[jnp.clip(base[:, None] + r[None, :], 0, M - 1)]
    tok_p = slot_p // TOP_K
    dst_p = jnp.where(r[None, :] < valid[:, None], slot_p, M + r[None, :])
    seg_first = jnp.concatenate([jnp.ones((1,), bool), blk_e[1:] != blk_e[:-1]])
    par = (jnp.cumsum(seg_first.astype(i32)) - 1) % 2
    later = (experts[None, :] > experts[:, None]) & (counts[None, :] > 0)
    nxt_of = jnp.min(jnp.where(later, experts[None, :], E), axis=1)
    nxt_e = jnp.where(nxt_of == E, -1, nxt_of)[blk_e]
    return blk_e, nxt_e.astype(i32), par.astype(i32), n_used.reshape(1), tok_p, dst_p


def _combine_kernel(h_ref, ys_ref, gate_ref, fw_ref, o_ref, *, tm, nl, eps):
    gate = gate_ref[...]
    acc = h_ref[...]
    for j in range(TOP_K):
        yj = _unpack_rows(_load_lines(ys_ref, tm, nl, first=j * nl, stride=TOP_K * nl))
        acc = acc + yj * gate[:, j:j + 1]
    ms = jnp.mean(acc * acc, axis=-1, keepdims=True)
    o_ref[...] = acc * lax.rsqrt(ms + eps) * fw_ref[...]


def _combine(h, ys, gate, fw, *, tm):
    T, D = h.shape
    nl = D // 2 // V7X_LANES
    return pl.pallas_call(
        functools.partial(_combine_kernel, tm=tm, nl=nl, eps=NORM_EPS),
        grid=(T // tm,),
        in_specs=[pl.BlockSpec((tm, D), lambda i: (i, 0)),
                  pl.BlockSpec((tm * TOP_K * nl, V7X_LANES), lambda i: (i, 0)),
                  pl.BlockSpec((tm, V7X_LANES), lambda i: (i, 0)),
                  pl.BlockSpec((1, D), lambda i: (0, 0))],
        out_specs=pl.BlockSpec((tm, D), lambda i: (i, 0)),
        out_shape=jax.ShapeDtypeStruct((T, D), F32),
        compiler_params=_cparams(("arbitrary",)),
        name="combine_norm",
    )(h, ys, gate, fw.reshape(1, D))


def _tiles(T, seq, D):
    pick = lambda n, prefs: next(p for p in prefs if n % p == 0)
    return dict(
        tm_in=pick(seq, (512, 256, 128)),
        tn_da=512,
        tq=pick(seq, (1024, 512, 256, 128)),
        tk=pick(seq, (256, 128)),
        rw_L=pick(seq, (128, 64)),
        rw_PW=8,
        tm_out=pick(T, (256, 128)),
        bm=256,
        tm_cmb=pick(T, (256, 128)),
    )


def kernel(x, norm1_w, w_in, lam_q1, lam_k1, lam_q2, lam_k2, subln_w, rw_mu, rw_w0, rw_w_up, rw_a0, rw_a_up, rw_g_up, rw_k_k, rw_k_a, rw_r_k, rw_lnx_w, rw_lnx_b, w_out, norm2_w, router_group_w, router_group_b, router_expert_w, router_expert_b, moe_w1, moe_w3, moe_w2, final_norm_w):
    B, S, D = x.shape
    T = B * S
    depth = w_in.shape[0]
    rw_heads = rw_w0.shape[1] // RW_HEAD_DIM
    rw_cols = rw_mu.shape[1]
    da_cols = w_in.shape[2] - rw_cols
    da_width = da_cols // 3
    da_heads = da_width // (2 * DA_HEAD_DIM)
    tl = _tiles(T, S, D)
    cos, sin = _rope_tables(S, tl["tn_da"])
    q_scale = DA_HEAD_DIM ** -0.5 * math.log2(math.e)
    colscale = jnp.concatenate([jnp.full((da_width,), q_scale, F32), jnp.ones((da_width,), F32)])

    h = x.reshape(T, D)
    for l in range(depth):
        assert l == 0, "lam_init is specialised to the first layer"
        w_qk = (w_in[l][:, :2 * da_width] * colscale).astype(BF16)
        w_v = w_in[l][:, 2 * da_width:da_cols].T.astype(BF16)
        w_rw = w_in[l][:, da_cols:].astype(BF16)
        zqk = _inproj(h, norm1_w[l], w_qk, seq=S, tm=tl["tm_in"], tn=tl["tn_da"], out_dtype=BF16,
                      mode="rope", cos=cos, sin=sin)
        vt = _inproj(h, norm1_w[l], w_v, seq=S, tm=tl["tm_in"], tn=tl["tn_da"], out_dtype=BF16,
                     mode="transposed", tkv=tl["tk"])
        zrw = _inproj(h, norm1_w[l], w_rw, seq=S, tm=tl["tm_in"], tn=rw_cols // 2, out_dtype=F32)
        lam = (jnp.exp(jnp.sum(lam_q1[l] * lam_k1[l])) - jnp.exp(jnp.sum(lam_q2[l] * lam_k2[l])) + LAM_INIT)
        o_da = _diff_attention(zqk, vt, lam.astype(F32), subln_w[l], batch=B, seq=S, n_heads=da_heads,
                               tq=tl["tq"], tk=tl["tk"])
        o_rw = _rwkv7(zrw, rw_mu[l], rw_w0[l], rw_w_up[l], rw_a0[l], rw_a_up[l], rw_g_up[l], rw_k_k[l], rw_k_a[l],
                      rw_r_k[l], rw_lnx_w[l], rw_lnx_b[l], batch=B, seq=S, n_heads=rw_heads,
                      L=tl["rw_L"], PW=tl["rw_PW"])
        h, xn_lines, eid, gate = _outproj_route(o_da, o_rw, w_out[l], h, norm2_w[l], router_group_w[l],
                                                router_group_b[l], router_expert_w[l], router_expert_b[l],
                                                tm=tl["tm_out"])
        plan = _moe_plan(eid[:, :TOP_K], bm=tl["bm"])
        ys = _moe_ffn(xn_lines, plan, moe_w1[l], moe_w3[l], moe_w2[l], bm=tl["bm"], n_slots=T * TOP_K)
        assert depth == 1, "the final norm is fused into the last layer's combine"
        out = _combine(h, ys, gate, final_norm_w, tm=tl["tm_cmb"])
    return out.reshape(B, S, D)
```

```python
import functools
import math

import jax
import jax.numpy as jnp
from jax import lax
from jax.experimental import pallas as pl
from jax.experimental.pallas import tpu as pltpu

F32 = jnp.float32
BF16 = jnp.bfloat16

DA_HEAD_DIM = 64
RW_HEAD_DIM = 64
RW_DECAY_RANK = 64
RW_ICLR_RANK = 64
RW_GATE_RANK = 128
ROPE_THETA = 10000.0
N_GROUPS = 8
EXPERTS_PER_GROUP = 8
N_EXPERTS = N_GROUPS * EXPERTS_PER_GROUP
TOP_K = 2
NORM_EPS = 1e-6
SUBLN_EPS = 1e-5
RW_GN_EPS = 64e-5
LAM_INIT = 0.8 - 0.6 * math.exp(-0.3 * 0)

V7X_LANES = 128
V7X_VMEM_LIMIT = 56 * 1024 * 1024
NEG_BIG = -1e30
ATTN_LOOKAHEAD = 5


def _cparams(sem):
    return pltpu.CompilerParams(dimension_semantics=sem, vmem_limit_bytes=V7X_VMEM_LIMIT)


def _inproj_kernel(*refs, mode, tn, tkv, eps):
    if mode == "rope":
        x_ref, nw_ref, w_ref, cos_ref, sin_ref, o_ref, xn_ref = refs
    else:
        x_ref, nw_ref, w_ref, o_ref, xn_ref = refs
    j = pl.program_id(1)

    @pl.when(j == 0)
    def _norm():
        x = x_ref[...]
        ms = jnp.mean(x * x, axis=-1, keepdims=True)
        xn_ref[...] = (x * lax.rsqrt(ms + eps) * nw_ref[...]).astype(BF16)

    if mode == "transposed":
        acc_t = lax.dot_general(w_ref[...], xn_ref[...], (((1,), (1,)), ((), ())), preferred_element_type=F32)
        for c in range(acc_t.shape[1] // tkv):
            o_ref[c] = acc_t[:, c * tkv:(c + 1) * tkv].astype(o_ref.dtype)
        return

    acc = jnp.dot(xn_ref[...], w_ref[...], preferred_element_type=F32)
    if mode == "rope":
        cos = cos_ref[...]
        sin = sin_ref[...]
        for c in range(tn // V7X_LANES):
            cols = slice(c * V7X_LANES, (c + 1) * V7X_LANES)
            blk = acc[:, cols]
            o_ref[:, cols] = (blk * cos + pltpu.roll(blk, V7X_LANES // 2, 1) * sin).astype(o_ref.dtype)
    else:
        o_ref[...] = acc.astype(o_ref.dtype)


def _inproj(x2, nw, w, *, seq, tm, tn, out_dtype, mode="plain", cos=None, sin=None, tkv=None):
    T, D = x2.shape
    N = w.shape[0] if mode == "transposed" else w.shape[1]
    assert T % tm == 0 and N % tn == 0 and seq % tm == 0
    w_spec = (pl.BlockSpec((tn, D), lambda i, j: (j, 0)) if mode == "transposed"
              else pl.BlockSpec((D, tn), lambda i, j: (0, j)))
    in_specs = [
        pl.BlockSpec((tm, D), lambda i, j: (i, 0)),
        pl.BlockSpec((1, D), lambda i, j: (0, 0)),
        w_spec,
    ]
    args = [x2, nw.reshape(1, D), w]
    if mode == "rope":
        ns = seq // tm
        in_specs += [pl.BlockSpec((tm, V7X_LANES), lambda i, j: (i % ns, 0)),
                     pl.BlockSpec((tm, V7X_LANES), lambda i, j: (i % ns, 0))]
        args += [cos, sin]
    if mode == "transposed":
        assert tm % tkv == 0
        out_specs = pl.BlockSpec((tm // tkv, tn, tkv), lambda i, j: (i, j, 0))
        out_shape = jax.ShapeDtypeStruct((T // tkv, N, tkv), out_dtype)
    else:
        out_specs = pl.BlockSpec((tm, tn), lambda i, j: (i, j))
        out_shape = jax.ShapeDtypeStruct((T, N), out_dtype)
    return pl.pallas_call(
        functools.partial(_inproj_kernel, mode=mode, tn=tn, tkv=tkv, eps=NORM_EPS),
        grid=(T // tm, N // tn),
        in_specs=in_specs,
        out_specs=out_specs,
        out_shape=out_shape,
        scratch_shapes=[pltpu.VMEM((tm, D), BF16)],
        compiler_params=_cparams(("arbitrary", "arbitrary")),
        name="inproj_" + mode,
    )(*args)


def _rope_column_order(width):
    half = DA_HEAD_DIM // 2
    blk = jnp.concatenate([jnp.arange(0, half), jnp.arange(2 * half, 3 * half),
                           jnp.arange(half, 2 * half), jnp.arange(3 * half, 4 * half)])
    cols = jnp.arange(width)
    return (cols // V7X_LANES) * V7X_LANES + blk[cols % V7X_LANES]


def _rope_tables(seq):
    half = DA_HEAD_DIM // 2
    inv = ROPE_THETA ** (-jnp.arange(half, dtype=F32) / half)
    ang = jnp.arange(seq, dtype=F32)[:, None] * inv[None, :]
    cos = jnp.cos(ang)
    sin = jnp.sin(ang)
    return jnp.tile(cos, (1, 4)), jnp.concatenate([-sin, -sin, sin, sin], axis=-1)


def _attn_kernel(lam_ref, q_ref, k_ref, vt_ref, sw_ref, o_ref, qs_ref, m_ref, l_ref, acc_ref, *, tq, tk):
    qi = pl.program_id(2)
    d = DA_HEAD_DIM
    q = q_ref[...]
    lane = lax.broadcasted_iota(jnp.int32, q.shape, 1)
    comp1 = (lane % d) < (d // 2)
    zero = jnp.zeros_like(q)
    qs_ref[0:tq, :] = jnp.where(comp1, q, zero)
    qs_ref[tq:2 * tq, :] = jnp.where(comp1, zero, q)
    m_ref[...] = jnp.full(m_ref.shape, NEG_BIG, F32)
    l_ref[...] = jnp.zeros(l_ref.shape, F32)
    acc_ref[...] = jnp.zeros(acc_ref.shape, F32)

    n_diag = tq // tk
    n_full = qi * n_diag

    n_strips = 2 * tq // tk

    def strip_scores(k, si, on_diagonal):
        s = lax.dot_general(k, qs_ref[si * tk:(si + 1) * tk, :], (((1,), (1,)), ((), ())),
                            preferred_element_type=F32)
        if on_diagonal:
            kpos = lax.broadcasted_iota(jnp.int32, s.shape, 0)
            qpos = lax.broadcasted_iota(jnp.int32, s.shape, 1)
            s = jnp.where(kpos <= qpos, s, NEG_BIG)
        return s

    def strip_update(vt, si, s):
        lanes = slice(si * tk, (si + 1) * tk)
        m_old = m_ref[:, lanes]
        m_new = jnp.maximum(m_old, jnp.max(s, axis=0, keepdims=True))
        alpha = jnp.exp2(m_old - m_new)
        p = jnp.exp2(s - m_new)
        l_ref[:, lanes] = alpha * l_ref[:, lanes] + jnp.sum(p, axis=0, keepdims=True)
        acc_ref[:, lanes] = alpha * acc_ref[:, lanes] + jnp.dot(vt, p.astype(BF16), preferred_element_type=F32)
        m_ref[:, lanes] = m_new

    def kv_block(j, work):
        k = k_ref[pl.ds(pl.multiple_of(j * tk, tk), tk), :]
        vt = vt_ref[j]
        pending = [strip_scores(k, si, dg) for si, dg in work[:ATTN_LOOKAHEAD]]
        for n, (si, _) in enumerate(work):
            if n + ATTN_LOOKAHEAD < len(work):
                nsi, ndg = work[n + ATTN_LOOKAHEAD]
                pending.append(strip_scores(k, nsi, ndg))
            strip_update(vt, si, pending[n])

    def full_step(j, c):
        kv_block(j, [(si, False) for si in range(n_strips)])
        return c

    lax.fori_loop(0, n_full, full_step, 0)

    for c in range(n_diag):
        kv_block(n_full + c, [(si, si % n_diag == c) for si in range(n_strips) if si % n_diag >= c])

    o = acc_ref[...] / l_ref[...]
    od = o[:, 0:tq] - lam_ref[0] * o[:, tq:2 * tq]
    ms = jnp.mean(od * od, axis=0, keepdims=True)
    on = (od * lax.rsqrt(ms + SUBLN_EPS) * sw_ref[...]) * (1.0 - LAM_INIT)
    o_ref[...] = on.T.astype(o_ref.dtype)


def _diff_attention(zqk, vt, lam, subln_w, *, batch, seq, n_heads, tq, tk):
    T = zqk.shape[0]
    hw = 2 * DA_HEAD_DIM
    nq = seq // tq
    nk = seq // tk
    sw_b = jnp.broadcast_to(subln_w.reshape(hw, 1), (hw, tq)).astype(F32)
    return pl.pallas_call(
        functools.partial(_attn_kernel, tq=tq, tk=tk),
        grid=(batch, n_heads, nq),
        in_specs=[
            pl.BlockSpec(memory_space=pltpu.SMEM),
            pl.BlockSpec((tq, hw), lambda b, h, i: (b * nq + i, h)),
            pl.BlockSpec((seq, hw), lambda b, h, i: (b, n_heads + h)),
            pl.BlockSpec((nk, hw, tk), lambda b, h, i: (b, h, 0)),
            pl.BlockSpec((hw, tq), lambda b, h, i: (0, 0)),
        ],
        out_specs=pl.BlockSpec((tq, hw), lambda b, h, i: (b * nq + i, h)),
        out_shape=jax.ShapeDtypeStruct((T, n_heads * hw), BF16),
        scratch_shapes=[
            pltpu.VMEM((2 * tq, hw), BF16),
            pltpu.VMEM((1, 2 * tq), F32),
            pltpu.VMEM((1, 2 * tq), F32),
            pltpu.VMEM((hw, 2 * tq), F32),
        ],
        compiler_params=_cparams(("arbitrary", "arbitrary", "arbitrary")),
        name="diff_attn",
    )(lam.reshape(1), zqk, zqk, vt, sw_b)


RW_CHUNK = 64
RW_PAIR = 2 * RW_HEAD_DIM
RW_INV_BLOCK = 16


def _bmm(a, b):
    return jnp.einsum("bij,bjk->bik", a.astype(BF16), b.astype(BF16), preferred_element_type=F32)


def _bmm_nt(a, b):
    return jnp.einsum("bik,bjk->bij", a.astype(BF16), b.astype(BF16), preferred_element_type=F32)


def _bmm_tn(a, b):
    return jnp.einsum("bti,btj->bij", a.astype(BF16), b.astype(BF16), preferred_element_type=F32)


def _mm_split(x, e):
    hi = x.astype(BF16)
    lo = (x - hi.astype(F32)).astype(BF16)
    return (jnp.dot(hi, e, preferred_element_type=F32) + jnp.dot(lo, e, preferred_element_type=F32))


def _sigmoid(x):
    return 1.0 / (1.0 + jnp.exp(-x))


def _softplus(x):
    return jnp.maximum(x, 0.0) + jnp.log(1.0 + jnp.exp(-jnp.abs(x)))


def _unit_lower_inverse(a, eye, diag_blk):
    ad = jnp.where(diag_blk, a, 0.0)
    ao = a - ad
    a2 = _bmm(ad, ad)
    a4 = _bmm(a2, a2)
    a8 = _bmm(a4, a4)
    td = eye + ad
    td = td + _bmm(td, a2)
    td = td + _bmm(td, a4)
    td = td + _bmm(td, a8)
    n1 = _bmm(td, ao)
    n2 = _bmm(n1, n1)
    x = td + _bmm(n2, td)
    return x + _bmm(n1, x)


def _rwkv_kernel(zr_ref, zk_ref, zv_ref, zwa_ref, zg_ref, mur_ref, muk_ref, muv_ref, muwa_ref, mug_ref,
                 vec_ref, ww_ref, wa_ref, gup_ref, o_ref,
                 cr_ref, ck_ref, cv_ref, cwa_ref, cg_ref, state_ref, *, L, PW):
    t = pl.program_id(2)
    C = RW_CHUNK
    PL = RW_PAIR
    NC = L // C

    @pl.when(t == 0)
    def _reset():
        for c_ref in (cr_ref, ck_ref, cv_ref, cwa_ref, cg_ref):
            c_ref[...] = jnp.zeros(c_ref.shape, F32)
        state_ref[...] = jnp.zeros(state_ref.shape, F32)

    def shift_mix(z_ref, mu_ref, c_ref):
        z = z_ref[...]
        row = lax.broadcasted_iota(jnp.int32, z.shape, 0)
        zprev = jnp.where(row == 0, c_ref[0:1, :], pltpu.roll(z, 1, 0))
        c_ref[0:1, :] = z[L - 1:L, :]
        return z + (zprev - z) * mu_ref[...]

    r = shift_mix(zr_ref, mur_ref, cr_ref)
    k = shift_mix(zk_ref, muk_ref, ck_ref)
    v = shift_mix(zv_ref, muv_ref, cv_ref)
    zwa = shift_mix(zwa_ref, muwa_ref, cwa_ref)
    zg = shift_mix(zg_ref, mug_ref, cg_ref)

    ri = lax.broadcasted_iota(jnp.int32, (PL, PL), 0)
    ci = lax.broadcasted_iota(jnp.int32, (PL, PL), 1)
    eye = (ri == ci).astype(F32)
    strict = ci < ri
    incl = ci <= ri
    diag_blk = (ri // RW_INV_BLOCK) == (ci // RW_INV_BLOCK)
    seg_ones = ((ri // RW_HEAD_DIM) == (ci // RW_HEAD_DIM)).astype(BF16)
    rc = lax.broadcasted_iota(jnp.int32, (C, C), 0)
    cc = lax.broadcasted_iota(jnp.int32, (C, C), 1)
    tri_incl = (cc <= rc).astype(BF16)
    head0 = lax.broadcasted_iota(jnp.int32, (C, PL), 1) < RW_HEAD_DIM
    lanes = [slice(pi * PL, (pi + 1) * PL) for pi in range(PW)]
    rows = [slice(c * C, (c + 1) * C) for c in range(NC)]

    def seg_sum(x):
        return jnp.concatenate([_mm_split(x[:, sl], seg_ones) for sl in lanes], axis=1)

    w0, a0, k_k, k_a, r_k, lnx_w, lnx_b = (vec_ref[i:i + 1, :] for i in range(7))
    w_pre = w0 + jnp.dot(jnp.tanh(zwa).astype(BF16), ww_ref[...], preferred_element_type=F32)
    logdec = -jnp.exp(-_softplus(-w_pre) - 0.5)
    a = _sigmoid(a0 + jnp.dot(zwa.astype(BF16), wa_ref[...], preferred_element_type=F32))
    g = jnp.dot(_sigmoid(zg).astype(BF16), gup_ref[...], preferred_element_type=F32)
    kk = k * k_k
    kkn = kk / jnp.maximum(jnp.sqrt(seg_sum(kk * kk)), 1e-12)
    kf = k * (1.0 + (a - 1.0) * k_a)
    a_s = -kkn
    b_s = kkn * a
    bonus = seg_sum(r * kf * r_k) * v

    ld_hi = logdec.astype(BF16)
    ld_lo = (logdec - ld_hi.astype(F32)).astype(BF16)
    cum_c = [jnp.dot(tri_incl, ld_hi[rs], preferred_element_type=F32)
             + jnp.dot(tri_incl, ld_lo[rs], preferred_element_type=F32) for rs in rows]
    tot_c = [cu[C - 1:C, :] for cu in cum_c]
    cum = jnp.concatenate(cum_c, axis=0)
    tot = jnp.concatenate([jnp.broadcast_to(tc, (C, tc.shape[1])) for tc in tot_c], axis=0)
    p_inv = jnp.exp(-cum)
    p_end = jnp.exp(tot - cum)

    def stack(x):
        out = []
        for rs in rows:
            for sl in lanes:
                blk = x[rs, sl]
                out.append(jnp.concatenate([jnp.where(head0, blk, 0.0), jnp.where(head0, 0.0, blk)], axis=0))
        return jnp.stack(out, axis=0).astype(BF16)

    rt_s = stack(r * jnp.exp(cum))
    at_s = stack(a_s * jnp.exp(cum - logdec))
    kt_s = stack(kf * p_inv)
    bt_s = stack(b_s * p_inv)
    kh_s = stack(kf * p_end)
    bh_s = stack(b_s * p_end)
    v_s = stack(v)

    m1 = _bmm_nt(jnp.concatenate([at_s, rt_s], axis=1), jnp.concatenate([bt_s, kt_s], axis=1))
    a_ab = jnp.where(strict, m1[:, 0:PL, 0:PL], 0.0)
    a_ak = jnp.where(strict, m1[:, 0:PL, PL:2 * PL], 0.0)
    a_rb = jnp.where(incl, m1[:, PL:2 * PL, 0:PL], 0.0)
    a_rk = jnp.where(incl, m1[:, PL:2 * PL, PL:2 * PL], 0.0)
    tinv = _unit_lower_inverse(a_ab, eye, diag_blk)
    wu = _bmm(tinv, jnp.concatenate([at_s, _bmm(a_ak, v_s).astype(BF16)], axis=2))

    s = state_ref[...]
    us, s0s = [], []
    for c in range(NC):
        sel = slice(c * PW, (c + 1) * PW)
        sb = s.astype(BF16)
        u = _bmm_nt(wu[sel, :, 0:PL], sb) + wu[sel, :, PL:2 * PL]
        us.append(u)
        s0s.append(sb)
        p_tot = jnp.stack([jnp.exp(tot_c[c][:, sl]) for sl in lanes], axis=0)
        s = s * p_tot + _bmm_tn(jnp.concatenate([u.astype(BF16), v_s[sel]], axis=1),
                                jnp.concatenate([bh_s[sel], kh_s[sel]], axis=1))
    state_ref[...] = s

    u_all = jnp.concatenate(us, axis=0)
    s0_all = jnp.concatenate(s0s, axis=0)
    y2 = _bmm_nt(rt_s, s0_all) + _bmm(a_rb, u_all) + _bmm(a_rk, v_s)
    y2 = y2[:, 0:C, :] + y2[:, C:2 * C, :]
    y = jnp.concatenate([jnp.concatenate([y2[c * PW + pi] for pi in range(PW)], axis=1) for c in range(NC)], axis=0)

    inv_n = 1.0 / RW_HEAD_DIM
    mean = seg_sum(y) * inv_n
    yc = y - mean
    var = seg_sum(yc * yc) * inv_n
    yn = yc * lax.rsqrt(var + RW_GN_EPS) * lnx_w + lnx_b
    o_ref[...] = ((yn + bonus) * g).astype(o_ref.dtype)


def _rwkv7(zrw, mu, w0, w_up, a0, a_up, g_up, k_k, k_a, r_k, lnx_w, lnx_b, *, batch, seq, n_heads, L, PW):
    T = zrw.shape[0]
    HC = n_heads * RW_HEAD_DIM
    PWL = PW * RW_PAIR
    assert HC % PWL == 0 and seq % L == 0 and L % RW_CHUNK == 0
    ng = HC // PWL
    nt = seq // L
    lora_w = RW_DECAY_RANK + RW_ICLR_RANK
    assert lora_w == V7X_LANES and RW_GATE_RANK == V7X_LANES
    vecs = jnp.stack([w0, a0, k_k, k_a, r_k.reshape(HC), lnx_w, lnx_b, jnp.zeros((HC,), F32)], axis=0)
    ww = jnp.concatenate([w_up, jnp.zeros((RW_ICLR_RANK, HC), F32)], axis=0).astype(BF16)
    wa = jnp.concatenate([jnp.zeros((RW_DECAY_RANK, HC), F32), a_up], axis=0).astype(BF16)
    gw = g_up.astype(BF16)
    mu2 = mu.reshape(1, -1)
    cb = HC // PWL
    wa_blk = 3 * HC // V7X_LANES
    zrow = lambda b, g, t: b * nt + t
    big = lambda sec: pl.BlockSpec((L, PWL), lambda b, g, t: (zrow(b, g, t), sec * cb + g))
    small = lambda off: pl.BlockSpec((L, V7X_LANES), lambda b, g, t: (zrow(b, g, t), wa_blk + off))
    mu_big = lambda sec: pl.BlockSpec((1, PWL), lambda b, g, t: (0, sec * cb + g))
    mu_small = lambda off: pl.BlockSpec((1, V7X_LANES), lambda b, g, t: (0, wa_blk + off))
    wspec = pl.BlockSpec((V7X_LANES, PWL), lambda b, g, t: (0, g))
    return pl.pallas_call(
        functools.partial(_rwkv_kernel, L=L, PW=PW),
        grid=(batch, ng, nt),
        in_specs=[big(0), big(1), big(2), small(0), small(1),
                  mu_big(0), mu_big(1), mu_big(2), mu_small(0), mu_small(1),
                  pl.BlockSpec((8, PWL), lambda b, g, t: (0, g)),
                  wspec, wspec, wspec],
        out_specs=pl.BlockSpec((L, PWL), lambda b, g, t: (zrow(b, g, t), g)),
        out_shape=jax.ShapeDtypeStruct((T, HC), BF16),
        scratch_shapes=[pltpu.VMEM((8, PWL), F32), pltpu.VMEM((8, PWL), F32), pltpu.VMEM((8, PWL), F32),
                        pltpu.VMEM((8, V7X_LANES), F32), pltpu.VMEM((8, V7X_LANES), F32),
                        pltpu.VMEM((PW, RW_PAIR, RW_PAIR), F32)],
        compiler_params=_cparams(("arbitrary", "arbitrary", "arbitrary")),
        name="rwkv7_chunked",
    )(zrw, zrw, zrw, zrw, zrw, mu2, mu2, mu2, mu2, mu2, vecs, ww, wa, gw)


def _route(lg):
    lane_i = lax.broadcasted_iota(jnp.int32, lg.shape, 1)
    lane = lane_i.astype(F32)
    is_g = lane_i < N_GROUPS
    mg = jnp.max(jnp.where(is_g, lg, NEG_BIG), axis=-1, keepdims=True)
    eg = jnp.exp(jnp.where(is_g, lg - mg, NEG_BIG))
    pg = eg / jnp.sum(eg, axis=-1, keepdims=True)
    p_g = jnp.max(pg, axis=-1, keepdims=True)
    g_sel = jnp.min(jnp.where(is_g & (pg == p_g), lane, 1e9), axis=-1, keepdims=True)
    grp = (jnp.right_shift(lane_i, 3) - 1).astype(F32)
    is_e = (lane_i >= N_GROUPS) & (lane_i < N_GROUPS + N_EXPERTS) & (grp == g_sel)
    me = jnp.max(jnp.where(is_e, lg, NEG_BIG), axis=-1, keepdims=True)
    ee = jnp.exp(jnp.where(is_e, lg - me, NEG_BIG))
    pe = ee / jnp.sum(ee, axis=-1, keepdims=True)
    p1 = jnp.max(jnp.where(is_e, pe, -1.0), axis=-1, keepdims=True)
    i1 = jnp.min(jnp.where(is_e & (pe == p1), lane, 1e9), axis=-1, keepdims=True)
    rest = is_e & (lane != i1)
    p2 = jnp.max(jnp.where(rest, pe, -1.0), axis=-1, keepdims=True)
    i2 = jnp.min(jnp.where(rest & (pe == p2), lane, 1e9), axis=-1, keepdims=True)
    den = p1 + p2
    eid = jnp.where(lane_i == 0, i1 - N_GROUPS, jnp.where(lane_i == 1, i2 - N_GROUPS, 0.0)).astype(jnp.int32)
    gate = jnp.where(lane_i == 0, p_g * p1 / den, jnp.where(lane_i == 1, p_g * p2 / den, 0.0))
    return eid, gate


def _store_lines(ref, x, rows, first=0, stride=None):
    nl = x.shape[1] // V7X_LANES
    for s in range(nl):
        ref[pl.ds(first + s, rows, stride=stride or nl), :] = x[:, s * V7X_LANES:(s + 1) * V7X_LANES]


def _load_lines(ref, rows, nl, first=0, stride=None):
    return jnp.concatenate([ref[pl.ds(first + s, rows, stride=stride or nl), :] for s in range(nl)], axis=1)


def _outproj_kernel(oda_ref, orw_ref, wt_ref, wb_ref, x_ref, n2_ref, rcat_ref, rb_ref,
                    h_ref, xn_ref, eid_ref, gate_ref, *, tm, eps):
    acc = (jnp.dot(oda_ref[...], wt_ref[...], preferred_element_type=F32)
           + jnp.dot(orw_ref[...], wb_ref[...], preferred_element_type=F32))
    h = x_ref[...] + acc
    h_ref[...] = h
    ms = jnp.mean(h * h, axis=-1, keepdims=True)
    xn = h * lax.rsqrt(ms + eps) * n2_ref[...]
    _store_lines(xn_ref, xn, tm)
    hi = xn.astype(BF16)
    lo = (xn - hi.astype(F32)).astype(BF16)
    hw = jnp.dot(hi, rcat_ref[...], preferred_element_type=F32)
    lg = (hw[:, 0:V7X_LANES] + hw[:, V7X_LANES:2 * V7X_LANES]
          + jnp.dot(lo, rcat_ref[:, 0:V7X_LANES], preferred_element_type=F32)) + rb_ref[...]
    eid, gate = _route(lg)
    eid_ref[...] = eid
    gate_ref[...] = gate


def _outproj_route(oda, orw, w_out, x2, n2, wg, bg, we, be, *, tm):
    T, D = x2.shape
    w1 = oda.shape[1]
    w2 = orw.shape[1]
    nl = D // V7X_LANES
    wt = w_out[:w1].astype(BF16)
    wb = w_out[w1:].astype(BF16)
    pad = V7X_LANES - N_GROUPS - N_EXPERTS
    wr = jnp.concatenate([wg, we, jnp.zeros((D, pad), F32)], axis=1)
    rb = jnp.concatenate([bg, be, jnp.zeros((pad,), F32)]).reshape(1, V7X_LANES)
    rhi = wr.astype(BF16)
    rlo = (wr - rhi.astype(F32)).astype(BF16)
    rcat = jnp.concatenate([rhi, rlo], axis=1)
    const = lambda shape: pl.BlockSpec(shape, lambda i: (0, 0))
    return pl.pallas_call(
        functools.partial(_outproj_kernel, tm=tm, eps=NORM_EPS),
        grid=(T // tm,),
        in_specs=[pl.BlockSpec((tm, w1), lambda i: (i, 0)), pl.BlockSpec((tm, w2), lambda i: (i, 0)),
                  const((w1, D)), const((w2, D)), pl.BlockSpec((tm, D), lambda i: (i, 0)), const((1, D)),
                  const((D, 2 * V7X_LANES)), const((1, V7X_LANES))],
        out_specs=[pl.BlockSpec((tm, D), lambda i: (i, 0)),
                   pl.BlockSpec((tm * nl, V7X_LANES), lambda i: (i, 0)),
                   pl.BlockSpec((tm, V7X_LANES), lambda i: (i, 0)),
                   pl.BlockSpec((tm, V7X_LANES), lambda i: (i, 0))],
        out_shape=[jax.ShapeDtypeStruct((T, D), F32),
                   jax.ShapeDtypeStruct((T * nl, V7X_LANES), F32),
                   jax.ShapeDtypeStruct((T, V7X_LANES), jnp.int32),
                   jax.ShapeDtypeStruct((T, V7X_LANES), F32)],
        compiler_params=_cparams(("arbitrary",)),
        name="outproj_route",
    )(oda, orw, wt, wb, x2, n2.reshape(1, D), rcat, rb)


MOE_DMA_UNROLL = 8


def _moe_kernel(blk_e_ref, nxt_e_ref, par_ref, nused_ref, tok_ref, tokn_ref, dst_ref,
                xn_hbm, w1_hbm, w3_hbm, w2_hbm, ys_hbm,
                xbuf, ybuf, wf1, wf3, wf2, w1b, w3b, w2b, sem_in, sem_out, sem_w, *, bm, nl, n_slots):
    i = pl.program_id(0)
    n_used = nused_ref[0]
    slot = i % 2

    def gather(ids_ref, dst_slot):
        def body(r, c):
            src = xn_hbm.at[pl.ds(pl.multiple_of(ids_ref[0, r] * nl, nl), nl)]
            pltpu.make_async_copy(src, xbuf.at[dst_slot, pl.ds(pl.multiple_of(r * nl, nl), nl)],
                                  sem_in.at[dst_slot]).start()
            return c
        lax.fori_loop(0, bm, body, 0, unroll=MOE_DMA_UNROLL)

    def wait_gather(src_slot):
        pltpu.make_async_copy(xn_hbm.at[pl.ds(0, bm * nl)], xbuf.at[src_slot], sem_in.at[src_slot]).wait()

    def scatter():
        def body(r, c):
            dst = ys_hbm.at[pl.ds(pl.multiple_of(dst_ref[0, r] * nl, nl), nl)]
            pltpu.make_async_copy(ybuf.at[pl.ds(pl.multiple_of(r * nl, nl), nl)], dst, sem_out).start()
            return c
        lax.fori_loop(0, bm, body, 0, unroll=MOE_DMA_UNROLL)

    def wait_scatter():
        pltpu.make_async_copy(ybuf, ys_hbm.at[pl.ds(0, bm * nl)], sem_out).wait()

    def weight_copies(e, wslot):
        return (pltpu.make_async_copy(w1_hbm.at[e], wf1.at[wslot], sem_w.at[wslot]),
                pltpu.make_async_copy(w3_hbm.at[e], wf3.at[wslot], sem_w.at[wslot]),
                pltpu.make_async_copy(w2_hbm.at[e], wf2.at[wslot], sem_w.at[wslot]))

    @pl.when(i == 0)
    def _prologue():
        for cp in weight_copies(blk_e_ref[0], par_ref[0]):
            cp.start()
        gather(tok_ref, 0)
        ybuf[...] = jnp.zeros(ybuf.shape, ybuf.dtype)
        spare = pltpu.make_async_copy(ybuf, ys_hbm.at[pl.ds(n_slots * nl, bm * nl)], sem_out)
        spare.start()
        spare.wait()

    @pl.when(i + 1 < n_used)
    def _prefetch():
        gather(tokn_ref, 1 - slot)

    @pl.when(i < n_used)
    def _active():
        e = blk_e_ref[i]
        first = jnp.logical_or(i == 0, e != blk_e_ref[jnp.maximum(i - 1, 0)])

        @pl.when(first)
        def _new_expert():
            wslot = par_ref[i]
            for cp in weight_copies(e, wslot):
                cp.wait()
            nxt = nxt_e_ref[i]

            @pl.when(nxt >= 0)
            def _():
                for cp in weight_copies(nxt, 1 - wslot):
                    cp.start()

            w1b[...] = wf1[wslot].astype(BF16)
            w3b[...] = wf3[wslot].astype(BF16)
            w2b[...] = wf2[wslot].astype(BF16)

        wait_gather(slot)
        x = _load_lines(xbuf.at[slot], bm, nl).astype(BF16)
        h1 = jnp.dot(x, w1b[...], preferred_element_type=F32)
        h3 = jnp.dot(x, w3b[...], preferred_element_type=F32)
        hh = (h1 * _sigmoid(h1) * h3).astype(BF16)
        y = jnp.dot(hh, w2b[...], preferred_element_type=F32)

        @pl.when(i > 0)
        def _drain_prev():
            wait_scatter()

        _store_lines(ybuf, y, bm)
        scatter()

        @pl.when(i == n_used - 1)
        def _drain_last():
            wait_scatter()


def _moe_ffn(xn_lines, plan, w1, w3, w2, *, bm, n_slots):
    blk_e, nxt_e, par, n_used, tok_p, dst_p = plan
    E, D, DE = w1.shape
    nl = D // V7X_LANES
    nb = blk_e.shape[0]
    ids = lambda imap: pl.BlockSpec((None, 1, bm), imap, memory_space=pltpu.SMEM)
    grid_spec = pltpu.PrefetchScalarGridSpec(
        num_scalar_prefetch=4,
        grid=(nb,),
        in_specs=[
            ids(lambda i, *_: (i, 0, 0)),
            ids(lambda i, *_: (jnp.minimum(i + 1, nb - 1), 0, 0)),
            ids(lambda i, *_: (i, 0, 0)),
            pl.BlockSpec(memory_space=pl.ANY),
            pl.BlockSpec(memory_space=pl.ANY),
            pl.BlockSpec(memory_space=pl.ANY),
            pl.BlockSpec(memory_space=pl.ANY),
        ],
        out_specs=pl.BlockSpec(memory_space=pl.ANY),
        scratch_shapes=[pltpu.VMEM((2, bm * nl, V7X_LANES), F32), pltpu.VMEM((bm * nl, V7X_LANES), F32),
                        pltpu.VMEM((2, D, DE), F32), pltpu.VMEM((2, D, DE), F32), pltpu.VMEM((2, DE, D), F32),
                        pltpu.VMEM((D, DE), BF16), pltpu.VMEM((D, DE), BF16), pltpu.VMEM((DE, D), BF16),
                        pltpu.SemaphoreType.DMA((2,)), pltpu.SemaphoreType.DMA, pltpu.SemaphoreType.DMA((2,))],
    )
    tok3 = tok_p.reshape(nb, 1, bm)
    return pl.pallas_call(
        functools.partial(_moe_kernel, bm=bm, nl=nl, n_slots=n_slots),
        grid_spec=grid_spec,
        out_shape=jax.ShapeDtypeStruct(((n_slots + bm) * nl, V7X_LANES), F32),
        compiler_params=_cparams(("arbitrary",)),
        name="moe_ffn",
    )(blk_e, nxt_e, par, n_used, tok3, tok3, dst_p.reshape(nb, 1, bm), xn_lines, w1, w3, w2)


def _moe_plan(eid, *, bm):
    T = eid.shape[0]
    M = T * TOP_K
    E = N_EXPERTS
    i32 = jnp.int32
    e_flat = eid.reshape(M)
    order = jnp.argsort(e_flat).astype(i32)
    experts = jnp.arange(E, dtype=i32)
    counts = jnp.sum((e_flat[:, None] == experts[None, :]).astype(i32), axis=0)
    start = jnp.cumsum(counts) - counts
    padded = (counts + bm - 1) // bm * bm
    pend = jnp.cumsum(padded)
    pstart = pend - padded
    nb = M // bm + E
    blk = jnp.arange(nb, dtype=i32)
    blk_start = blk * bm
    n_used = (pend[-1] // bm).astype(i32)
    blk_e = jnp.minimum(jnp.sum((blk_start[:, None] >= pend[None, :]).astype(i32), axis=1), E - 1)
    blk_e = jnp.where(blk < n_used, blk_e, blk_e[jnp.maximum(n_used - 1, 0)])
    off = blk_start - pstart[blk_e]
    base = start[blk_e] + off
    valid = jnp.where(blk < n_used, jnp.clip(counts[blk_e] - off, 0, bm), 0)
    r = jnp.arange(bm, dtype=i32)
    slot_p = order[jnp.clip(base[:, None] + r[None, :], 0, M - 1)]
    tok_p = slot_p // TOP_K
    dst_p = jnp.where(r[None, :] < valid[:, None], slot_p, M + r[None, :])
    seg_first = jnp.concatenate([jnp.ones((1,), bool), blk_e[1:] != blk_e[:-1]])
    par = (jnp.cumsum(seg_first.astype(i32)) - 1) % 2
    later = (experts[None, :] > experts[:, None]) & (counts[None, :] > 0)
    nxt_of = jnp.min(jnp.where(later, experts[None, :], E), axis=1)
    nxt_e = jnp.where(nxt_of == E, -1, nxt_of)[blk_e]
    return blk_e, nxt_e.astype(i32), par.astype(i32), n_used.reshape(1), tok_p, dst_p


def _combine_kernel(h_ref, ys_ref, gate_ref, fw_ref, o_ref, *, tm, nl, eps):
    gate = gate_ref[...]
    acc = h_ref[...]
    for j in range(TOP_K):
        yj = _load_lines(ys_ref, tm, nl, first=j * nl, stride=TOP_K * nl)
        acc = acc + yj * gate[:, j:j + 1]
    ms = jnp.mean(acc * acc, axis=-1, keepdims=True)
    o_ref[...] = acc * lax.rsqrt(ms + eps) * fw_ref[...]


def _combine(h, ys, gate, fw, *, tm):
    T, D = h.shape
    nl = D // V7X_LANES
    return pl.pallas_call(
        functools.partial(_combine_kernel, tm=tm, nl=nl, eps=NORM_EPS),
        grid=(T // tm,),
        in_specs=[pl.BlockSpec((tm, D), lambda i: (i, 0)),
                  pl.BlockSpec((tm * TOP_K * nl, V7X_LANES), lambda i: (i, 0)),
                  pl.BlockSpec((tm, V7X_LANES), lambda i: (i, 0)),
                  pl.BlockSpec((1, D), lambda i: (0, 0))],
        out_specs=pl.BlockSpec((tm, D), lambda i: (i, 0)),
        out_shape=jax.ShapeDtypeStruct((T, D), F32),
        compiler_params=_cparams(("arbitrary",)),
        name="combine_norm",
    )(h, ys, gate, fw.reshape(1, D))


def _tiles(T, seq, D):
    pick = lambda n, prefs: next(p for p in prefs if n % p == 0)
    return dict(
        tm_in=pick(seq, (512, 256, 128)),
        tm_rw=pick(seq, (1024, 512, 256, 128)),
        tn_da=1024,
        tq=pick(seq, (1024, 512, 256, 128)),
        tk=pick(seq, (256, 128)),
        rw_L=pick(seq, (256, 128, 64)),
        rw_PW=8,
        tm_out=pick(T, (256, 128)),
        bm=256,
        tm_cmb=pick(T, (256, 128)),
    )


def kernel(x, norm1_w, w_in, lam_q1, lam_k1, lam_q2, lam_k2, subln_w, rw_mu, rw_w0, rw_w_up, rw_a0, rw_a_up, rw_g_up, rw_k_k, rw_k_a, rw_r_k, rw_lnx_w, rw_lnx_b, w_out, norm2_w, router_group_w, router_group_b, router_expert_w, router_expert_b, moe_w1, moe_w3, moe_w2, final_norm_w):
    B, S, D = x.shape
    T = B * S
    depth = w_in.shape[0]
    rw_heads = rw_w0.shape[1] // RW_HEAD_DIM
    rw_cols = rw_mu.shape[1]
    da_cols = w_in.shape[2] - rw_cols
    da_width = da_cols // 3
    da_heads = da_width // (2 * DA_HEAD_DIM)
    tl = _tiles(T, S, D)
    cos, sin = _rope_tables(S)
    q_scale = DA_HEAD_DIM ** -0.5 * math.log2(math.e)
    colscale = jnp.concatenate([jnp.full((da_width,), q_scale, F32), jnp.ones((da_width,), F32)])

    h = x.reshape(T, D)
    for l in range(depth):
        assert l == 0, "lam_init is specialised to the first layer"
        w_qk = (w_in[l][:, :2 * da_width] * colscale)[:, _rope_column_order(2 * da_width)].astype(BF16)
        w_v = w_in[l][:, 2 * da_width:da_cols].T.astype(BF16)
        w_rw = w_in[l][:, da_cols:].astype(BF16)
        zqk = _inproj(h, norm1_w[l], w_qk, seq=S, tm=tl["tm_in"], tn=tl["tn_da"], out_dtype=BF16,
                      mode="rope", cos=cos, sin=sin)
        vt = _inproj(h, norm1_w[l], w_v, seq=S, tm=tl["tm_in"], tn=tl["tn_da"], out_dtype=BF16,
                     mode="transposed", tkv=tl["tk"])
        zrw = _inproj(h, norm1_w[l], w_rw, seq=S, tm=tl["tm_rw"], tn=rw_cols // 2, out_dtype=F32)
        lam = (jnp.exp(jnp.sum(lam_q1[l] * lam_k1[l])) - jnp.exp(jnp.sum(lam_q2[l] * lam_k2[l])) + LAM_INIT)
        o_da = _diff_attention(zqk, vt, lam.astype(F32), subln_w[l], batch=B, seq=S, n_heads=da_heads,
                               tq=tl["tq"], tk=tl["tk"])
        o_rw = _rwkv7(zrw, rw_mu[l], rw_w0[l], rw_w_up[l], rw_a0[l], rw_a_up[l], rw_g_up[l], rw_k_k[l], rw_k_a[l],
                      rw_r_k[l], rw_lnx_w[l], rw_lnx_b[l], batch=B, seq=S, n_heads=rw_heads,
                      L=tl["rw_L"], PW=tl["rw_PW"])
        h, xn_lines, eid, gate = _outproj_route(o_da, o_rw, w_out[l], h, norm2_w[l], router_group_w[l],
                                                router_group_b[l], router_expert_w[l], router_expert_b[l],
                                                tm=tl["tm_out"])
        plan = _moe_plan(eid[:, :TOP_K], bm=tl["bm"])
        ys = _moe_ffn(xn_lines, plan, moe_w1[l], moe_w3[l], moe_w2[l], bm=tl["bm"], n_slots=T * TOP_K)
        assert depth == 1, "the final norm is fused into the last layer's combine"
        out = _combine(h, ys, gate, final_norm_w, tm=tl["tm_cmb"])
    return out.reshape(B, S, D)
```

```python
import functools
import math

import jax
import jax.numpy as jnp
from jax import lax
from jax.experimental import pallas as pl
from jax.experimental.pallas import tpu as pltpu

F32 = jnp.float32
BF16 = jnp.bfloat16

DA_HEAD_DIM = 64
RW_HEAD_DIM = 64
RW_DECAY_RANK = 64
RW_ICLR_RANK = 64
RW_GATE_RANK = 128
ROPE_THETA = 10000.0
N_GROUPS = 8
EXPERTS_PER_GROUP = 8
N_EXPERTS = N_GROUPS * EXPERTS_PER_GROUP
TOP_K = 2
NORM_EPS = 1e-6
SUBLN_EPS = 1e-5
RW_GN_EPS = 64e-5
LAM_INIT = 0.8 - 0.6 * math.exp(-0.3 * 0)

V7X_LANES = 128
V7X_VMEM_LIMIT = 56 * 1024 * 1024
NEG_BIG = -1e30
ATTN_LOOKAHEAD = 5


def _cparams(sem):
    return pltpu.CompilerParams(dimension_semantics=sem, vmem_limit_bytes=V7X_VMEM_LIMIT)


def _inproj_kernel(*refs, mode, tn, tkv, eps):
    if mode == "rope":
        x_ref, nw_ref, w_ref, cos_ref, sin_ref, o_ref, xn_ref = refs
    else:
        x_ref, nw_ref, w_ref, o_ref, xn_ref = refs
    j = pl.program_id(1)

    @pl.when(j == 0)
    def _norm():
        x = x_ref[...]
        ms = jnp.mean(x * x, axis=-1, keepdims=True)
        xn_ref[...] = (x * lax.rsqrt(ms + eps) * nw_ref[...]).astype(BF16)

    if mode == "transposed":
        acc_t = lax.dot_general(w_ref[...], xn_ref[...], (((1,), (1,)), ((), ())), preferred_element_type=F32)
        for c in range(acc_t.shape[1] // tkv):
            o_ref[c] = acc_t[:, c * tkv:(c + 1) * tkv].astype(o_ref.dtype)
        return

    acc = jnp.dot(xn_ref[...], w_ref[...], preferred_element_type=F32)
    if mode == "rope":
        cos = cos_ref[...]
        sin = sin_ref[...]
        for c in range(tn // V7X_LANES):
            cols = slice(c * V7X_LANES, (c + 1) * V7X_LANES)
            blk = acc[:, cols]
            o_ref[:, cols] = (blk * cos + pltpu.roll(blk, V7X_LANES // 2, 1) * sin).astype(o_ref.dtype)
    else:
        o_ref[...] = acc.astype(o_ref.dtype)


def _inproj(x2, nw, w, *, seq, tm, tn, out_dtype, mode="plain", cos=None, sin=None, tkv=None):
    T, D = x2.shape
    N = w.shape[0] if mode == "transposed" else w.shape[1]
    assert T % tm == 0 and N % tn == 0 and seq % tm == 0
    w_spec = (pl.BlockSpec((tn, D), lambda i, j: (j, 0)) if mode == "transposed"
              else pl.BlockSpec((D, tn), lambda i, j: (0, j)))
    in_specs = [
        pl.BlockSpec((tm, D), lambda i, j: (i, 0)),
        pl.BlockSpec((1, D), lambda i, j: (0, 0)),
        w_spec,
    ]
    args = [x2, nw.reshape(1, D), w]
    if mode == "rope":
        ns = seq // tm
        in_specs += [pl.BlockSpec((tm, V7X_LANES), lambda i, j: (i % ns, 0)),
                     pl.BlockSpec((tm, V7X_LANES), lambda i, j: (i % ns, 0))]
        args += [cos, sin]
    if mode == "transposed":
        assert tm % tkv == 0
        out_specs = pl.BlockSpec((tm // tkv, tn, tkv), lambda i, j: (i, j, 0))
        out_shape = jax.ShapeDtypeStruct((T // tkv, N, tkv), out_dtype)
    else:
        out_specs = pl.BlockSpec((tm, tn), lambda i, j: (i, j))
        out_shape = jax.ShapeDtypeStruct((T, N), out_dtype)
    return pl.pallas_call(
        functools.partial(_inproj_kernel, mode=mode, tn=tn, tkv=tkv, eps=NORM_EPS),
        grid=(T // tm, N // tn),
        in_specs=in_specs,
        out_specs=out_specs,
        out_shape=out_shape,
        scratch_shapes=[pltpu.VMEM((tm, D), BF16)],
        compiler_params=_cparams(("arbitrary", "arbitrary")),
        name="inproj_" + mode,
    )(*args)


def _rope_column_order(width):
    half = DA_HEAD_DIM // 2
    blk = jnp.concatenate([jnp.arange(0, half), jnp.arange(2 * half, 3 * half),
                           jnp.arange(half, 2 * half), jnp.arange(3 * half, 4 * half)])
    cols = jnp.arange(width)
    return (cols // V7X_LANES) * V7X_LANES + blk[cols % V7X_LANES]


def _rope_tables(seq):
    half = DA_HEAD_DIM // 2
    inv = ROPE_THETA ** (-jnp.arange(half, dtype=F32) / half)
    ang = jnp.arange(seq, dtype=F32)[:, None] * inv[None, :]
    cos = jnp.cos(ang)
    sin = jnp.sin(ang)
    return jnp.tile(cos, (1, 4)), jnp.concatenate([-sin, -sin, sin, sin], axis=-1)


def _attn_kernel(lam_ref, q_ref, k_ref, vt_ref, sw_ref, o_ref, qs_ref, m_ref, l_ref, acc_ref, *, tq, tk):
    qi = pl.program_id(2)
    d = DA_HEAD_DIM
    q = q_ref[...]
    lane = lax.broadcasted_iota(jnp.int32, q.shape, 1)
    comp1 = (lane % d) < (d // 2)
    zero = jnp.zeros_like(q)
    qs_ref[0:tq, :] = jnp.where(comp1, q, zero)
    qs_ref[tq:2 * tq, :] = jnp.where(comp1, zero, q)
    m_ref[...] = jnp.full(m_ref.shape, NEG_BIG, F32)
    l_ref[...] = jnp.zeros(l_ref.shape, F32)
    acc_ref[...] = jnp.zeros(acc_ref.shape, F32)

    n_diag = tq // tk
    n_full = qi * n_diag

    n_strips = 2 * tq // tk

    def strip_scores(k, si, on_diagonal):
        s = lax.dot_general(k, qs_ref[si * tk:(si + 1) * tk, :], (((1,), (1,)), ((), ())),
                            preferred_element_type=F32)
        if on_diagonal:
            kpos = lax.broadcasted_iota(jnp.int32, s.shape, 0)
            qpos = lax.broadcasted_iota(jnp.int32, s.shape, 1)
            s = jnp.where(kpos <= qpos, s, NEG_BIG)
        return s

    def strip_update(vt, si, s):
        lanes = slice(si * tk, (si + 1) * tk)
        m_old = m_ref[:, lanes]
        m_new = jnp.maximum(m_old, jnp.max(s, axis=0, keepdims=True))
        alpha = jnp.exp2(m_old - m_new)
        p = jnp.exp2(s - m_new)
        l_ref[:, lanes] = alpha * l_ref[:, lanes] + jnp.sum(p, axis=0, keepdims=True)
        acc_ref[:, lanes] = alpha * acc_ref[:, lanes] + jnp.dot(vt, p.astype(BF16), preferred_element_type=F32)
        m_ref[:, lanes] = m_new

    def kv_block(j, work):
        k = k_ref[pl.ds(pl.multiple_of(j * tk, tk), tk), :]
        vt = vt_ref[j]
        pending = [strip_scores(k, si, dg) for si, dg in work[:ATTN_LOOKAHEAD]]
        for n, (si, _) in enumerate(work):
            if n + ATTN_LOOKAHEAD < len(work):
                nsi, ndg = work[n + ATTN_LOOKAHEAD]
                pending.append(strip_scores(k, nsi, ndg))
            strip_update(vt, si, pending[n])

    def full_step(j, c):
        kv_block(j, [(si, False) for si in range(n_strips)])
        return c

    lax.fori_loop(0, n_full, full_step, 0)

    for c in range(n_diag):
        kv_block(n_full + c, [(si, si % n_diag == c) for si in range(n_strips) if si % n_diag >= c])

    o = acc_ref[...] / l_ref[...]
    od = o[:, 0:tq] - lam_ref[0] * o[:, tq:2 * tq]
    ms = jnp.mean(od * od, axis=0, keepdims=True)
    on = (od * lax.rsqrt(ms + SUBLN_EPS) * sw_ref[...]) * (1.0 - LAM_INIT)
    o_ref[...] = on.T.astype(o_ref.dtype)


def _diff_attention(zqk, vt, lam, subln_w, *, batch, seq, n_heads, tq, tk):
    T = zqk.shape[0]
    hw = 2 * DA_HEAD_DIM
    nq = seq // tq
    nk = seq // tk
    sw_b = jnp.broadcast_to(subln_w.reshape(hw, 1), (hw, tq)).astype(F32)
    return pl.pallas_call(
        functools.partial(_attn_kernel, tq=tq, tk=tk),
        grid=(batch, n_heads, nq),
        in_specs=[
            pl.BlockSpec(memory_space=pltpu.SMEM),
            pl.BlockSpec((tq, hw), lambda b, h, i: (b * nq + i, h)),
            pl.BlockSpec((seq, hw), lambda b, h, i: (b, n_heads + h)),
            pl.BlockSpec((nk, hw, tk), lambda b, h, i: (b, h, 0)),
            pl.BlockSpec((hw, tq), lambda b, h, i: (0, 0)),
        ],
        out_specs=pl.BlockSpec((tq, hw), lambda b, h, i: (b * nq + i, h)),
        out_shape=jax.ShapeDtypeStruct((T, n_heads * hw), BF16),
        scratch_shapes=[
            pltpu.VMEM((2 * tq, hw), BF16),
            pltpu.VMEM((1, 2 * tq), F32),
            pltpu.VMEM((1, 2 * tq), F32),
            pltpu.VMEM((hw, 2 * tq), F32),
        ],
        compiler_params=_cparams(("arbitrary", "arbitrary", "arbitrary")),
        name="diff_attn",
    )(lam.reshape(1), zqk, zqk, vt, sw_b)


RW_CHUNK = 64
RW_PAIR = 2 * RW_HEAD_DIM
RW_INV_BLOCK = 16


def _bmm(a, b):
    return jnp.einsum("bij,bjk->bik", a.astype(BF16), b.astype(BF16), preferred_element_type=F32)


def _bmm_nt(a, b):
    return jnp.einsum("bik,bjk->bij", a.astype(BF16), b.astype(BF16), preferred_element_type=F32)


def _bmm_tn(a, b):
    return jnp.einsum("bti,btj->bij", a.astype(BF16), b.astype(BF16), preferred_element_type=F32)


def _mm_split(x, e):
    hi = x.astype(BF16)
    lo = (x - hi.astype(F32)).astype(BF16)
    return (jnp.dot(hi, e, preferred_element_type=F32) + jnp.dot(lo, e, preferred_element_type=F32))


def _sigmoid(x):
    return 1.0 / (1.0 + jnp.exp(-x))


def _softplus(x):
    return jnp.maximum(x, 0.0) + jnp.log(1.0 + jnp.exp(-jnp.abs(x)))


def _unit_lower_inverse(a, eye, diag_blk):
    ad = jnp.where(diag_blk, a, 0.0)
    ao = a - ad
    a2 = _bmm(ad, ad)
    a4 = _bmm(a2, a2)
    a8 = _bmm(a4, a4)
    td = eye + ad
    td = td + _bmm(td, a2)
    td = td + _bmm(td, a4)
    td = td + _bmm(td, a8)
    n1 = _bmm(td, ao)
    n2 = _bmm(n1, n1)
    x = td + _bmm(n2, td)
    return x + _bmm(n1, x)


def _rwkv_kernel(zr_ref, zk_ref, zv_ref, zwa_ref, zg_ref, mur_ref, muk_ref, muv_ref, muwa_ref, mug_ref,
                 vec_ref, ww_ref, wa_ref, gup_ref, o_ref,
                 cr_ref, ck_ref, cv_ref, cwa_ref, cg_ref, state_ref, *, L, PW):
    t = pl.program_id(2)
    C = RW_CHUNK
    PL = RW_PAIR
    NC = L // C

    @pl.when(t == 0)
    def _reset():
        for c_ref in (cr_ref, ck_ref, cv_ref, cwa_ref, cg_ref):
            c_ref[...] = jnp.zeros(c_ref.shape, F32)
        state_ref[...] = jnp.zeros(state_ref.shape, F32)

    def shift_mix(z_ref, mu_ref, c_ref):
        z = z_ref[...].astype(F32)
        row = lax.broadcasted_iota(jnp.int32, z.shape, 0)
        zprev = jnp.where(row == 0, c_ref[0:1, :], pltpu.roll(z, 1, 0))
        c_ref[0:1, :] = z[L - 1:L, :]
        return z + (zprev - z) * mu_ref[...]

    r = shift_mix(zr_ref, mur_ref, cr_ref)
    k = shift_mix(zk_ref, muk_ref, ck_ref)
    v = shift_mix(zv_ref, muv_ref, cv_ref)
    zwa = shift_mix(zwa_ref, muwa_ref, cwa_ref)
    zg = shift_mix(zg_ref, mug_ref, cg_ref)

    ri = lax.broadcasted_iota(jnp.int32, (PL, PL), 0)
    ci = lax.broadcasted_iota(jnp.int32, (PL, PL), 1)
    eye = (ri == ci).astype(F32)
    strict = ci < ri
    incl = ci <= ri
    diag_blk = (ri // RW_INV_BLOCK) == (ci // RW_INV_BLOCK)
    seg_ones = ((ri // RW_HEAD_DIM) == (ci // RW_HEAD_DIM)).astype(BF16)
    rc = lax.broadcasted_iota(jnp.int32, (C, C), 0)
    cc = lax.broadcasted_iota(jnp.int32, (C, C), 1)
    tri_incl = (cc <= rc).astype(BF16)
    head0 = lax.broadcasted_iota(jnp.int32, (C, PL), 1) < RW_HEAD_DIM
    lanes = [slice(pi * PL, (pi + 1) * PL) for pi in range(PW)]
    rows = [slice(c * C, (c + 1) * C) for c in range(NC)]

    def seg_sum(x):
        return jnp.concatenate([_mm_split(x[:, sl], seg_ones) for sl in lanes], axis=1)

    w0, a0, k_k, k_a, r_k, lnx_w, lnx_b = (vec_ref[i:i + 1, :] for i in range(7))
    w_pre = w0 + jnp.dot(jnp.tanh(zwa).astype(BF16), ww_ref[...], preferred_element_type=F32)
    logdec = -jnp.exp(-_softplus(-w_pre) - 0.5)
    a = _sigmoid(a0 + jnp.dot(zwa.astype(BF16), wa_ref[...], preferred_element_type=F32))
    g = jnp.dot(_sigmoid(zg).astype(BF16), gup_ref[...], preferred_element_type=F32)
    kk = k * k_k
    kkn = kk / jnp.maximum(jnp.sqrt(seg_sum(kk * kk)), 1e-12)
    kf = k * (1.0 + (a - 1.0) * k_a)
    a_s = -kkn
    b_s = kkn * a
    bonus = seg_sum(r * kf * r_k) * v

    ld_hi = logdec.astype(BF16)
    ld_lo = (logdec - ld_hi.astype(F32)).astype(BF16)
    cum_c = [jnp.dot(tri_incl, ld_hi[rs], preferred_element_type=F32)
             + jnp.dot(tri_incl, ld_lo[rs], preferred_element_type=F32) for rs in rows]
    tot_c = [cu[C - 1:C, :] for cu in cum_c]
    cum = jnp.concatenate(cum_c, axis=0)
    tot = jnp.concatenate([jnp.broadcast_to(tc, (C, tc.shape[1])) for tc in tot_c], axis=0)
    p_inv = jnp.exp(-cum)
    p_end = jnp.exp(tot - cum)

    def stack(x):
        out = []
        for rs in rows:
            for sl in lanes:
                blk = x[rs, sl]
                out.append(jnp.concatenate([jnp.where(head0, blk, 0.0), jnp.where(head0, 0.0, blk)], axis=0))
        return jnp.stack(out, axis=0).astype(BF16)

    rt_s = stack(r * jnp.exp(cum))
    at_s = stack(a_s * jnp.exp(cum - logdec))
    kt_s = stack(kf * p_inv)
    bt_s = stack(b_s * p_inv)
    kh_s = stack(kf * p_end)
    bh_s = stack(b_s * p_end)
    v_s = stack(v)

    m1 = _bmm_nt(jnp.concatenate([at_s, rt_s], axis=1), jnp.concatenate([bt_s, kt_s], axis=1))
    a_ab = jnp.where(strict, m1[:, 0:PL, 0:PL], 0.0)
    a_ak = jnp.where(strict, m1[:, 0:PL, PL:2 * PL], 0.0)
    a_rb = jnp.where(incl, m1[:, PL:2 * PL, 0:PL], 0.0)
    a_rk = jnp.where(incl, m1[:, PL:2 * PL, PL:2 * PL], 0.0)
    tinv = _unit_lower_inverse(a_ab, eye, diag_blk)
    wu = _bmm(tinv, jnp.concatenate([at_s, _bmm(a_ak, v_s).astype(BF16)], axis=2))

    s = state_ref[...]
    us, s0s = [], []
    for c in range(NC):
        sel = slice(c * PW, (c + 1) * PW)
        sb = s.astype(BF16)
        u = _bmm_nt(wu[sel, :, 0:PL], sb) + wu[sel, :, PL:2 * PL]
        us.append(u)
        s0s.append(sb)
        p_tot = jnp.stack([jnp.exp(tot_c[c][:, sl]) for sl in lanes], axis=0)
        s = s * p_tot + _bmm_tn(jnp.concatenate([u.astype(BF16), v_s[sel]], axis=1),
                                jnp.concatenate([bh_s[sel], kh_s[sel]], axis=1))
    state_ref[...] = s

    u_all = jnp.concatenate(us, axis=0)
    s0_all = jnp.concatenate(s0s, axis=0)
    y2 = _bmm_nt(rt_s, s0_all) + _bmm(a_rb, u_all) + _bmm(a_rk, v_s)
    y2 = y2[:, 0:C, :] + y2[:, C:2 * C, :]
    y = jnp.concatenate([jnp.concatenate([y2[c * PW + pi] for pi in range(PW)], axis=1) for c in range(NC)], axis=0)

    inv_n = 1.0 / RW_HEAD_DIM
    mean = seg_sum(y) * inv_n
    yc = y - mean
    var = seg_sum(yc * yc) * inv_n
    yn = yc * lax.rsqrt(var + RW_GN_EPS) * lnx_w + lnx_b
    o_ref[...] = ((yn + bonus) * g).astype(o_ref.dtype)


def _rwkv7(zrw, mu, w0, w_up, a0, a_up, g_up, k_k, k_a, r_k, lnx_w, lnx_b, *, batch, seq, n_heads, L, PW):
    T = zrw.shape[0]
    HC = n_heads * RW_HEAD_DIM
    PWL = PW * RW_PAIR
    assert HC % PWL == 0 and seq % L == 0 and L % RW_CHUNK == 0
    ng = HC // PWL
    nt = seq // L
    lora_w = RW_DECAY_RANK + RW_ICLR_RANK
    assert lora_w == V7X_LANES and RW_GATE_RANK == V7X_LANES
    vecs = jnp.stack([w0, a0, k_k, k_a, r_k.reshape(HC), lnx_w, lnx_b, jnp.zeros((HC,), F32)], axis=0)
    ww = jnp.concatenate([w_up, jnp.zeros((RW_ICLR_RANK, HC), F32)], axis=0).astype(BF16)
    wa = jnp.concatenate([jnp.zeros((RW_DECAY_RANK, HC), F32), a_up], axis=0).astype(BF16)
    gw = g_up.astype(BF16)
    mu2 = mu.reshape(1, -1)
    cb = HC // PWL
    wa_blk = 3 * HC // V7X_LANES
    zrow = lambda b, g, t: b * nt + t
    big = lambda sec: pl.BlockSpec((L, PWL), lambda b, g, t: (zrow(b, g, t), sec * cb + g))
    small = lambda off: pl.BlockSpec((L, V7X_LANES), lambda b, g, t: (zrow(b, g, t), wa_blk + off))
    mu_big = lambda sec: pl.BlockSpec((1, PWL), lambda b, g, t: (0, sec * cb + g))
    mu_small = lambda off: pl.BlockSpec((1, V7X_LANES), lambda b, g, t: (0, wa_blk + off))
    wspec = pl.BlockSpec((V7X_LANES, PWL), lambda b, g, t: (0, g))
    return pl.pallas_call(
        functools.partial(_rwkv_kernel, L=L, PW=PW),
        grid=(batch, ng, nt),
        in_specs=[big(0), big(1), big(2), small(0), small(1),
                  mu_big(0), mu_big(1), mu_big(2), mu_small(0), mu_small(1),
                  pl.BlockSpec((8, PWL), lambda b, g, t: (0, g)),
                  wspec, wspec, wspec],
        out_specs=pl.BlockSpec((L, PWL), lambda b, g, t: (zrow(b, g, t), g)),
        out_shape=jax.ShapeDtypeStruct((T, HC), BF16),
        scratch_shapes=[pltpu.VMEM((8, PWL), F32), pltpu.VMEM((8, PWL), F32), pltpu.VMEM((8, PWL), F32),
                        pltpu.VMEM((8, V7X_LANES), F32), pltpu.VMEM((8, V7X_LANES), F32),
                        pltpu.VMEM((PW, RW_PAIR, RW_PAIR), F32)],
        compiler_params=_cparams(("arbitrary", "arbitrary", "arbitrary")),
        name="rwkv7_chunked",
    )(zrw, zrw, zrw, zrw, zrw, mu2, mu2, mu2, mu2, mu2, vecs, ww, wa, gw)


def _route(lg):
    lane_i = lax.broadcasted_iota(jnp.int32, lg.shape, 1)
    lane = lane_i.astype(F32)
    is_g = lane_i < N_GROUPS
    mg = jnp.max(jnp.where(is_g, lg, NEG_BIG), axis=-1, keepdims=True)
    eg = jnp.exp(jnp.where(is_g, lg - mg, NEG_BIG))
    pg = eg / jnp.sum(eg, axis=-1, keepdims=True)
    p_g = jnp.max(pg, axis=-1, keepdims=True)
    g_sel = jnp.min(jnp.where(is_g & (pg == p_g), lane, 1e9), axis=-1, keepdims=True)
    grp = (jnp.right_shift(lane_i, 3) - 1).astype(F32)
    is_e = (lane_i >= N_GROUPS) & (lane_i < N_GROUPS + N_EXPERTS) & (grp == g_sel)
    me = jnp.max(jnp.where(is_e, lg, NEG_BIG), axis=-1, keepdims=True)
    ee = jnp.exp(jnp.where(is_e, lg - me, NEG_BIG))
    pe = ee / jnp.sum(ee, axis=-1, keepdims=True)
    p1 = jnp.max(jnp.where(is_e, pe, -1.0), axis=-1, keepdims=True)
    i1 = jnp.min(jnp.where(is_e & (pe == p1), lane, 1e9), axis=-1, keepdims=True)
    rest = is_e & (lane != i1)
    p2 = jnp.max(jnp.where(rest, pe, -1.0), axis=-1, keepdims=True)
    i2 = jnp.min(jnp.where(rest & (pe == p2), lane, 1e9), axis=-1, keepdims=True)
    den = p1 + p2
    eid = jnp.where(lane_i == 0, i1 - N_GROUPS, jnp.where(lane_i == 1, i2 - N_GROUPS, 0.0)).astype(jnp.int32)
    gate = jnp.where(lane_i == 0, p_g * p1 / den, jnp.where(lane_i == 1, p_g * p2 / den, 0.0))
    return eid, gate


def _to_row_tiles(x):
    return x.astype(BF16).reshape(x.shape[0], x.shape[1] // V7X_LANES, V7X_LANES)


def _from_row_tiles(x3):
    return x3.reshape(x3.shape[0], x3.shape[1] * x3.shape[2])


def _outproj_kernel(oda_ref, orw_ref, wt_ref, wb_ref, x_ref, n2_ref, rcat_ref, rb_ref,
                    h_ref, xn_ref, eid_ref, gate_ref, *, tm, eps):
    acc = (jnp.dot(oda_ref[...], wt_ref[...], preferred_element_type=F32)
           + jnp.dot(orw_ref[...], wb_ref[...], preferred_element_type=F32))
    h = x_ref[...] + acc
    h_ref[...] = h
    ms = jnp.mean(h * h, axis=-1, keepdims=True)
    xn = h * lax.rsqrt(ms + eps) * n2_ref[...]
    xn_ref[...] = _to_row_tiles(xn)
    hi = xn.astype(BF16)
    lo = (xn - hi.astype(F32)).astype(BF16)
    hw = jnp.dot(hi, rcat_ref[...], preferred_element_type=F32)
    lg = (hw[:, 0:V7X_LANES] + hw[:, V7X_LANES:2 * V7X_LANES]
          + jnp.dot(lo, rcat_ref[:, 0:V7X_LANES], preferred_element_type=F32)) + rb_ref[...]
    eid, gate = _route(lg)
    eid_ref[...] = eid
    gate_ref[...] = gate


def _outproj_route(oda, orw, w_out, x2, n2, wg, bg, we, be, *, tm):
    T, D = x2.shape
    w1 = oda.shape[1]
    w2 = orw.shape[1]
    nl = D // V7X_LANES
    wt = w_out[:w1].astype(BF16)
    wb = w_out[w1:].astype(BF16)
    pad = V7X_LANES - N_GROUPS - N_EXPERTS
    wr = jnp.concatenate([wg, we, jnp.zeros((D, pad), F32)], axis=1)
    rb = jnp.concatenate([bg, be, jnp.zeros((pad,), F32)]).reshape(1, V7X_LANES)
    rhi = wr.astype(BF16)
    rlo = (wr - rhi.astype(F32)).astype(BF16)
    rcat = jnp.concatenate([rhi, rlo], axis=1)
    const = lambda shape: pl.BlockSpec(shape, lambda i: (0, 0))
    return pl.pallas_call(
        functools.partial(_outproj_kernel, tm=tm, eps=NORM_EPS),
        grid=(T // tm,),
        in_specs=[pl.BlockSpec((tm, w1), lambda i: (i, 0)), pl.BlockSpec((tm, w2), lambda i: (i, 0)),
                  const((w1, D)), const((w2, D)), pl.BlockSpec((tm, D), lambda i: (i, 0)), const((1, D)),
                  const((D, 2 * V7X_LANES)), const((1, V7X_LANES))],
        out_specs=[pl.BlockSpec((tm, D), lambda i: (i, 0)),
                   pl.BlockSpec((tm, nl, V7X_LANES), lambda i: (i, 0, 0)),
                   pl.BlockSpec((tm, V7X_LANES), lambda i: (i, 0)),
                   pl.BlockSpec((tm, V7X_LANES), lambda i: (i, 0))],
        out_shape=[jax.ShapeDtypeStruct((T, D), F32),
                   jax.ShapeDtypeStruct((T, nl, V7X_LANES), BF16),
                   jax.ShapeDtypeStruct((T, V7X_LANES), jnp.int32),
                   jax.ShapeDtypeStruct((T, V7X_LANES), F32)],
        compiler_params=_cparams(("arbitrary",)),
        name="outproj_route",
    )(oda, orw, wt, wb, x2, n2.reshape(1, D), rcat, rb)


MOE_DMA_UNROLL = 8


def _moe_kernel(blk_e_ref, nxt_e_ref, par_ref, nused_ref, tok_ref, tokn_ref, dst_ref,
                xn_hbm, w1_hbm, w3_hbm, w2_hbm, ys_hbm,
                xbuf, ybuf, wf1, wf3, wf2, w1b, w3b, w2b, sem_in, sem_out, sem_w, *, bm, nl, n_slots):
    i = pl.program_id(0)
    n_used = nused_ref[0]
    slot = i % 2

    def gather(ids_ref, dst_slot):
        def body(r, c):
            pltpu.make_async_copy(xn_hbm.at[ids_ref[0, r]], xbuf.at[dst_slot, r], sem_in.at[dst_slot]).start()
            return c
        lax.fori_loop(0, bm, body, 0, unroll=MOE_DMA_UNROLL)

    def wait_gather(src_slot):
        pltpu.make_async_copy(xn_hbm.at[pl.ds(0, bm)], xbuf.at[src_slot], sem_in.at[src_slot]).wait()

    def scatter():
        def body(r, c):
            d = dst_ref[0, r]
            pltpu.make_async_copy(ybuf.at[r // TOP_K, r % TOP_K], ys_hbm.at[d // TOP_K, d % TOP_K], sem_out).start()
            return c
        lax.fori_loop(0, bm, body, 0, unroll=MOE_DMA_UNROLL)

    def wait_scatter():
        pltpu.make_async_copy(ybuf, ys_hbm.at[pl.ds(0, bm // TOP_K)], sem_out).wait()

    def weight_copies(e, wslot):
        return (pltpu.make_async_copy(w1_hbm.at[e], wf1.at[wslot], sem_w.at[wslot]),
                pltpu.make_async_copy(w3_hbm.at[e], wf3.at[wslot], sem_w.at[wslot]),
                pltpu.make_async_copy(w2_hbm.at[e], wf2.at[wslot], sem_w.at[wslot]))

    @pl.when(i == 0)
    def _prologue():
        for cp in weight_copies(blk_e_ref[0], par_ref[0]):
            cp.start()
        gather(tok_ref, 0)
        ybuf[...] = jnp.zeros(ybuf.shape, ybuf.dtype)
        spare = pltpu.make_async_copy(ybuf, ys_hbm.at[pl.ds(n_slots // TOP_K, bm // TOP_K)], sem_out)
        spare.start()
        spare.wait()

    @pl.when(i + 1 < n_used)
    def _prefetch():
        gather(tokn_ref, 1 - slot)

    @pl.when(i < n_used)
    def _active():
        e = blk_e_ref[i]
        first = jnp.logical_or(i == 0, e != blk_e_ref[jnp.maximum(i - 1, 0)])

        @pl.when(first)
        def _new_expert():
            wslot = par_ref[i]
            for cp in weight_copies(e, wslot):
                cp.wait()
            nxt = nxt_e_ref[i]

            @pl.when(nxt >= 0)
            def _():
                for cp in weight_copies(nxt, 1 - wslot):
                    cp.start()

            w1b[...] = wf1[wslot].astype(BF16)
            w3b[...] = wf3[wslot].astype(BF16)
            w2b[...] = wf2[wslot].astype(BF16)

        wait_gather(slot)
        x = _from_row_tiles(xbuf[slot])
        h1 = jnp.dot(x, w1b[...], preferred_element_type=F32)
        h3 = jnp.dot(x, w3b[...], preferred_element_type=F32)
        hh = (h1 * _sigmoid(h1) * h3).astype(BF16)
        y = jnp.dot(hh, w2b[...], preferred_element_type=F32)

        @pl.when(i > 0)
        def _drain_prev():
            wait_scatter()

        ybuf[...] = _to_row_tiles(y).reshape(ybuf.shape)
        scatter()

        @pl.when(i == n_used - 1)
        def _drain_last():
            wait_scatter()


def _moe_ffn(xn_lines, plan, w1, w3, w2, *, bm, n_slots):
    blk_e, nxt_e, par, n_used, tok_p, dst_p = plan
    E, D, DE = w1.shape
    nl = D // V7X_LANES
    nb = blk_e.shape[0]
    ids = lambda imap: pl.BlockSpec((None, 1, bm), imap, memory_space=pltpu.SMEM)
    grid_spec = pltpu.PrefetchScalarGridSpec(
        num_scalar_prefetch=4,
        grid=(nb,),
        in_specs=[
            ids(lambda i, *_: (i, 0, 0)),
            ids(lambda i, *_: (jnp.minimum(i + 1, nb - 1), 0, 0)),
            ids(lambda i, *_: (i, 0, 0)),
            pl.BlockSpec(memory_space=pl.ANY),
            pl.BlockSpec(memory_space=pl.ANY),
            pl.BlockSpec(memory_space=pl.ANY),
            pl.BlockSpec(memory_space=pl.ANY),
        ],
        out_specs=pl.BlockSpec(memory_space=pl.ANY),
        scratch_shapes=[pltpu.VMEM((2, bm, nl, V7X_LANES), BF16), pltpu.VMEM((bm // TOP_K, TOP_K, nl, V7X_LANES), BF16),
                        pltpu.VMEM((2, D, DE), F32), pltpu.VMEM((2, D, DE), F32), pltpu.VMEM((2, DE, D), F32),
                        pltpu.VMEM((D, DE), BF16), pltpu.VMEM((D, DE), BF16), pltpu.VMEM((DE, D), BF16),
                        pltpu.SemaphoreType.DMA((2,)), pltpu.SemaphoreType.DMA, pltpu.SemaphoreType.DMA((2,))],
    )
    tok3 = tok_p.reshape(nb, 1, bm)
    return pl.pallas_call(
        functools.partial(_moe_kernel, bm=bm, nl=nl, n_slots=n_slots),
        grid_spec=grid_spec,
        out_shape=jax.ShapeDtypeStruct(((n_slots + bm) // TOP_K, TOP_K, nl, V7X_LANES), BF16),
        compiler_params=_cparams(("arbitrary",)),
        name="moe_ffn",
    )(blk_e, nxt_e, par, n_used, tok3, tok3, dst_p.reshape(nb, 1, bm), xn_lines, w1, w3, w2)


def _moe_plan(eid, *, bm):
    T = eid.shape[0]
    M = T * TOP_K
    E = N_EXPERTS
    i32 = jnp.int32
    e_flat = eid.reshape(M)
    order = jnp.argsort(e_flat).astype(i32)
    experts = jnp.arange(E, dtype=i32)
    counts = jnp.sum((e_flat[:, None] == experts[None, :]).astype(i32), axis=0)
    start = jnp.cumsum(counts) - counts
    padded = (counts + bm - 1) // bm * bm
    pend = jnp.cumsum(padded)
    pstart = pend - padded
    nb = M // bm + E
    blk = jnp.arange(nb, dtype=i32)
    blk_start = blk * bm
    n_used = (pend[-1] // bm).astype(i32)
    blk_e = jnp.minimum(jnp.sum((blk_start[:, None] >= pend[None, :]).astype(i32), axis=1), E - 1)
    blk_e = jnp.where(blk < n_used, blk_e, blk_e[jnp.maximum(n_used - 1, 0)])
    off = blk_start - pstart[blk_e]
    base = start[blk_e] + off
    valid = jnp.where(blk < n_used, jnp.clip(counts[blk_e] - off, 0, bm), 0)
    r = jnp.arange(bm, dtype=i32)
    slot_p = order[jnp.clip(base[:, None] + r[None, :], 0, M - 1)]
    tok_p = slot_p // TOP_K
    dst_p = jnp.where(r[None, :] < valid[:, None], slot_p, M + r[None, :])
    seg_first = jnp.concatenate([jnp.ones((1,), bool), blk_e[1:] != blk_e[:-1]])
    par = (jnp.cumsum(seg_first.astype(i32)) - 1) % 2
    later = (experts[None, :] > experts[:, None]) & (counts[None, :] > 0)
    nxt_of = jnp.min(jnp.where(later, experts[None, :], E), axis=1)
    nxt_e = jnp.where(nxt_of == E, -1, nxt_of)[blk_e]
    return blk_e, nxt_e.astype(i32), par.astype(i32), n_used.reshape(1), tok_p, dst_p


def _combine_kernel(h_ref, ys_ref, gate_ref, fw_ref, o_ref, *, tm, nl, eps):
    gate = gate_ref[...]
    acc = h_ref[...]
    for j in range(TOP_K):
        yj = _from_row_tiles(ys_ref[:, j, :, :]).astype(F32)
        acc = acc + yj * gate[:, j:j + 1]
    ms = jnp.mean(acc * acc, axis=-1, keepdims=True)
    o_ref[...] = acc * lax.rsqrt(ms + eps) * fw_ref[...]


def _combine(h, ys, gate, fw, *, tm):
    T, D = h.shape
    nl = D // V7X_LANES
    return pl.pallas_call(
        functools.partial(_combine_kernel, tm=tm, nl=nl, eps=NORM_EPS),
        grid=(T // tm,),
        in_specs=[pl.BlockSpec((tm, D), lambda i: (i, 0)),
                  pl.BlockSpec((tm, TOP_K, nl, V7X_LANES), lambda i: (i, 0, 0, 0)),
                  pl.BlockSpec((tm, V7X_LANES), lambda i: (i, 0)),
                  pl.BlockSpec((1, D), lambda i: (0, 0))],
        out_specs=pl.BlockSpec((tm, D), lambda i: (i, 0)),
        out_shape=jax.ShapeDtypeStruct((T, D), F32),
        compiler_params=_cparams(("arbitrary",)),
        name="combine_norm",
    )(h, ys, gate, fw.reshape(1, D))


def _tiles(T, seq, D):
    pick = lambda n, prefs: next(p for p in prefs if n % p == 0)
    return dict(
        tm_in=pick(seq, (512, 256, 128)),
        tm_rw=pick(seq, (1024, 512, 256, 128)),
        tn_da=1024,
        tq=pick(seq, (1024, 512, 256, 128)),
        tk=pick(seq, (256, 128)),
        rw_L=pick(seq, (256, 128, 64)),
        rw_PW=8,
        tm_out=pick(T, (256, 128)),
        bm=256,
        tm_cmb=pick(T, (256, 128)),
    )


def kernel(x, norm1_w, w_in, lam_q1, lam_k1, lam_q2, lam_k2, subln_w, rw_mu, rw_w0, rw_w_up, rw_a0, rw_a_up, rw_g_up, rw_k_k, rw_k_a, rw_r_k, rw_lnx_w, rw_lnx_b, w_out, norm2_w, router_group_w, router_group_b, router_expert_w, router_expert_b, moe_w1, moe_w3, moe_w2, final_norm_w):
    B, S, D = x.shape
    T = B * S
    depth = w_in.shape[0]
    rw_heads = rw_w0.shape[1] // RW_HEAD_DIM
    rw_cols = rw_mu.shape[1]
    da_cols = w_in.shape[2] - rw_cols
    da_width = da_cols // 3
    da_heads = da_width // (2 * DA_HEAD_DIM)
    tl = _tiles(T, S, D)
    cos, sin = _rope_tables(S)
    q_scale = DA_HEAD_DIM ** -0.5 * math.log2(math.e)
    colscale = jnp.concatenate([jnp.full((da_width,), q_scale, F32), jnp.ones((da_width,), F32)])

    h = x.reshape(T, D)
    for l in range(depth):
        assert l == 0, "lam_init is specialised to the first layer"
        w_qk = (w_in[l][:, :2 * da_width] * colscale)[:, _rope_column_order(2 * da_width)].astype(BF16)
        w_v = w_in[l][:, 2 * da_width:da_cols].T.astype(BF16)
        w_rw = w_in[l][:, da_cols:].astype(BF16)
        zqk = _inproj(h, norm1_w[l], w_qk, seq=S, tm=tl["tm_in"], tn=tl["tn_da"], out_dtype=BF16,
                      mode="rope", cos=cos, sin=sin)
        vt = _inproj(h, norm1_w[l], w_v, seq=S, tm=tl["tm_in"], tn=tl["tn_da"], out_dtype=BF16,
                     mode="transposed", tkv=tl["tk"])
        zrw = _inproj(h, norm1_w[l], w_rw, seq=S, tm=tl["tm_rw"], tn=rw_cols // 2, out_dtype=BF16)
        lam = (jnp.exp(jnp.sum(lam_q1[l] * lam_k1[l])) - jnp.exp(jnp.sum(lam_q2[l] * lam_k2[l])) + LAM_INIT)
        o_da = _diff_attention(zqk, vt, lam.astype(F32), subln_w[l], batch=B, seq=S, n_heads=da_heads,
                               tq=tl["tq"], tk=tl["tk"])
        o_rw = _rwkv7(zrw, rw_mu[l], rw_w0[l], rw_w_up[l], rw_a0[l], rw_a_up[l], rw_g_up[l], rw_k_k[l], rw_k_a[l],
                      rw_r_k[l], rw_lnx_w[l], rw_lnx_b[l], batch=B, seq=S, n_heads=rw_heads,
                      L=tl["rw_L"], PW=tl["rw_PW"])
        h, xn_lines, eid, gate = _outproj_route(o_da, o_rw, w_out[l], h, norm2_w[l], router_group_w[l],
                                                router_group_b[l], router_expert_w[l], router_expert_b[l],
                                                tm=tl["tm_out"])
        plan = _moe_plan(eid[:, :TOP_K], bm=tl["bm"])
        ys = _moe_ffn(xn_lines, plan, moe_w1[l], moe_w3[l], moe_w2[l], bm=tl["bm"], n_slots=T * TOP_K)
        assert depth == 1, "the final norm is fused into the last layer's combine"
        out = _combine(h, ys, gate, final_norm_w, tm=tl["tm_cmb"])
    return out.reshape(B, S, D)
```

```python
import functools
import math

import jax
import jax.numpy as jnp
from jax import lax
from jax.experimental import pallas as pl
from jax.experimental.pallas import tpu as pltpu

F32 = jnp.float32
BF16 = jnp.bfloat16

DA_HEAD_DIM = 64
RW_HEAD_DIM = 64
RW_DECAY_RANK = 64
RW_ICLR_RANK = 64
RW_GATE_RANK = 128
ROPE_THETA = 10000.0
N_GROUPS = 8
EXPERTS_PER_GROUP = 8
N_EXPERTS = N_GROUPS * EXPERTS_PER_GROUP
TOP_K = 2
NORM_EPS = 1e-6
SUBLN_EPS = 1e-5
RW_GN_EPS = 64e-5
LAM_INIT = 0.8 - 0.6 * math.exp(-0.3 * 0)

V7X_LANES = 128
V7X_VMEM_LIMIT = 56 * 1024 * 1024
NEG_BIG = -1e30
ATTN_KV_PER_STEP = 4
ATTN_LOOKAHEAD = 8


def _cparams(sem):
    return pltpu.CompilerParams(dimension_semantics=sem, vmem_limit_bytes=V7X_VMEM_LIMIT)


def _inproj_kernel(*refs, mode, tn, tkv, eps):
    if mode == "rope":
        x_ref, nw_ref, w_ref, cos_ref, sin_ref, o_ref, xn_ref = refs
    else:
        x_ref, nw_ref, w_ref, o_ref, xn_ref = refs
    j = pl.program_id(1)

    @pl.when(j == 0)
    def _norm():
        x = x_ref[...]
        ms = jnp.mean(x * x, axis=-1, keepdims=True)
        xn_ref[...] = (x * lax.rsqrt(ms + eps) * nw_ref[...]).astype(BF16)

    if mode == "transposed":
        acc_t = lax.dot_general(w_ref[...], xn_ref[...], (((1,), (1,)), ((), ())), preferred_element_type=F32)
        for c in range(acc_t.shape[1] // tkv):
            o_ref[c] = acc_t[:, c * tkv:(c + 1) * tkv].astype(o_ref.dtype)
        return

    acc = jnp.dot(xn_ref[...], w_ref[...], preferred_element_type=F32)
    if mode == "rope":
        cos = cos_ref[...]
        sin = sin_ref[...]
        for c in range(tn // V7X_LANES):
            cols = slice(c * V7X_LANES, (c + 1) * V7X_LANES)
            blk = acc[:, cols]
            o_ref[:, cols] = (blk * cos + pltpu.roll(blk, V7X_LANES // 2, 1) * sin).astype(o_ref.dtype)
    else:
        o_ref[...] = acc.astype(o_ref.dtype)


def _inproj(x2, nw, w, *, seq, tm, tn, out_dtype, mode="plain", cos=None, sin=None, tkv=None):
    T, D = x2.shape
    N = w.shape[0] if mode == "transposed" else w.shape[1]
    assert T % tm == 0 and N % tn == 0 and seq % tm == 0
    w_spec = (pl.BlockSpec((tn, D), lambda i, j: (j, 0)) if mode == "transposed"
              else pl.BlockSpec((D, tn), lambda i, j: (0, j)))
    in_specs = [
        pl.BlockSpec((tm, D), lambda i, j: (i, 0)),
        pl.BlockSpec((1, D), lambda i, j: (0, 0)),
        w_spec,
    ]
    args = [x2, nw.reshape(1, D), w]
    if mode == "rope":
        ns = seq // tm
        in_specs += [pl.BlockSpec((tm, V7X_LANES), lambda i, j: (i % ns, 0)),
                     pl.BlockSpec((tm, V7X_LANES), lambda i, j: (i % ns, 0))]
        args += [cos, sin]
    if mode == "transposed":
        assert tm % tkv == 0
        out_specs = pl.BlockSpec((tm // tkv, tn, tkv), lambda i, j: (i, j, 0))
        out_shape = jax.ShapeDtypeStruct((T // tkv, N, tkv), out_dtype)
    else:
        out_specs = pl.BlockSpec((tm, tn), lambda i, j: (i, j))
        out_shape = jax.ShapeDtypeStruct((T, N), out_dtype)
    return pl.pallas_call(
        functools.partial(_inproj_kernel, mode=mode, tn=tn, tkv=tkv, eps=NORM_EPS),
        grid=(T // tm, N // tn),
        in_specs=in_specs,
        out_specs=out_specs,
        out_shape=out_shape,
        scratch_shapes=[pltpu.VMEM((tm, D), BF16)],
        compiler_params=_cparams(("arbitrary", "arbitrary")),
        name="inproj_" + mode,
    )(*args)


def _rope_column_order(width):
    half = DA_HEAD_DIM // 2
    blk = jnp.concatenate([jnp.arange(0, half), jnp.arange(2 * half, 3 * half),
                           jnp.arange(half, 2 * half), jnp.arange(3 * half, 4 * half)])
    cols = jnp.arange(width)
    return (cols // V7X_LANES) * V7X_LANES + blk[cols % V7X_LANES]


def _rope_tables(seq):
    half = DA_HEAD_DIM // 2
    inv = ROPE_THETA ** (-jnp.arange(half, dtype=F32) / half)
    ang = jnp.arange(seq, dtype=F32)[:, None] * inv[None, :]
    cos = jnp.cos(ang)
    sin = jnp.sin(ang)
    return jnp.tile(cos, (1, 4)), jnp.concatenate([-sin, -sin, sin, sin], axis=-1)


def _attn_kernel(lam_ref, q_ref, k_ref, vt_ref, sw_ref, o_ref, qs_ref, m_ref, l_ref, acc_ref, *, tq, tk):
    qi = pl.program_id(2)
    d = DA_HEAD_DIM
    q = q_ref[...]
    lane = lax.broadcasted_iota(jnp.int32, q.shape, 1)
    comp1 = (lane % d) < (d // 2)
    zero = jnp.zeros_like(q)
    qs_ref[0:tq, :] = jnp.where(comp1, q, zero)
    qs_ref[tq:2 * tq, :] = jnp.where(comp1, zero, q)
    m_ref[...] = jnp.full(m_ref.shape, NEG_BIG, F32)
    l_ref[...] = jnp.zeros(l_ref.shape, F32)
    acc_ref[...] = jnp.zeros(acc_ref.shape, F32)

    n_diag = tq // tk
    n_full = qi * n_diag

    n_strips = 2 * tq // tk

    def strip_scores(k, si, on_diagonal):
        s = lax.dot_general(k, qs_ref[si * tk:(si + 1) * tk, :], (((1,), (1,)), ((), ())),
                            preferred_element_type=F32)
        if on_diagonal:
            kpos = lax.broadcasted_iota(jnp.int32, s.shape, 0)
            qpos = lax.broadcasted_iota(jnp.int32, s.shape, 1)
            s = jnp.where(kpos <= qpos, s, NEG_BIG)
        return s

    def strip_update(vt, si, s):
        lanes = slice(si * tk, (si + 1) * tk)
        m_old = m_ref[:, lanes]
        m_new = jnp.maximum(m_old, jnp.max(s, axis=0, keepdims=True))
        alpha = jnp.exp2(m_old - m_new)
        p = jnp.exp2(s - m_new)
        l_ref[:, lanes] = alpha * l_ref[:, lanes] + jnp.sum(p, axis=0, keepdims=True)
        acc_ref[:, lanes] = alpha * acc_ref[:, lanes] + jnp.dot(vt, p.astype(BF16), preferred_element_type=F32)
        m_ref[:, lanes] = m_new

    def kv_blocks(blocks):
        kv = [(k_ref[pl.ds(pl.multiple_of(j * tk, tk), tk), :], vt_ref[j]) for j, _ in blocks]
        work = [(b, si, dg) for b, (_, strips) in enumerate(blocks) for si, dg in strips]
        pending = [strip_scores(kv[b][0], si, dg) for b, si, dg in work[:ATTN_LOOKAHEAD]]
        for n, (b, si, _) in enumerate(work):
            if n + ATTN_LOOKAHEAD < len(work):
                nb, nsi, ndg = work[n + ATTN_LOOKAHEAD]
                pending.append(strip_scores(kv[nb][0], nsi, ndg))
            strip_update(kv[b][1], si, pending[n])

    all_strips = [(si, False) for si in range(n_strips)]

    def full_steps(i, c):
        kv_blocks([(i * ATTN_KV_PER_STEP + u, all_strips) for u in range(ATTN_KV_PER_STEP)])
        return c

    lax.fori_loop(0, n_full // ATTN_KV_PER_STEP, full_steps, 0)

    def full_step(j, c):
        kv_blocks([(j, all_strips)])
        return c

    lax.fori_loop(n_full // ATTN_KV_PER_STEP * ATTN_KV_PER_STEP, n_full, full_step, 0)

    kv_blocks([(n_full + c, [(si, si % n_diag == c) for si in range(n_strips) if si % n_diag >= c])
               for c in range(n_diag)])

    o = acc_ref[...] / l_ref[...]
    od = o[:, 0:tq] - lam_ref[0] * o[:, tq:2 * tq]
    ms = jnp.mean(od * od, axis=0, keepdims=True)
    on = (od * lax.rsqrt(ms + SUBLN_EPS) * sw_ref[...]) * (1.0 - LAM_INIT)
    o_ref[...] = on.T.astype(o_ref.dtype)


def _diff_attention(zqk, vt, lam, subln_w, *, batch, seq, n_heads, tq, tk):
    T = zqk.shape[0]
    hw = 2 * DA_HEAD_DIM
    nq = seq // tq
    nk = seq // tk
    sw_b = jnp.broadcast_to(subln_w.reshape(hw, 1), (hw, tq)).astype(F32)
    return pl.pallas_call(
        functools.partial(_attn_kernel, tq=tq, tk=tk),
        grid=(batch, n_heads, nq),
        in_specs=[
            pl.BlockSpec(memory_space=pltpu.SMEM),
            pl.BlockSpec((tq, hw), lambda b, h, i: (b * nq + i, h)),
            pl.BlockSpec((seq, hw), lambda b, h, i: (b, n_heads + h)),
            pl.BlockSpec((nk, hw, tk), lambda b, h, i: (b, h, 0)),
            pl.BlockSpec((hw, tq), lambda b, h, i: (0, 0)),
        ],
        out_specs=pl.BlockSpec((tq, hw), lambda b, h, i: (b * nq + i, h)),
        out_shape=jax.ShapeDtypeStruct((T, n_heads * hw), BF16),
        scratch_shapes=[
            pltpu.VMEM((2 * tq, hw), BF16),
            pltpu.VMEM((1, 2 * tq), F32),
            pltpu.VMEM((1, 2 * tq), F32),
            pltpu.VMEM((hw, 2 * tq), F32),
        ],
        compiler_params=_cparams(("arbitrary", "arbitrary", "arbitrary")),
        name="diff_attn",
    )(lam.reshape(1), zqk, zqk, vt, sw_b)


RW_CHUNK = 64
RW_PAIR = 2 * RW_HEAD_DIM
RW_INV_BLOCK = 16


def _bmm(a, b):
    return jnp.einsum("bij,bjk->bik", a.astype(BF16), b.astype(BF16), preferred_element_type=F32)


def _bmm_nt(a, b):
    return jnp.einsum("bik,bjk->bij", a.astype(BF16), b.astype(BF16), preferred_element_type=F32)


def _bmm_tn(a, b):
    return jnp.einsum("bti,btj->bij", a.astype(BF16), b.astype(BF16), preferred_element_type=F32)


def _mm_split(x, e):
    hi = x.astype(BF16)
    lo = (x - hi.astype(F32)).astype(BF16)
    return (jnp.dot(hi, e, preferred_element_type=F32) + jnp.dot(lo, e, preferred_element_type=F32))


def _sigmoid(x):
    return 1.0 / (1.0 + jnp.exp(-x))


def _softplus(x):
    return jnp.maximum(x, 0.0) + jnp.log(1.0 + jnp.exp(-jnp.abs(x)))


def _unit_lower_inverse(a, eye, diag_blk):
    ad = jnp.where(diag_blk, a, 0.0)
    ao = a - ad
    a2 = _bmm(ad, ad)
    a4 = _bmm(a2, a2)
    a8 = _bmm(a4, a4)
    td = eye + ad
    td = td + _bmm(td, a2)
    td = td + _bmm(td, a4)
    td = td + _bmm(td, a8)
    n1 = _bmm(td, ao)
    n2 = _bmm(n1, n1)
    x = td + _bmm(n2, td)
    return x + _bmm(n1, x)


def _rwkv_kernel(zr_ref, zk_ref, zv_ref, zwa_ref, zg_ref, mur_ref, muk_ref, muv_ref, muwa_ref, mug_ref,
                 vec_ref, ww_ref, wa_ref, gup_ref, o_ref,
                 cr_ref, ck_ref, cv_ref, cwa_ref, cg_ref, state_ref, *, L, PW):
    t = pl.program_id(2)
    C = RW_CHUNK
    PL = RW_PAIR
    NC = L // C

    @pl.when(t == 0)
    def _reset():
        for c_ref in (cr_ref, ck_ref, cv_ref, cwa_ref, cg_ref):
            c_ref[...] = jnp.zeros(c_ref.shape, F32)
        state_ref[...] = jnp.zeros(state_ref.shape, F32)

    def shift_mix(z_ref, mu_ref, c_ref):
        z = z_ref[...].astype(F32)
        row = lax.broadcasted_iota(jnp.int32, z.shape, 0)
        zprev = jnp.where(row == 0, c_ref[0:1, :], pltpu.roll(z, 1, 0))
        c_ref[0:1, :] = z[L - 1:L, :]
        return z + (zprev - z) * mu_ref[...]

    r = shift_mix(zr_ref, mur_ref, cr_ref)
    k = shift_mix(zk_ref, muk_ref, ck_ref)
    v = shift_mix(zv_ref, muv_ref, cv_ref)
    zwa = shift_mix(zwa_ref, muwa_ref, cwa_ref)
    zg = shift_mix(zg_ref, mug_ref, cg_ref)

    ri = lax.broadcasted_iota(jnp.int32, (PL, PL), 0)
    ci = lax.broadcasted_iota(jnp.int32, (PL, PL), 1)
    eye = (ri == ci).astype(F32)
    strict = ci < ri
    incl = ci <= ri
    diag_blk = (ri // RW_INV_BLOCK) == (ci // RW_INV_BLOCK)
    seg_ones = ((ri // RW_HEAD_DIM) == (ci // RW_HEAD_DIM)).astype(BF16)
    rc = lax.broadcasted_iota(jnp.int32, (C, C), 0)
    cc = lax.broadcasted_iota(jnp.int32, (C, C), 1)
    tri_incl = (cc <= rc).astype(BF16)
    head0 = lax.broadcasted_iota(jnp.int32, (C, PL), 1) < RW_HEAD_DIM
    lanes = [slice(pi * PL, (pi + 1) * PL) for pi in range(PW)]
    rows = [slice(c * C, (c + 1) * C) for c in range(NC)]

    def seg_sum(x):
        return jnp.concatenate([_mm_split(x[:, sl], seg_ones) for sl in lanes], axis=1)

    w0, a0, k_k, k_a, r_k, lnx_w, lnx_b = (vec_ref[i:i + 1, :] for i in range(7))
    w_pre = w0 + jnp.dot(jnp.tanh(zwa).astype(BF16), ww_ref[...], preferred_element_type=F32)
    logdec = -jnp.exp(-_softplus(-w_pre) - 0.5)
    a = _sigmoid(a0 + jnp.dot(zwa.astype(BF16), wa_ref[...], preferred_element_type=F32))
    g = jnp.dot(_sigmoid(zg).astype(BF16), gup_ref[...], preferred_element_type=F32)
    kk = k * k_k
    kkn = kk / jnp.maximum(jnp.sqrt(seg_sum(kk * kk)), 1e-12)
    kf = k * (1.0 + (a - 1.0) * k_a)
    a_s = -kkn
    b_s = kkn * a
    bonus = seg_sum(r * kf * r_k) * v

    ld_hi = logdec.astype(BF16)
    ld_lo = (logdec - ld_hi.astype(F32)).astype(BF16)
    cum_c = [jnp.dot(tri_incl, ld_hi[rs], preferred_element_type=F32)
             + jnp.dot(tri_incl, ld_lo[rs], preferred_element_type=F32) for rs in rows]
    tot_c = [cu[C - 1:C, :] for cu in cum_c]
    cum = jnp.concatenate(cum_c, axis=0)
    tot = jnp.concatenate([jnp.broadcast_to(tc, (C, tc.shape[1])) for tc in tot_c], axis=0)
    p_inv = jnp.exp(-cum)
    p_end = jnp.exp(tot - cum)

    def stack(x):
        out = []
        for rs in rows:
            for sl in lanes:
                blk = x[rs, sl]
                out.append(jnp.concatenate([jnp.where(head0, blk, 0.0), jnp.where(head0, 0.0, blk)], axis=0))
        return jnp.stack(out, axis=0).astype(BF16)

    rt_s = stack(r * jnp.exp(cum))
    at_s = stack(a_s * jnp.exp(cum - logdec))
    kt_s = stack(kf * p_inv)
    bt_s = stack(b_s * p_inv)
    kh_s = stack(kf * p_end)
    bh_s = stack(b_s * p_end)
    v_s = stack(v)

    m1 = _bmm_nt(jnp.concatenate([at_s, rt_s], axis=1), jnp.concatenate([bt_s, kt_s], axis=1))
    a_ab = jnp.where(strict, m1[:, 0:PL, 0:PL], 0.0)
    a_ak = jnp.where(strict, m1[:, 0:PL, PL:2 * PL], 0.0)
    a_rb = jnp.where(incl, m1[:, PL:2 * PL, 0:PL], 0.0)
    a_rk = jnp.where(incl, m1[:, PL:2 * PL, PL:2 * PL], 0.0)
    tinv = _unit_lower_inverse(a_ab, eye, diag_blk)
    wu = _bmm(tinv, jnp.concatenate([at_s, _bmm(a_ak, v_s).astype(BF16)], axis=2))

    s = state_ref[...]
    us, s0s = [], []
    for c in range(NC):
        sel = slice(c * PW, (c + 1) * PW)
        sb = s.astype(BF16)
        u = _bmm_nt(wu[sel, :, 0:PL], sb) + wu[sel, :, PL:2 * PL]
        us.append(u)
        s0s.append(sb)
        p_tot = jnp.stack([jnp.exp(tot_c[c][:, sl]) for sl in lanes], axis=0)
        s = s * p_tot + _bmm_tn(jnp.concatenate([u.astype(BF16), v_s[sel]], axis=1),
                                jnp.concatenate([bh_s[sel], kh_s[sel]], axis=1))
    state_ref[...] = s

    u_all = jnp.concatenate(us, axis=0)
    s0_all = jnp.concatenate(s0s, axis=0)
    y2 = _bmm_nt(rt_s, s0_all) + _bmm(a_rb, u_all) + _bmm(a_rk, v_s)
    y2 = y2[:, 0:C, :] + y2[:, C:2 * C, :]
    y = jnp.concatenate([jnp.concatenate([y2[c * PW + pi] for pi in range(PW)], axis=1) for c in range(NC)], axis=0)

    inv_n = 1.0 / RW_HEAD_DIM
    mean = seg_sum(y) * inv_n
    yc = y - mean
    var = seg_sum(yc * yc) * inv_n
    yn = yc * lax.rsqrt(var + RW_GN_EPS) * lnx_w + lnx_b
    o_ref[...] = ((yn + bonus) * g).astype(o_ref.dtype)


def _rwkv7(zrw, mu, w0, w_up, a0, a_up, g_up, k_k, k_a, r_k, lnx_w, lnx_b, *, batch, seq, n_heads, L, PW):
    T = zrw.shape[0]
    HC = n_heads * RW_HEAD_DIM
    PWL = PW * RW_PAIR
    assert HC % PWL == 0 and seq % L == 0 and L % RW_CHUNK == 0
    ng = HC // PWL
    nt = seq // L
    lora_w = RW_DECAY_RANK + RW_ICLR_RANK
    assert lora_w == V7X_LANES and RW_GATE_RANK == V7X_LANES
    vecs = jnp.stack([w0, a0, k_k, k_a, r_k.reshape(HC), lnx_w, lnx_b, jnp.zeros((HC,), F32)], axis=0)
    ww = jnp.concatenate([w_up, jnp.zeros((RW_ICLR_RANK, HC), F32)], axis=0).astype(BF16)
    wa = jnp.concatenate([jnp.zeros((RW_DECAY_RANK, HC), F32), a_up], axis=0).astype(BF16)
    gw = g_up.astype(BF16)
    mu2 = mu.reshape(1, -1)
    cb = HC // PWL
    wa_blk = 3 * HC // V7X_LANES
    zrow = lambda b, g, t: b * nt + t
    big = lambda sec: pl.BlockSpec((L, PWL), lambda b, g, t: (zrow(b, g, t), sec * cb + g))
    small = lambda off: pl.BlockSpec((L, V7X_LANES), lambda b, g, t: (zrow(b, g, t), wa_blk + off))
    mu_big = lambda sec: pl.BlockSpec((1, PWL), lambda b, g, t: (0, sec * cb + g))
    mu_small = lambda off: pl.BlockSpec((1, V7X_LANES), lambda b, g, t: (0, wa_blk + off))
    wspec = pl.BlockSpec((V7X_LANES, PWL), lambda b, g, t: (0, g))
    return pl.pallas_call(
        functools.partial(_rwkv_kernel, L=L, PW=PW),
        grid=(batch, ng, nt),
        in_specs=[big(0), big(1), big(2), small(0), small(1),
                  mu_big(0), mu_big(1), mu_big(2), mu_small(0), mu_small(1),
                  pl.BlockSpec((8, PWL), lambda b, g, t: (0, g)),
                  wspec, wspec, wspec],
        out_specs=pl.BlockSpec((L, PWL), lambda b, g, t: (zrow(b, g, t), g)),
        out_shape=jax.ShapeDtypeStruct((T, HC), BF16),
        scratch_shapes=[pltpu.VMEM((8, PWL), F32), pltpu.VMEM((8, PWL), F32), pltpu.VMEM((8, PWL), F32),
                        pltpu.VMEM((8, V7X_LANES), F32), pltpu.VMEM((8, V7X_LANES), F32),
                        pltpu.VMEM((PW, RW_PAIR, RW_PAIR), F32)],
        compiler_params=_cparams(("arbitrary", "arbitrary", "arbitrary")),
        name="rwkv7_chunked",
    )(zrw, zrw, zrw, zrw, zrw, mu2, mu2, mu2, mu2, mu2, vecs, ww, wa, gw)


def _route(lg):
    lane_i = lax.broadcasted_iota(jnp.int32, lg.shape, 1)
    lane = lane_i.astype(F32)
    is_g = lane_i < N_GROUPS
    mg = jnp.max(jnp.where(is_g, lg, NEG_BIG), axis=-1, keepdims=True)
    eg = jnp.exp(jnp.where(is_g, lg - mg, NEG_BIG))
    pg = eg / jnp.sum(eg, axis=-1, keepdims=True)
    p_g = jnp.max(pg, axis=-1, keepdims=True)
    g_sel = jnp.min(jnp.where(is_g & (pg == p_g), lane, 1e9), axis=-1, keepdims=True)
    grp = (jnp.right_shift(lane_i, 3) - 1).astype(F32)
    is_e = (lane_i >= N_GROUPS) & (lane_i < N_GROUPS + N_EXPERTS) & (grp == g_sel)
    me = jnp.max(jnp.where(is_e, lg, NEG_BIG), axis=-1, keepdims=True)
    ee = jnp.exp(jnp.where(is_e, lg - me, NEG_BIG))
    pe = ee / jnp.sum(ee, axis=-1, keepdims=True)
    p1 = jnp.max(jnp.where(is_e, pe, -1.0), axis=-1, keepdims=True)
    i1 = jnp.min(jnp.where(is_e & (pe == p1), lane, 1e9), axis=-1, keepdims=True)
    rest = is_e & (lane != i1)
    p2 = jnp.max(jnp.where(rest, pe, -1.0), axis=-1, keepdims=True)
    i2 = jnp.min(jnp.where(rest & (pe == p2), lane, 1e9), axis=-1, keepdims=True)
    den = p1 + p2
    eid = jnp.where(lane_i == 0, i1 - N_GROUPS, jnp.where(lane_i == 1, i2 - N_GROUPS, 0.0)).astype(jnp.int32)
    gate = jnp.where(lane_i == 0, p_g * p1 / den, jnp.where(lane_i == 1, p_g * p2 / den, 0.0))
    return eid, gate


def _to_row_tiles(x):
    return x.astype(BF16).reshape(x.shape[0], x.shape[1] // V7X_LANES, V7X_LANES)


def _from_row_tiles(x3):
    return x3.reshape(x3.shape[0], x3.shape[1] * x3.shape[2])


def _outproj_kernel(oda_ref, orw_ref, wt_ref, wb_ref, x_ref, n2_ref, rcat_ref, rb_ref,
                    h_ref, xn_ref, eid_ref, gate_ref, *, tm, eps):
    acc = (jnp.dot(oda_ref[...], wt_ref[...], preferred_element_type=F32)
           + jnp.dot(orw_ref[...], wb_ref[...], preferred_element_type=F32))
    h = x_ref[...] + acc
    h_ref[...] = h
    ms = jnp.mean(h * h, axis=-1, keepdims=True)
    xn = h * lax.rsqrt(ms + eps) * n2_ref[...]
    xn_ref[...] = _to_row_tiles(xn)
    hi = xn.astype(BF16)
    lo = (xn - hi.astype(F32)).astype(BF16)
    hw = jnp.dot(hi, rcat_ref[...], preferred_element_type=F32)
    lg = (hw[:, 0:V7X_LANES] + hw[:, V7X_LANES:2 * V7X_LANES]
          + jnp.dot(lo, rcat_ref[:, 0:V7X_LANES], preferred_element_type=F32)) + rb_ref[...]
    eid, gate = _route(lg)
    eid_ref[...] = eid
    gate_ref[...] = gate


def _outproj_route(oda, orw, w_out, x2, n2, wg, bg, we, be, *, tm):
    T, D = x2.shape
    w1 = oda.shape[1]
    w2 = orw.shape[1]
    nl = D // V7X_LANES
    wt = w_out[:w1].astype(BF16)
    wb = w_out[w1:].astype(BF16)
    pad = V7X_LANES - N_GROUPS - N_EXPERTS
    wr = jnp.concatenate([wg, we, jnp.zeros((D, pad), F32)], axis=1)
    rb = jnp.concatenate([bg, be, jnp.zeros((pad,), F32)]).reshape(1, V7X_LANES)
    rhi = wr.astype(BF16)
    rlo = (wr - rhi.astype(F32)).astype(BF16)
    rcat = jnp.concatenate([rhi, rlo], axis=1)
    const = lambda shape: pl.BlockSpec(shape, lambda i: (0, 0))
    return pl.pallas_call(
        functools.partial(_outproj_kernel, tm=tm, eps=NORM_EPS),
        grid=(T // tm,),
        in_specs=[pl.BlockSpec((tm, w1), lambda i: (i, 0)), pl.BlockSpec((tm, w2), lambda i: (i, 0)),
                  const((w1, D)), const((w2, D)), pl.BlockSpec((tm, D), lambda i: (i, 0)), const((1, D)),
                  const((D, 2 * V7X_LANES)), const((1, V7X_LANES))],
        out_specs=[pl.BlockSpec((tm, D), lambda i: (i, 0)),
                   pl.BlockSpec((tm, nl, V7X_LANES), lambda i: (i, 0, 0)),
                   pl.BlockSpec((tm, V7X_LANES), lambda i: (i, 0)),
                   pl.BlockSpec((tm, V7X_LANES), lambda i: (i, 0))],
        out_shape=[jax.ShapeDtypeStruct((T, D), F32),
                   jax.ShapeDtypeStruct((T, nl, V7X_LANES), BF16),
                   jax.ShapeDtypeStruct((T, V7X_LANES), jnp.int32),
                   jax.ShapeDtypeStruct((T, V7X_LANES), F32)],
        compiler_params=_cparams(("arbitrary",)),
        name="outproj_route",
    )(oda, orw, wt, wb, x2, n2.reshape(1, D), rcat, rb)


MOE_DMA_UNROLL = 8


def _moe_kernel(blk_e_ref, nxt_e_ref, par_ref, nused_ref, tok_ref, tokn_ref, dst_ref,
                xn_hbm, w1_hbm, w3_hbm, w2_hbm, ys_hbm,
                xbuf, ybuf, wf1, wf3, wf2, w1b, w3b, w2b, sem_in, sem_out, sem_w, *, bm, nl, n_slots):
    i = pl.program_id(0)
    n_used = nused_ref[0]
    slot = i % 2

    def gather(ids_ref, dst_slot):
        def body(r, c):
            pltpu.make_async_copy(xn_hbm.at[ids_ref[0, r]], xbuf.at[dst_slot, r], sem_in.at[dst_slot]).start()
            return c
        lax.fori_loop(0, bm, body, 0, unroll=MOE_DMA_UNROLL)

    def wait_gather(src_slot):
        pltpu.make_async_copy(xn_hbm.at[pl.ds(0, bm)], xbuf.at[src_slot], sem_in.at[src_slot]).wait()

    def scatter():
        def body(r, c):
            d = dst_ref[0, r]
            sh, mk = TOP_K.bit_length() - 1, TOP_K - 1
            pltpu.make_async_copy(ybuf.at[lax.shift_right_logical(r, sh), r & mk],
                                  ys_hbm.at[lax.shift_right_logical(d, sh), d & mk], sem_out).start()
            return c
        lax.fori_loop(0, bm, body, 0, unroll=MOE_DMA_UNROLL)

    def wait_scatter():
        pltpu.make_async_copy(ybuf, ys_hbm.at[pl.ds(0, bm // TOP_K)], sem_out).wait()

    def weight_copies(e, wslot):
        return (pltpu.make_async_copy(w1_hbm.at[e], wf1.at[wslot], sem_w.at[wslot]),
                pltpu.make_async_copy(w3_hbm.at[e], wf3.at[wslot], sem_w.at[wslot]),
                pltpu.make_async_copy(w2_hbm.at[e], wf2.at[wslot], sem_w.at[wslot]))

    @pl.when(i == 0)
    def _prologue():
        for cp in weight_copies(blk_e_ref[0], par_ref[0]):
            cp.start()
        gather(tok_ref, 0)
        ybuf[...] = jnp.zeros(ybuf.shape, ybuf.dtype)
        spare = pltpu.make_async_copy(ybuf, ys_hbm.at[pl.ds(n_slots // TOP_K, bm // TOP_K)], sem_out)
        spare.start()
        spare.wait()

    @pl.when(i + 1 < n_used)
    def _prefetch():
        gather(tokn_ref, 1 - slot)

    @pl.when(i < n_used)
    def _active():
        e = blk_e_ref[i]
        first = jnp.logical_or(i == 0, e != blk_e_ref[jnp.maximum(i - 1, 0)])

        @pl.when(first)
        def _new_expert():
            wslot = par_ref[i]
            for cp in weight_copies(e, wslot):
                cp.wait()
            nxt = nxt_e_ref[i]

            @pl.when(nxt >= 0)
            def _():
                for cp in weight_copies(nxt, 1 - wslot):
                    cp.start()

            w1b[...] = wf1[wslot].astype(BF16)
            w3b[...] = wf3[wslot].astype(BF16)
            w2b[...] = wf2[wslot].astype(BF16)

        wait_gather(slot)
        x = _from_row_tiles(xbuf[slot])
        h1 = jnp.dot(x, w1b[...], preferred_element_type=F32)
        h3 = jnp.dot(x, w3b[...], preferred_element_type=F32)
        hh = (h1 * _sigmoid(h1) * h3).astype(BF16)
        y = jnp.dot(hh, w2b[...], preferred_element_type=F32)

        @pl.when(i > 0)
        def _drain_prev():
            wait_scatter()

        ybuf[...] = _to_row_tiles(y).reshape(ybuf.shape)
        scatter()

        @pl.when(i == n_used - 1)
        def _drain_last():
            wait_scatter()


def _moe_ffn(xn_lines, plan, w1, w3, w2, *, bm, n_slots):
    blk_e, nxt_e, par, n_used, tok_p, dst_p = plan
    E, D, DE = w1.shape
    nl = D // V7X_LANES
    nb = blk_e.shape[0]
    ids = lambda imap: pl.BlockSpec((None, 1, bm), imap, memory_space=pltpu.SMEM)
    grid_spec = pltpu.PrefetchScalarGridSpec(
        num_scalar_prefetch=4,
        grid=(nb,),
        in_specs=[
            ids(lambda i, *_: (i, 0, 0)),
            ids(lambda i, *_: (jnp.minimum(i + 1, nb - 1), 0, 0)),
            ids(lambda i, *_: (i, 0, 0)),
            pl.BlockSpec(memory_space=pl.ANY),
            pl.BlockSpec(memory_space=pl.ANY),
            pl.BlockSpec(memory_space=pl.ANY),
            pl.BlockSpec(memory_space=pl.ANY),
        ],
        out_specs=pl.BlockSpec(memory_space=pl.ANY),
        scratch_shapes=[pltpu.VMEM((2, bm, nl, V7X_LANES), BF16), pltpu.VMEM((bm // TOP_K, TOP_K, nl, V7X_LANES), BF16),
                        pltpu.VMEM((2, D, DE), F32), pltpu.VMEM((2, D, DE), F32), pltpu.VMEM((2, DE, D), F32),
                        pltpu.VMEM((D, DE), BF16), pltpu.VMEM((D, DE), BF16), pltpu.VMEM((DE, D), BF16),
                        pltpu.SemaphoreType.DMA((2,)), pltpu.SemaphoreType.DMA, pltpu.SemaphoreType.DMA((2,))],
    )
    tok3 = tok_p.reshape(nb, 1, bm)
    return pl.pallas_call(
        functools.partial(_moe_kernel, bm=bm, nl=nl, n_slots=n_slots),
        grid_spec=grid_spec,
        out_shape=jax.ShapeDtypeStruct(((n_slots + bm) // TOP_K, TOP_K, nl, V7X_LANES), BF16),
        compiler_params=_cparams(("arbitrary",)),
        name="moe_ffn",
    )(blk_e, nxt_e, par, n_used, tok3, tok3, dst_p.reshape(nb, 1, bm), xn_lines, w1, w3, w2)


def _moe_plan(eid, *, bm):
    T = eid.shape[0]
    M = T * TOP_K
    E = N_EXPERTS
    i32 = jnp.int32
    e_flat = eid.reshape(M)
    order = jnp.argsort(e_flat).astype(i32)
    experts = jnp.arange(E, dtype=i32)
    counts = jnp.sum((e_flat[:, None] == experts[None, :]).astype(i32), axis=0)
    start = jnp.cumsum(counts) - counts
    padded = (counts + bm - 1) // bm * bm
    pend = jnp.cumsum(padded)
    pstart = pend - padded
    nb = M // bm + E
    blk = jnp.arange(nb, dtype=i32)
    blk_start = blk * bm
    n_used = (pend[-1] // bm).astype(i32)
    blk_e = jnp.minimum(jnp.sum((blk_start[:, None] >= pend[None, :]).astype(i32), axis=1), E - 1)
    blk_e = jnp.where(blk < n_used, blk_e, blk_e[jnp.maximum(n_used - 1, 0)])
    off = blk_start - pstart[blk_e]
    base = start[blk_e] + off
    valid = jnp.where(blk < n_used, jnp.clip(counts[blk_e] - off, 0, bm), 0)
    r = jnp.arange(bm, dtype=i32)
    slot_p = order[jnp.clip(base[:, None] + r[None, :], 0, M - 1)]
    tok_p = slot_p // TOP_K
    dst_p = jnp.where(r[None, :] < valid[:, None], slot_p, M + r[None, :])
    seg_first = jnp.concatenate([jnp.ones((1,), bool), blk_e[1:] != blk_e[:-1]])
    par = (jnp.cumsum(seg_first.astype(i32)) - 1) % 2
    later = (experts[None, :] > experts[:, None]) & (counts[None, :] > 0)
    nxt_of = jnp.min(jnp.where(later, experts[None, :], E), axis=1)
    nxt_e = jnp.where(nxt_of == E, -1, nxt_of)[blk_e]
    return blk_e, nxt_e.astype(i32), par.astype(i32), n_used.reshape(1), tok_p, dst_p


def _combine_kernel(h_ref, ys_ref, gate_ref, fw_ref, o_ref, *, tm, nl, eps):
    gate = gate_ref[...]
    acc = h_ref[...]
    for j in range(TOP_K):
        yj = _from_row_tiles(ys_ref[:, j, :, :]).astype(F32)
        acc = acc + yj * gate[:, j:j + 1]
    ms = jnp.mean(acc * acc, axis=-1, keepdims=True)
    o_ref[...] = acc * lax.rsqrt(ms + eps) * fw_ref[...]


def _combine(h, ys, gate, fw, *, tm):
    T, D = h.shape
    nl = D // V7X_LANES
    return pl.pallas_call(
        functools.partial(_combine_kernel, tm=tm, nl=nl, eps=NORM_EPS),
        grid=(T // tm,),
        in_specs=[pl.BlockSpec((tm, D), lambda i: (i, 0)),
                  pl.BlockSpec((tm, TOP_K, nl, V7X_LANES), lambda i: (i, 0, 0, 0)),
                  pl.BlockSpec((tm, V7X_LANES), lambda i: (i, 0)),
                  pl.BlockSpec((1, D), lambda i: (0, 0))],
        out_specs=pl.BlockSpec((tm, D), lambda i: (i, 0)),
        out_shape=jax.ShapeDtypeStruct((T, D), F32),
        compiler_params=_cparams(("arbitrary",)),
        name="combine_norm",
    )(h, ys, gate, fw.reshape(1, D))


def _tiles(T, seq, D):
    pick = lambda n, prefs: next(p for p in prefs if n % p == 0)
    return dict(
        tm_in=pick(seq, (512, 256, 128)),
        tm_rw=pick(seq, (1024, 512, 256, 128)),
        tn_da=1024,
        tq=pick(seq, (1024, 512, 256, 128)),
        tk=pick(seq, (256, 128)),
        rw_L=pick(seq, (256, 128, 64)),
        rw_PW=8,
        tm_out=pick(T, (256, 128)),
        bm=256,
        tm_cmb=pick(T, (256, 128)),
    )


def kernel(x, norm1_w, w_in, lam_q1, lam_k1, lam_q2, lam_k2, subln_w, rw_mu, rw_w0, rw_w_up, rw_a0, rw_a_up, rw_g_up, rw_k_k, rw_k_a, rw_r_k, rw_lnx_w, rw_lnx_b, w_out, norm2_w, router_group_w, router_group_b, router_expert_w, router_expert_b, moe_w1, moe_w3, moe_w2, final_norm_w):
    B, S, D = x.shape
    T = B * S
    depth = w_in.shape[0]
    rw_heads = rw_w0.shape[1] // RW_HEAD_DIM
    rw_cols = rw_mu.shape[1]
    da_cols = w_in.shape[2] - rw_cols
    da_width = da_cols // 3
    da_heads = da_width // (2 * DA_HEAD_DIM)
    tl = _tiles(T, S, D)
    cos, sin = _rope_tables(S)
    q_scale = DA_HEAD_DIM ** -0.5 * math.log2(math.e)
    colscale = jnp.concatenate([jnp.full((da_width,), q_scale, F32), jnp.ones((da_width,), F32)])

    h = x.reshape(T, D)
    for l in range(depth):
        assert l == 0, "lam_init is specialised to the first layer"
        w_qk = (w_in[l][:, :2 * da_width] * colscale)[:, _rope_column_order(2 * da_width)].astype(BF16)
        w_v = w_in[l][:, 2 * da_width:da_cols].T.astype(BF16)
        w_rw = w_in[l][:, da_cols:].astype(BF16)
        zqk = _inproj(h, norm1_w[l], w_qk, seq=S, tm=tl["tm_in"], tn=tl["tn_da"], out_dtype=BF16,
                      mode="rope", cos=cos, sin=sin)
        vt = _inproj(h, norm1_w[l], w_v, seq=S, tm=tl["tm_in"], tn=tl["tn_da"], out_dtype=BF16,
                     mode="transposed", tkv=tl["tk"])
        zrw = _inproj(h, norm1_w[l], w_rw, seq=S, tm=tl["tm_rw"], tn=rw_cols // 2, out_dtype=BF16)
        lam = (jnp.exp(jnp.sum(lam_q1[l] * lam_k1[l])) - jnp.exp(jnp.sum(lam_q2[l] * lam_k2[l])) + LAM_INIT)
        o_da = _diff_attention(zqk, vt, lam.astype(F32), subln_w[l], batch=B, seq=S, n_heads=da_heads,
                               tq=tl["tq"], tk=tl["tk"])
        o_rw = _rwkv7(zrw, rw_mu[l], rw_w0[l], rw_w_up[l], rw_a0[l], rw_a_up[l], rw_g_up[l], rw_k_k[l], rw_k_a[l],
                      rw_r_k[l], rw_lnx_w[l], rw_lnx_b[l], batch=B, seq=S, n_heads=rw_heads,
                      L=tl["rw_L"], PW=tl["rw_PW"])
        h, xn_lines, eid, gate = _outproj_route(o_da, o_rw, w_out[l], h, norm2_w[l], router_group_w[l],
                                                router_group_b[l], router_expert_w[l], router_expert_b[l],
                                                tm=tl["tm_out"])
        plan = _moe_plan(eid[:, :TOP_K], bm=tl["bm"])
        ys = _moe_ffn(xn_lines, plan, moe_w1[l], moe_w3[l], moe_w2[l], bm=tl["bm"], n_slots=T * TOP_K)
        assert depth == 1, "the final norm is fused into the last layer's combine"
        out = _combine(h, ys, gate, final_norm_w, tm=tl["tm_cmb"])
    return out.reshape(B, S, D)
```

```python
import functools
import math

import jax
import jax.numpy as jnp
from jax import lax
from jax.experimental import pallas as pl
from jax.experimental.pallas import tpu as pltpu

F32 = jnp.float32
BF16 = jnp.bfloat16

DA_HEAD_DIM = 64
RW_HEAD_DIM = 64
RW_DECAY_RANK = 64
RW_ICLR_RANK = 64
RW_GATE_RANK = 128
ROPE_THETA = 10000.0
N_GROUPS = 8
EXPERTS_PER_GROUP = 8
N_EXPERTS = N_GROUPS * EXPERTS_PER_GROUP
TOP_K = 2
NORM_EPS = 1e-6
SUBLN_EPS = 1e-5
RW_GN_EPS = 64e-5
LAM_INIT = 0.8 - 0.6 * math.exp(-0.3 * 0)

V7X_LANES = 128
V7X_VMEM_LIMIT = 56 * 1024 * 1024
NEG_BIG = -1e30
ATTN_KV_PER_STEP = 4
ATTN_LOOKAHEAD = 8


def _cparams(sem):
    return pltpu.CompilerParams(dimension_semantics=sem, vmem_limit_bytes=V7X_VMEM_LIMIT)


def _inproj_kernel(*refs, mode, tn, tkv, eps):
    if mode == "rope":
        x_ref, nw_ref, w_ref, cos_ref, sin_ref, o_ref, xn_ref = refs
    else:
        x_ref, nw_ref, w_ref, o_ref, xn_ref = refs
    j = pl.program_id(1)

    @pl.when(j == 0)
    def _norm():
        x = x_ref[...]
        ms = jnp.mean(x * x, axis=-1, keepdims=True)
        xn_ref[...] = (x * lax.rsqrt(ms + eps) * nw_ref[...]).astype(BF16)

    if mode == "transposed":
        acc_t = lax.dot_general(w_ref[...], xn_ref[...], (((1,), (1,)), ((), ())), preferred_element_type=F32)
        for c in range(acc_t.shape[1] // tkv):
            o_ref[c] = acc_t[:, c * tkv:(c + 1) * tkv].astype(o_ref.dtype)
        return

    acc = jnp.dot(xn_ref[...], w_ref[...], preferred_element_type=F32)
    if mode == "rope":
        cos = cos_ref[...]
        sin = sin_ref[...]
        for c in range(tn // V7X_LANES):
            cols = slice(c * V7X_LANES, (c + 1) * V7X_LANES)
            blk = acc[:, cols]
            o_ref[:, cols] = (blk * cos + pltpu.roll(blk, V7X_LANES // 2, 1) * sin).astype(o_ref.dtype)
    else:
        o_ref[...] = acc.astype(o_ref.dtype)


def _inproj(x2, nw, w, *, seq, tm, tn, out_dtype, mode="plain", cos=None, sin=None, tkv=None):
    T, D = x2.shape
    N = w.shape[0] if mode == "transposed" else w.shape[1]
    assert T % tm == 0 and N % tn == 0 and seq % tm == 0
    w_spec = (pl.BlockSpec((tn, D), lambda i, j: (j, 0)) if mode == "transposed"
              else pl.BlockSpec((D, tn), lambda i, j: (0, j)))
    in_specs = [
        pl.BlockSpec((tm, D), lambda i, j: (i, 0)),
        pl.BlockSpec((1, D), lambda i, j: (0, 0)),
        w_spec,
    ]
    args = [x2, nw.reshape(1, D), w]
    if mode == "rope":
        ns = seq // tm
        in_specs += [pl.BlockSpec((tm, V7X_LANES), lambda i, j: (i % ns, 0)),
                     pl.BlockSpec((tm, V7X_LANES), lambda i, j: (i % ns, 0))]
        args += [cos, sin]
    if mode == "transposed":
        assert tm % tkv == 0
        out_specs = pl.BlockSpec((tm // tkv, tn, tkv), lambda i, j: (i, j, 0))
        out_shape = jax.ShapeDtypeStruct((T // tkv, N, tkv), out_dtype)
    else:
        out_specs = pl.BlockSpec((tm, tn), lambda i, j: (i, j))
        out_shape = jax.ShapeDtypeStruct((T, N), out_dtype)
    return pl.pallas_call(
        functools.partial(_inproj_kernel, mode=mode, tn=tn, tkv=tkv, eps=NORM_EPS),
        grid=(T // tm, N // tn),
        in_specs=in_specs,
        out_specs=out_specs,
        out_shape=out_shape,
        scratch_shapes=[pltpu.VMEM((tm, D), BF16)],
        compiler_params=_cparams(("arbitrary", "arbitrary")),
        name="inproj_" + mode,
    )(*args)


def _rope_column_order(w):
    rows, width = w.shape
    half = DA_HEAD_DIM // 2
    w5 = w.reshape(rows, width // V7X_LANES, 2, 2, half)
    return w5.transpose(0, 1, 3, 2, 4).reshape(rows, width)


def _rope_tables(seq):
    half = DA_HEAD_DIM // 2
    inv = ROPE_THETA ** (-jnp.arange(half, dtype=F32) / half)
    ang = jnp.arange(seq, dtype=F32)[:, None] * inv[None, :]
    cos = jnp.cos(ang)
    sin = jnp.sin(ang)
    return jnp.tile(cos, (1, 4)), jnp.concatenate([-sin, -sin, sin, sin], axis=-1)


def _attn_kernel(lam_ref, q_ref, k_ref, vt_ref, sw_ref, o_ref, qs_ref, m_ref, l_ref, acc_ref, *, tq, tk):
    qi = pl.program_id(2)
    d = DA_HEAD_DIM
    q = q_ref[...]
    lane = lax.broadcasted_iota(jnp.int32, q.shape, 1)
    comp1 = (lane % d) < (d // 2)
    zero = jnp.zeros_like(q)
    qs_ref[0:tq, :] = jnp.where(comp1, q, zero)
    qs_ref[tq:2 * tq, :] = jnp.where(comp1, zero, q)
    m_ref[...] = jnp.full(m_ref.shape, NEG_BIG, F32)
    l_ref[...] = jnp.zeros(l_ref.shape, F32)
    acc_ref[...] = jnp.zeros(acc_ref.shape, F32)

    n_diag = tq // tk
    n_full = qi * n_diag

    n_strips = 2 * tq // tk

    def strip_scores(k, si, on_diagonal):
        s = lax.dot_general(k, qs_ref[si * tk:(si + 1) * tk, :], (((1,), (1,)), ((), ())),
                            preferred_element_type=F32)
        if on_diagonal:
            kpos = lax.broadcasted_iota(jnp.int32, s.shape, 0)
            qpos = lax.broadcasted_iota(jnp.int32, s.shape, 1)
            s = jnp.where(kpos <= qpos, s, NEG_BIG)
        return s

    def strip_update(vt, si, s):
        lanes = slice(si * tk, (si + 1) * tk)
        m_old = m_ref[:, lanes]
        m_new = jnp.maximum(m_old, jnp.max(s, axis=0, keepdims=True))
        alpha = jnp.exp2(m_old - m_new)
        p = jnp.exp2(s - m_new)
        l_ref[:, lanes] = alpha * l_ref[:, lanes] + jnp.sum(p, axis=0, keepdims=True)
        acc_ref[:, lanes] = alpha * acc_ref[:, lanes] + jnp.dot(vt, p.astype(BF16), preferred_element_type=F32)
        m_ref[:, lanes] = m_new

    def kv_blocks(blocks):
        kv = [(k_ref[pl.ds(pl.multiple_of(j * tk, tk), tk), :], vt_ref[j]) for j, _ in blocks]
        work = [(b, si, dg) for b, (_, strips) in enumerate(blocks) for si, dg in strips]
        pending = [strip_scores(kv[b][0], si, dg) for b, si, dg in work[:ATTN_LOOKAHEAD]]
        for n, (b, si, _) in enumerate(work):
            if n + ATTN_LOOKAHEAD < len(work):
                nb, nsi, ndg = work[n + ATTN_LOOKAHEAD]
                pending.append(strip_scores(kv[nb][0], nsi, ndg))
            strip_update(kv[b][1], si, pending[n])

    all_strips = [(si, False) for si in range(n_strips)]

    def full_steps(i, c):
        kv_blocks([(i * ATTN_KV_PER_STEP + u, all_strips) for u in range(ATTN_KV_PER_STEP)])
        return c

    lax.fori_loop(0, n_full // ATTN_KV_PER_STEP, full_steps, 0)

    def full_step(j, c):
        kv_blocks([(j, all_strips)])
        return c

    lax.fori_loop(n_full // ATTN_KV_PER_STEP * ATTN_KV_PER_STEP, n_full, full_step, 0)

    kv_blocks([(n_full + c, [(si, si % n_diag == c) for si in range(n_strips) if si % n_diag >= c])
               for c in range(n_diag)])

    o = acc_ref[...] / l_ref[...]
    od = o[:, 0:tq] - lam_ref[0] * o[:, tq:2 * tq]
    ms = jnp.mean(od * od, axis=0, keepdims=True)
    on = (od * lax.rsqrt(ms + SUBLN_EPS) * sw_ref[...]) * (1.0 - LAM_INIT)
    o_ref[...] = on.T.astype(o_ref.dtype)


def _diff_attention(zqk, vt, lam, subln_w, *, batch, seq, n_heads, tq, tk):
    T = zqk.shape[0]
    hw = 2 * DA_HEAD_DIM
    nq = seq // tq
    nk = seq // tk
    sw_b = jnp.broadcast_to(subln_w.reshape(hw, 1), (hw, tq)).astype(F32)
    return pl.pallas_call(
        functools.partial(_attn_kernel, tq=tq, tk=tk),
        grid=(batch, n_heads, nq),
        in_specs=[
            pl.BlockSpec(memory_space=pltpu.SMEM),
            pl.BlockSpec((tq, hw), lambda b, h, i: (b * nq + i, h)),
            pl.BlockSpec((seq, hw), lambda b, h, i: (b, n_heads + h)),
            pl.BlockSpec((nk, hw, tk), lambda b, h, i: (b, h, 0)),
            pl.BlockSpec((hw, tq), lambda b, h, i: (0, 0)),
        ],
        out_specs=pl.BlockSpec((tq, hw), lambda b, h, i: (b * nq + i, h)),
        out_shape=jax.ShapeDtypeStruct((T, n_heads * hw), BF16),
        scratch_shapes=[
            pltpu.VMEM((2 * tq, hw), BF16),
            pltpu.VMEM((1, 2 * tq), F32),
            pltpu.VMEM((1, 2 * tq), F32),
            pltpu.VMEM((hw, 2 * tq), F32),
        ],
        compiler_params=_cparams(("arbitrary", "arbitrary", "arbitrary")),
        name="diff_attn",
    )(lam.reshape(1), zqk, zqk, vt, sw_b)


RW_CHUNK = 64
RW_PAIR = 2 * RW_HEAD_DIM
RW_INV_BLOCK = 16


def _bmm(a, b):
    return jnp.einsum("bij,bjk->bik", a.astype(BF16), b.astype(BF16), preferred_element_type=F32)


def _bmm_nt(a, b):
    return jnp.einsum("bik,bjk->bij", a.astype(BF16), b.astype(BF16), preferred_element_type=F32)


def _bmm_tn(a, b):
    return jnp.einsum("bti,btj->bij", a.astype(BF16), b.astype(BF16), preferred_element_type=F32)


def _mm_split(x, e):
    hi = x.astype(BF16)
    lo = (x - hi.astype(F32)).astype(BF16)
    return (jnp.dot(hi, e, preferred_element_type=F32) + jnp.dot(lo, e, preferred_element_type=F32))


def _sigmoid(x):
    return 1.0 / (1.0 + jnp.exp(-x))


def _softplus(x):
    return jnp.maximum(x, 0.0) + jnp.log(1.0 + jnp.exp(-jnp.abs(x)))


def _unit_lower_inverse(a, eye, diag_blk):
    ad = jnp.where(diag_blk, a, 0.0)
    ao = a - ad
    a2 = _bmm(ad, ad)
    a4 = _bmm(a2, a2)
    a8 = _bmm(a4, a4)
    td = eye + ad
    td = td + _bmm(td, a2)
    td = td + _bmm(td, a4)
    td = td + _bmm(td, a8)
    n1 = _bmm(td, ao)
    n2 = _bmm(n1, n1)
    x = td + _bmm(n2, td)
    return x + _bmm(n1, x)


def _rwkv_kernel(zr_ref, zk_ref, zv_ref, zwa_ref, zg_ref, mur_ref, muk_ref, muv_ref, muwa_ref, mug_ref,
                 vec_ref, ww_ref, wa_ref, gup_ref, o_ref,
                 cr_ref, ck_ref, cv_ref, cwa_ref, cg_ref, state_ref, *, L, PW):
    t = pl.program_id(2)
    C = RW_CHUNK
    PL = RW_PAIR
    NC = L // C

    @pl.when(t == 0)
    def _reset():
        for c_ref in (cr_ref, ck_ref, cv_ref, cwa_ref, cg_ref):
            c_ref[...] = jnp.zeros(c_ref.shape, F32)
        state_ref[...] = jnp.zeros(state_ref.shape, F32)

    def shift_mix(z_ref, mu_ref, c_ref):
        z = z_ref[...].astype(F32)
        row = lax.broadcasted_iota(jnp.int32, z.shape, 0)
        zprev = jnp.where(row == 0, c_ref[0:1, :], pltpu.roll(z, 1, 0))
        c_ref[0:1, :] = z[L - 1:L, :]
        return z + (zprev - z) * mu_ref[...]

    r = shift_mix(zr_ref, mur_ref, cr_ref)
    k = shift_mix(zk_ref, muk_ref, ck_ref)
    v = shift_mix(zv_ref, muv_ref, cv_ref)
    zwa = shift_mix(zwa_ref, muwa_ref, cwa_ref)
    zg = shift_mix(zg_ref, mug_ref, cg_ref)

    ri = lax.broadcasted_iota(jnp.int32, (PL, PL), 0)
    ci = lax.broadcasted_iota(jnp.int32, (PL, PL), 1)
    eye = (ri == ci).astype(F32)
    strict = ci < ri
    incl = ci <= ri
    diag_blk = (ri // RW_INV_BLOCK) == (ci // RW_INV_BLOCK)
    seg_w = 2 * PL if (PW * PL) % (2 * PL) == 0 else PL
    sr = lax.broadcasted_iota(jnp.int32, (seg_w, seg_w), 0)
    sc = lax.broadcasted_iota(jnp.int32, (seg_w, seg_w), 1)
    seg_ones = ((sr // RW_HEAD_DIM) == (sc // RW_HEAD_DIM)).astype(BF16)
    rc = lax.broadcasted_iota(jnp.int32, (C, C), 0)
    cc = lax.broadcasted_iota(jnp.int32, (C, C), 1)
    tri_incl = (cc <= rc).astype(BF16)
    head0 = lax.broadcasted_iota(jnp.int32, (C, PL), 1) < RW_HEAD_DIM
    lanes = [slice(pi * PL, (pi + 1) * PL) for pi in range(PW)]
    rows = [slice(c * C, (c + 1) * C) for c in range(NC)]

    def seg_sum(x):
        return jnp.concatenate([_mm_split(x[:, c * seg_w:(c + 1) * seg_w], seg_ones)
                                for c in range(PW * PL // seg_w)], axis=1)

    w0, a0, k_k, k_a, r_k, lnx_w, lnx_b = (vec_ref[i:i + 1, :] for i in range(7))
    w_pre = w0 + jnp.dot(jnp.tanh(zwa).astype(BF16), ww_ref[...], preferred_element_type=F32)
    logdec = -jnp.exp(-_softplus(-w_pre) - 0.5)
    a = _sigmoid(a0 + jnp.dot(zwa.astype(BF16), wa_ref[...], preferred_element_type=F32))
    g = jnp.dot(_sigmoid(zg).astype(BF16), gup_ref[...], preferred_element_type=F32)
    kk = k * k_k
    kkn = kk / jnp.maximum(jnp.sqrt(seg_sum(kk * kk)), 1e-12)
    kf = k * (1.0 + (a - 1.0) * k_a)
    a_s = -kkn
    b_s = kkn * a
    bonus = seg_sum(r * kf * r_k) * v

    ld_hi = logdec.astype(BF16)
    ld_lo = (logdec - ld_hi.astype(F32)).astype(BF16)
    cum_c = [jnp.dot(tri_incl, ld_hi[rs], preferred_element_type=F32)
             + jnp.dot(tri_incl, ld_lo[rs], preferred_element_type=F32) for rs in rows]
    tot_c = [cu[C - 1:C, :] for cu in cum_c]
    cum = jnp.concatenate(cum_c, axis=0)
    tot = jnp.concatenate([jnp.broadcast_to(tc, (C, tc.shape[1])) for tc in tot_c], axis=0)
    p_inv = jnp.exp(-cum)
    p_end = jnp.exp(tot - cum)

    def stack(x):
        out = []
        for rs in rows:
            for sl in lanes:
                blk = x[rs, sl]
                out.append(jnp.concatenate([jnp.where(head0, blk, 0.0), jnp.where(head0, 0.0, blk)], axis=0))
        return jnp.stack(out, axis=0).astype(BF16)

    rt_s = stack(r * jnp.exp(cum))
    at_s = stack(a_s * jnp.exp(cum - logdec))
    kt_s = stack(kf * p_inv)
    bt_s = stack(b_s * p_inv)
    kh_s = stack(kf * p_end)
    bh_s = stack(b_s * p_end)
    v_s = stack(v)

    m1 = _bmm_nt(jnp.concatenate([at_s, rt_s], axis=1), jnp.concatenate([bt_s, kt_s], axis=1))
    a_ab = jnp.where(strict, m1[:, 0:PL, 0:PL], 0.0)
    a_ak = jnp.where(strict, m1[:, 0:PL, PL:2 * PL], 0.0)
    a_rb = jnp.where(incl, m1[:, PL:2 * PL, 0:PL], 0.0)
    a_rk = jnp.where(incl, m1[:, PL:2 * PL, PL:2 * PL], 0.0)
    tinv = _unit_lower_inverse(a_ab, eye, diag_blk)
    wu = _bmm(tinv, jnp.concatenate([at_s, _bmm(a_ak, v_s).astype(BF16)], axis=2))

    s = state_ref[...]
    us, s0s = [], []
    for c in range(NC):
        sel = slice(c * PW, (c + 1) * PW)
        sb = s.astype(BF16)
        u = _bmm_nt(wu[sel, :, 0:PL], sb) + wu[sel, :, PL:2 * PL]
        us.append(u)
        s0s.append(sb)
        p_tot = jnp.stack([jnp.exp(tot_c[c][:, sl]) for sl in lanes], axis=0)
        s = s * p_tot + _bmm_tn(jnp.concatenate([u.astype(BF16), v_s[sel]], axis=1),
                                jnp.concatenate([bh_s[sel], kh_s[sel]], axis=1))
    state_ref[...] = s

    u_all = jnp.concatenate(us, axis=0)
    s0_all = jnp.concatenate(s0s, axis=0)
    y2 = _bmm_nt(rt_s, s0_all) + _bmm(a_rb, u_all) + _bmm(a_rk, v_s)
    y2 = y2[:, 0:C, :] + y2[:, C:2 * C, :]
    y = jnp.concatenate([jnp.concatenate([y2[c * PW + pi] for pi in range(PW)], axis=1) for c in range(NC)], axis=0)

    inv_n = 1.0 / RW_HEAD_DIM
    mean = seg_sum(y) * inv_n
    yc = y - mean
    var = seg_sum(yc * yc) * inv_n
    yn = yc * lax.rsqrt(var + RW_GN_EPS) * lnx_w + lnx_b
    o_ref[...] = ((yn + bonus) * g).astype(o_ref.dtype)


def _rwkv7(zrw, mu, w0, w_up, a0, a_up, g_up, k_k, k_a, r_k, lnx_w, lnx_b, *, batch, seq, n_heads, L, PW):
    T = zrw.shape[0]
    HC = n_heads * RW_HEAD_DIM
    PWL = PW * RW_PAIR
    assert HC % PWL == 0 and seq % L == 0 and L % RW_CHUNK == 0
    ng = HC // PWL
    nt = seq // L
    lora_w = RW_DECAY_RANK + RW_ICLR_RANK
    assert lora_w == V7X_LANES and RW_GATE_RANK == V7X_LANES
    vecs = jnp.stack([w0, a0, k_k, k_a, r_k.reshape(HC), lnx_w, lnx_b, jnp.zeros((HC,), F32)], axis=0)
    ww = jnp.concatenate([w_up, jnp.zeros((RW_ICLR_RANK, HC), F32)], axis=0).astype(BF16)
    wa = jnp.concatenate([jnp.zeros((RW_DECAY_RANK, HC), F32), a_up], axis=0).astype(BF16)
    gw = g_up.astype(BF16)
    mu2 = mu.reshape(1, -1)
    cb = HC // PWL
    wa_blk = 3 * HC // V7X_LANES
    zrow = lambda b, g, t: b * nt + t
    big = lambda sec: pl.BlockSpec((L, PWL), lambda b, g, t: (zrow(b, g, t), sec * cb + g))
    small = lambda off: pl.BlockSpec((L, V7X_LANES), lambda b, g, t: (zrow(b, g, t), wa_blk + off))
    mu_big = lambda sec: pl.BlockSpec((1, PWL), lambda b, g, t: (0, sec * cb + g))
    mu_small = lambda off: pl.BlockSpec((1, V7X_LANES), lambda b, g, t: (0, wa_blk + off))
    wspec = pl.BlockSpec((V7X_LANES, PWL), lambda b, g, t: (0, g))
    return pl.pallas_call(
        functools.partial(_rwkv_kernel, L=L, PW=PW),
        grid=(batch, ng, nt),
        in_specs=[big(0), big(1), big(2), small(0), small(1),
                  mu_big(0), mu_big(1), mu_big(2), mu_small(0), mu_small(1),
                  pl.BlockSpec((8, PWL), lambda b, g, t: (0, g)),
                  wspec, wspec, wspec],
        out_specs=pl.BlockSpec((L, PWL), lambda b, g, t: (zrow(b, g, t), g)),
        out_shape=jax.ShapeDtypeStruct((T, HC), BF16),
        scratch_shapes=[pltpu.VMEM((8, PWL), F32), pltpu.VMEM((8, PWL), F32), pltpu.VMEM((8, PWL), F32),
                        pltpu.VMEM((8, V7X_LANES), F32), pltpu.VMEM((8, V7X_LANES), F32),
                        pltpu.VMEM((PW, RW_PAIR, RW_PAIR), F32)],
        compiler_params=_cparams(("arbitrary", "arbitrary", "arbitrary")),
        name="rwkv7_chunked",
    )(zrw, zrw, zrw, zrw, zrw, mu2, mu2, mu2, mu2, mu2, vecs, ww, wa, gw)


def _route(lg):
    lane_i = lax.broadcasted_iota(jnp.int32, lg.shape, 1)
    lane = lane_i.astype(F32)
    is_g = lane_i < N_GROUPS
    mg = jnp.max(jnp.where(is_g, lg, NEG_BIG), axis=-1, keepdims=True)
    eg = jnp.exp(jnp.where(is_g, lg - mg, NEG_BIG))
    pg = eg / jnp.sum(eg, axis=-1, keepdims=True)
    p_g = jnp.max(pg, axis=-1, keepdims=True)
    g_sel = jnp.min(jnp.where(is_g & (pg == p_g), lane, 1e9), axis=-1, keepdims=True)
    grp = (jnp.right_shift(lane_i, 3) - 1).astype(F32)
    is_e = (lane_i >= N_GROUPS) & (lane_i < N_GROUPS + N_EXPERTS) & (grp == g_sel)
    me = jnp.max(jnp.where(is_e, lg, NEG_BIG), axis=-1, keepdims=True)
    ee = jnp.exp(jnp.where(is_e, lg - me, NEG_BIG))
    pe = ee / jnp.sum(ee, axis=-1, keepdims=True)
    p1 = jnp.max(jnp.where(is_e, pe, -1.0), axis=-1, keepdims=True)
    i1 = jnp.min(jnp.where(is_e & (pe == p1), lane, 1e9), axis=-1, keepdims=True)
    rest = is_e & (lane != i1)
    p2 = jnp.max(jnp.where(rest, pe, -1.0), axis=-1, keepdims=True)
    i2 = jnp.min(jnp.where(rest & (pe == p2), lane, 1e9), axis=-1, keepdims=True)
    den = p1 + p2
    eid = jnp.where(lane_i == 0, i1 - N_GROUPS, jnp.where(lane_i == 1, i2 - N_GROUPS, 0.0)).astype(jnp.int32)
    gate = jnp.where(lane_i == 0, p_g * p1 / den, jnp.where(lane_i == 1, p_g * p2 / den, 0.0))
    return eid, gate


def _to_row_tiles(x):
    return x.astype(BF16).reshape(x.shape[0], x.shape[1] // V7X_LANES, V7X_LANES)


def _from_row_tiles(x3):
    return x3.reshape(x3.shape[0], x3.shape[1] * x3.shape[2])


def _outproj_kernel(oda_ref, orw_ref, wt_ref, wb_ref, x_ref, n2_ref, rcat_ref, rb_ref,
                    h_ref, xn_ref, eid_ref, gate_ref, *, tm, eps):
    acc = (jnp.dot(oda_ref[...], wt_ref[...], preferred_element_type=F32)
           + jnp.dot(orw_ref[...], wb_ref[...], preferred_element_type=F32))
    h = x_ref[...] + acc
    h_ref[...] = h
    ms = jnp.mean(h * h, axis=-1, keepdims=True)
    xn = h * lax.rsqrt(ms + eps) * n2_ref[...]
    xn_ref[...] = _to_row_tiles(xn)
    hw = jnp.dot(xn.astype(BF16), rcat_ref[...], preferred_element_type=F32)
    lg = hw[:, 0:V7X_LANES] + hw[:, V7X_LANES:2 * V7X_LANES] + rb_ref[...]
    eid, gate = _route(lg)
    eid_ref[...] = eid
    gate_ref[...] = gate


def _outproj_route(oda, orw, w_out, x2, n2, wg, bg, we, be, *, tm):
    T, D = x2.shape
    w1 = oda.shape[1]
    w2 = orw.shape[1]
    nl = D // V7X_LANES
    wt = w_out[:w1].astype(BF16)
    wb = w_out[w1:].astype(BF16)
    pad = V7X_LANES - N_GROUPS - N_EXPERTS
    wr = jnp.concatenate([wg, we, jnp.zeros((D, pad), F32)], axis=1)
    rb = jnp.concatenate([bg, be, jnp.zeros((pad,), F32)]).reshape(1, V7X_LANES)
    rhi = wr.astype(BF16)
    rlo = (wr - rhi.astype(F32)).astype(BF16)
    rcat = jnp.concatenate([rhi, rlo], axis=1)
    const = lambda shape: pl.BlockSpec(shape, lambda i: (0, 0))
    return pl.pallas_call(
        functools.partial(_outproj_kernel, tm=tm, eps=NORM_EPS),
        grid=(T // tm,),
        in_specs=[pl.BlockSpec((tm, w1), lambda i: (i, 0)), pl.BlockSpec((tm, w2), lambda i: (i, 0)),
                  const((w1, D)), const((w2, D)), pl.BlockSpec((tm, D), lambda i: (i, 0)), const((1, D)),
                  const((D, 2 * V7X_LANES)), const((1, V7X_LANES))],
        out_specs=[pl.BlockSpec((tm, D), lambda i: (i, 0)),
                   pl.BlockSpec((tm, nl, V7X_LANES), lambda i: (i, 0, 0)),
                   pl.BlockSpec((tm, V7X_LANES), lambda i: (i, 0)),
                   pl.BlockSpec((tm, V7X_LANES), lambda i: (i, 0))],
        out_shape=[jax.ShapeDtypeStruct((T, D), F32),
                   jax.ShapeDtypeStruct((T, nl, V7X_LANES), BF16),
                   jax.ShapeDtypeStruct((T, V7X_LANES), jnp.int32),
                   jax.ShapeDtypeStruct((T, V7X_LANES), F32)],
        compiler_params=_cparams(("arbitrary",)),
        name="outproj_route",
    )(oda, orw, wt, wb, x2, n2.reshape(1, D), rcat, rb)


MOE_DMA_UNROLL = 8


def _moe_kernel(blk_e_ref, nxt_e_ref, par_ref, nused_ref, tok_ref, tokn_ref, dst_ref,
                xn_hbm, w1_hbm, w3_hbm, w2_hbm, ys_hbm,
                xbuf, ybuf, wf1, wf3, wf2, w1b, w3b, w2b, sem_in, sem_out, sem_w, *, bm, nl, n_slots):
    i = pl.program_id(0)
    n_used = nused_ref[0]
    slot = i % 2

    def gather(ids_ref, dst_slot):
        def body(r, c):
            pltpu.make_async_copy(xn_hbm.at[ids_ref[0, r]], xbuf.at[dst_slot, r], sem_in.at[dst_slot]).start()
            return c
        lax.fori_loop(0, bm, body, 0, unroll=MOE_DMA_UNROLL)

    def wait_gather(src_slot):
        pltpu.make_async_copy(xn_hbm.at[pl.ds(0, bm)], xbuf.at[src_slot], sem_in.at[src_slot]).wait()

    def scatter():
        def body(r, c):
            pltpu.make_async_copy(ybuf.at[r], ys_hbm.at[dst_ref[0, r]], sem_out).start()
            return c
        lax.fori_loop(0, bm, body, 0, unroll=MOE_DMA_UNROLL)

    def wait_scatter():
        pltpu.make_async_copy(ybuf, ys_hbm.at[pl.ds(0, bm)], sem_out).wait()

    def weight_copies(e, wslot):
        return (pltpu.make_async_copy(w1_hbm.at[e], wf1.at[wslot], sem_w.at[wslot]),
                pltpu.make_async_copy(w3_hbm.at[e], wf3.at[wslot], sem_w.at[wslot]),
                pltpu.make_async_copy(w2_hbm.at[e], wf2.at[wslot], sem_w.at[wslot]))

    @pl.when(i == 0)
    def _prologue():
        for cp in weight_copies(blk_e_ref[0], par_ref[0]):
            cp.start()
        gather(tok_ref, 0)
        ybuf[...] = jnp.zeros(ybuf.shape, ybuf.dtype)
        spare = pltpu.make_async_copy(ybuf, ys_hbm.at[pl.ds(n_slots, bm)], sem_out)
        spare.start()
        spare.wait()

    @pl.when(i + 1 < n_used)
    def _prefetch():
        gather(tokn_ref, 1 - slot)

    @pl.when(i < n_used)
    def _active():
        e = blk_e_ref[i]
        first = jnp.logical_or(i == 0, e != blk_e_ref[jnp.maximum(i - 1, 0)])

        @pl.when(first)
        def _new_expert():
            wslot = par_ref[i]
            for cp in weight_copies(e, wslot):
                cp.wait()
            nxt = nxt_e_ref[i]

            @pl.when(nxt >= 0)
            def _():
                for cp in weight_copies(nxt, 1 - wslot):
                    cp.start()

            w1b[...] = wf1[wslot].astype(BF16)
            w3b[...] = wf3[wslot].astype(BF16)
            w2b[...] = wf2[wslot].astype(BF16)

        wait_gather(slot)
        x = _from_row_tiles(xbuf[slot])
        h1 = jnp.dot(x, w1b[...], preferred_element_type=F32)
        h3 = jnp.dot(x, w3b[...], preferred_element_type=F32)
        hh = (h1 * _sigmoid(h1) * h3).astype(BF16)
        y = jnp.dot(hh, w2b[...], preferred_element_type=F32)

        @pl.when(i > 0)
        def _drain_prev():
            wait_scatter()

        ybuf[...] = _to_row_tiles(y)
        scatter()

        @pl.when(i == n_used - 1)
        def _drain_last():
            wait_scatter()


def _moe_ffn(xn_lines, plan, w1, w3, w2, *, bm, n_slots):
    blk_e, nxt_e, par, n_used, tok_p, dst_p = plan
    E, D, DE = w1.shape
    nl = D // V7X_LANES
    nb = blk_e.shape[0]
    ids = lambda imap: pl.BlockSpec((None, 1, bm), imap, memory_space=pltpu.SMEM)
    grid_spec = pltpu.PrefetchScalarGridSpec(
        num_scalar_prefetch=4,
        grid=(nb,),
        in_specs=[
            ids(lambda i, *_: (i, 0, 0)),
            ids(lambda i, *_: (jnp.minimum(i + 1, nb - 1), 0, 0)),
            ids(lambda i, *_: (i, 0, 0)),
            pl.BlockSpec(memory_space=pl.ANY),
            pl.BlockSpec(memory_space=pl.ANY),
            pl.BlockSpec(memory_space=pl.ANY),
            pl.BlockSpec(memory_space=pl.ANY),
        ],
        out_specs=pl.BlockSpec(memory_space=pl.ANY),
        scratch_shapes=[pltpu.VMEM((2, bm, nl, V7X_LANES), BF16), pltpu.VMEM((bm, nl, V7X_LANES), BF16),
                        pltpu.VMEM((2, D, DE), F32), pltpu.VMEM((2, D, DE), F32), pltpu.VMEM((2, DE, D), F32),
                        pltpu.VMEM((D, DE), BF16), pltpu.VMEM((D, DE), BF16), pltpu.VMEM((DE, D), BF16),
                        pltpu.SemaphoreType.DMA((2,)), pltpu.SemaphoreType.DMA, pltpu.SemaphoreType.DMA((2,))],
    )
    tok3 = tok_p.reshape(nb, 1, bm)
    ys = pl.pallas_call(
        functools.partial(_moe_kernel, bm=bm, nl=nl, n_slots=n_slots),
        grid_spec=grid_spec,
        out_shape=jax.ShapeDtypeStruct((n_slots + bm, nl, V7X_LANES), BF16),
        compiler_params=_cparams(("arbitrary",)),
        name="moe_ffn",
    )(blk_e, nxt_e, par, n_used, tok3, tok3, dst_p.reshape(nb, 1, bm), xn_lines, w1, w3, w2)
    return ys.reshape((n_slots + bm) // TOP_K, TOP_K, nl, V7X_LANES)


def _moe_plan(eid, *, bm):
    T = eid.shape[0]
    M = T * TOP_K
    E = N_EXPERTS
    i32 = jnp.int32
    e_flat = eid.reshape(M)
    order = jnp.argsort(e_flat).astype(i32)
    experts = jnp.arange(E, dtype=i32)
    counts = jnp.sum((e_flat[:, None] == experts[None, :]).astype(i32), axis=0)
    start = jnp.cumsum(counts) - counts
    padded = (counts + bm - 1) // bm * bm
    pend = jnp.cumsum(padded)
    pstart = pend - padded
    nb = M // bm + E
    blk = jnp.arange(nb, dtype=i32)
    blk_start = blk * bm
    n_used = (pend[-1] // bm).astype(i32)
    blk_e = jnp.minimum(jnp.sum((blk_start[:, None] >= pend[None, :]).astype(i32), axis=1), E - 1)
    blk_e = jnp.where(blk < n_used, blk_e, blk_e[jnp.maximum(n_used - 1, 0)])
    off = blk_start - pstart[blk_e]
    base = start[blk_e] + off
    valid = jnp.where(blk < n_used, jnp.clip(counts[blk_e] - off, 0, bm), 0)
    r = jnp.arange(bm, dtype=i32)
    slot_p = order[jnp.clip(base[:, None] + r[None, :], 0, M - 1)]
    tok_p = slot_p // TOP_K
    dst_p = jnp.where(r[None, :] < valid[:, None], slot_p, M + r[None, :])
    seg_first = jnp.concatenate([jnp.ones((1,), bool), blk_e[1:] != blk_e[:-1]])
    par = (jnp.cumsum(seg_first.astype(i32)) - 1) % 2
    later = (experts[None, :] > experts[:, None]) & (counts[None, :] > 0)
    nxt_of = jnp.min(jnp.where(later, experts[None, :], E), axis=1)
    nxt_e = jnp.where(nxt_of == E, -1, nxt_of)[blk_e]
    return blk_e, nxt_e.astype(i32), par.astype(i32), n_used.reshape(1), tok_p, dst_p


def _combine_kernel(h_ref, ys_ref, gate_ref, fw_ref, o_ref, *, tm, nl, eps):
    gate = gate_ref[...]
    acc = h_ref[...]
    for j in range(TOP_K):
        yj = _from_row_tiles(ys_ref[:, j, :, :]).astype(F32)
        acc = acc + yj * gate[:, j:j + 1]
    ms = jnp.mean(acc * acc, axis=-1, keepdims=True)
    o_ref[...] = acc * lax.rsqrt(ms + eps) * fw_ref[...]


def _combine(h, ys, gate, fw, *, tm):
    T, D = h.shape
    nl = D // V7X_LANES
    return pl.pallas_call(
        functools.partial(_combine_kernel, tm=tm, nl=nl, eps=NORM_EPS),
        grid=(T // tm,),
        in_specs=[pl.BlockSpec((tm, D), lambda i: (i, 0)),
                  pl.BlockSpec((tm, TOP_K, nl, V7X_LANES), lambda i: (i, 0, 0, 0)),
                  pl.BlockSpec((tm, V7X_LANES), lambda i: (i, 0)),
                  pl.BlockSpec((1, D), lambda i: (0, 0))],
        out_specs=pl.BlockSpec((tm, D), lambda i: (i, 0)),
        out_shape=jax.ShapeDtypeStruct((T, D), F32),
        compiler_params=_cparams(("arbitrary",)),
        name="combine_norm",
    )(h, ys, gate, fw.reshape(1, D))


def _tiles(T, seq, D):
    pick = lambda n, prefs: next(p for p in prefs if n % p == 0)
    return dict(
        tm_in=pick(seq, (512, 256, 128)),
        tm_rw=pick(seq, (1024, 512, 256, 128)),
        tn_da=1024,
        tq=pick(seq, (1024, 512, 256, 128)),
        tk=pick(seq, (256, 128)),
        rw_L=pick(seq, (256, 128, 64)),
        rw_PW=8,
        tm_out=pick(T, (256, 128)),
        bm=256,
        tm_cmb=pick(T, (256, 128)),
    )


def kernel(x, norm1_w, w_in, lam_q1, lam_k1, lam_q2, lam_k2, subln_w, rw_mu, rw_w0, rw_w_up, rw_a0, rw_a_up, rw_g_up, rw_k_k, rw_k_a, rw_r_k, rw_lnx_w, rw_lnx_b, w_out, norm2_w, router_group_w, router_group_b, router_expert_w, router_expert_b, moe_w1, moe_w3, moe_w2, final_norm_w):
    B, S, D = x.shape
    T = B * S
    depth = w_in.shape[0]
    rw_heads = rw_w0.shape[1] // RW_HEAD_DIM
    rw_cols = rw_mu.shape[1]
    da_cols = w_in.shape[2] - rw_cols
    da_width = da_cols // 3
    da_heads = da_width // (2 * DA_HEAD_DIM)
    tl = _tiles(T, S, D)
    cos, sin = _rope_tables(S)
    q_scale = DA_HEAD_DIM ** -0.5 * math.log2(math.e)
    colscale = jnp.concatenate([jnp.full((da_width,), q_scale, F32), jnp.ones((da_width,), F32)])

    h = x.reshape(T, D)
    for l in range(depth):
        assert l == 0, "lam_init is specialised to the first layer"
        w_qk = _rope_column_order(w_in[l][:, :2 * da_width] * colscale).astype(BF16)
        w_v = w_in[l][:, 2 * da_width:da_cols].T.astype(BF16)
        w_rw = w_in[l][:, da_cols:].astype(BF16)
        zqk = _inproj(h, norm1_w[l], w_qk, seq=S, tm=tl["tm_in"], tn=tl["tn_da"], out_dtype=BF16,
                      mode="rope", cos=cos, sin=sin)
        vt = _inproj(h, norm1_w[l], w_v, seq=S, tm=tl["tm_in"], tn=tl["tn_da"], out_dtype=BF16,
                     mode="transposed", tkv=tl["tk"])
        zrw = _inproj(h, norm1_w[l], w_rw, seq=S, tm=tl["tm_rw"], tn=rw_cols // 2, out_dtype=BF16)
        lam = (jnp.exp(jnp.sum(lam_q1[l] * lam_k1[l])) - jnp.exp(jnp.sum(lam_q2[l] * lam_k2[l])) + LAM_INIT)
        o_da = _diff_attention(zqk, vt, lam.astype(F32), subln_w[l], batch=B, seq=S, n_heads=da_heads,
                               tq=tl["tq"], tk=tl["tk"])
        o_rw = _rwkv7(zrw, rw_mu[l], rw_w0[l], rw_w_up[l], rw_a0[l], rw_a_up[l], rw_g_up[l], rw_k_k[l], rw_k_a[l],
                      rw_r_k[l], rw_lnx_w[l], rw_lnx_b[l], batch=B, seq=S, n_heads=rw_heads,
                      L=tl["rw_L"], PW=tl["rw_PW"])
        h, xn_lines, eid, gate = _outproj_route(o_da, o_rw, w_out[l], h, norm2_w[l], router_group_w[l],
                                                router_group_b[l], router_expert_w[l], router_expert_b[l],
                                                tm=tl["tm_out"])
        plan = _moe_plan(eid[:, :TOP_K], bm=tl["bm"])
        ys = _moe_ffn(xn_lines, plan, moe_w1[l], moe_w3[l], moe_w2[l], bm=tl["bm"], n_slots=T * TOP_K)
        assert depth == 1, "the final norm is fused into the last layer's combine"
        out = _combine(h, ys, gate, final_norm_w, tm=tl["tm_cmb"])
    return out.reshape(B, S, D)
```

```python
import functools
import math

import jax
import jax.numpy as jnp
from jax import lax
from jax.experimental import pallas as pl
from jax.experimental.pallas import tpu as pltpu

F32 = jnp.float32
BF16 = jnp.bfloat16

DA_HEAD_DIM = 64
RW_HEAD_DIM = 64
RW_DECAY_RANK = 64
RW_ICLR_RANK = 64
RW_GATE_RANK = 128
ROPE_THETA = 10000.0
N_GROUPS = 8
EXPERTS_PER_GROUP = 8
N_EXPERTS = N_GROUPS * EXPERTS_PER_GROUP
TOP_K = 2
NORM_EPS = 1e-6
SUBLN_EPS = 1e-5
RW_GN_EPS = 64e-5
LAM_INIT = 0.8 - 0.6 * math.exp(-0.3 * 0)

V7X_LANES = 128
V7X_VMEM_LIMIT = 56 * 1024 * 1024
NEG_BIG = -1e30
ATTN_KV_PER_STEP = 4
ATTN_LOOKAHEAD = 8


def _cparams(sem):
    return pltpu.CompilerParams(dimension_semantics=sem, vmem_limit_bytes=V7X_VMEM_LIMIT)


def _inproj_kernel(*refs, mode, tn, tkv, eps, prenormed, emit_xn):
    refs = list(refs)
    x_ref = refs.pop(0)
    nw_ref = None if prenormed else refs.pop(0)
    w_ref = refs.pop(0)
    cos_ref, sin_ref = (refs.pop(0), refs.pop(0)) if mode == "rope" else (None, None)
    o_ref = refs.pop(0)
    xno_ref = refs.pop(0) if emit_xn else None
    xn_ref = x_ref if prenormed else refs.pop(0)

    if not prenormed:
        @pl.when(pl.program_id(1) == 0)
        def _norm():
            x = x_ref[...]
            ms = jnp.mean(x * x, axis=-1, keepdims=True)
            xn_ref[...] = (x * lax.rsqrt(ms + eps) * nw_ref[...]).astype(BF16)
            if emit_xn:
                xno_ref[...] = xn_ref[...]

    if mode == "transposed":
        acc_t = lax.dot_general(w_ref[...], xn_ref[...], (((1,), (1,)), ((), ())), preferred_element_type=F32)
        for c in range(acc_t.shape[1] // tkv):
            o_ref[c] = acc_t[:, c * tkv:(c + 1) * tkv].astype(o_ref.dtype)
        return

    acc = jnp.dot(xn_ref[...], w_ref[...], preferred_element_type=F32)
    if mode == "rope":
        cos = cos_ref[...]
        sin = sin_ref[...]
        for c in range(tn // V7X_LANES):
            cols = slice(c * V7X_LANES, (c + 1) * V7X_LANES)
            blk = acc[:, cols]
            o_ref[:, cols] = (blk * cos + pltpu.roll(blk, V7X_LANES // 2, 1) * sin).astype(o_ref.dtype)
    else:
        o_ref[...] = acc.astype(o_ref.dtype)


def _inproj(x2, nw, w, *, seq, tm, tn, out_dtype, mode="plain", cos=None, sin=None, tkv=None,
            emit_xn=False, w_outer=False):
    T, D = x2.shape
    prenormed = nw is None
    assert not (w_outer and not prenormed) and not (emit_xn and prenormed)
    N = w.shape[0] if mode == "transposed" else w.shape[1]
    assert T % tm == 0 and N % tn == 0 and seq % tm == 0
    rc = (lambda a, b: (b, a)) if w_outer else (lambda a, b: (a, b))
    im = lambda f: (lambda a, b: f(*rc(a, b)))
    w_spec = (pl.BlockSpec((tn, D), im(lambda i, j: (j, 0))) if mode == "transposed"
              else pl.BlockSpec((D, tn), im(lambda i, j: (0, j))))
    in_specs = [pl.BlockSpec((tm, D), im(lambda i, j: (i, 0)))]
    args = [x2]
    if not prenormed:
        in_specs.append(pl.BlockSpec((1, D), im(lambda i, j: (0, 0))))
        args.append(nw.reshape(1, D))
    in_specs.append(w_spec)
    args.append(w)
    if mode == "rope":
        ns = seq // tm
        in_specs += [pl.BlockSpec((tm, V7X_LANES), im(lambda i, j: (i % ns, 0))),
                     pl.BlockSpec((tm, V7X_LANES), im(lambda i, j: (i % ns, 0)))]
        args += [cos, sin]
    if mode == "transposed":
        assert tm % tkv == 0
        out_specs = [pl.BlockSpec((tm // tkv, tn, tkv), im(lambda i, j: (i, j, 0)))]
        out_shape = [jax.ShapeDtypeStruct((T // tkv, N, tkv), out_dtype)]
    else:
        out_specs = [pl.BlockSpec((tm, tn), im(lambda i, j: (i, j)))]
        out_shape = [jax.ShapeDtypeStruct((T, N), out_dtype)]
    if emit_xn:
        out_specs.append(pl.BlockSpec((tm, D), im(lambda i, j: (i, 0))))
        out_shape.append(jax.ShapeDtypeStruct((T, D), BF16))
    grid = (N // tn, T // tm) if w_outer else (T // tm, N // tn)
    outs = pl.pallas_call(
        functools.partial(_inproj_kernel, mode=mode, tn=tn, tkv=tkv, eps=NORM_EPS,
                          prenormed=prenormed, emit_xn=emit_xn),
        grid=grid,
        in_specs=in_specs,
        out_specs=out_specs,
        out_shape=out_shape,
        scratch_shapes=[] if prenormed else [pltpu.VMEM((tm, D), BF16)],
        compiler_params=_cparams(("arbitrary", "arbitrary")),
        name="inproj_" + mode,
    )(*args)
    return outs if emit_xn else outs[0]


def _rope_column_order(w):
    rows, width = w.shape
    half = DA_HEAD_DIM // 2
    w5 = w.reshape(rows, width // V7X_LANES, 2, 2, half)
    return w5.transpose(0, 1, 3, 2, 4).reshape(rows, width)


def _rope_tables(seq):
    half = DA_HEAD_DIM // 2
    inv = ROPE_THETA ** (-jnp.arange(half, dtype=F32) / half)
    ang = jnp.arange(seq, dtype=F32)[:, None] * inv[None, :]
    cos = jnp.cos(ang)
    sin = jnp.sin(ang)
    return jnp.tile(cos, (1, 4)), jnp.concatenate([-sin, -sin, sin, sin], axis=-1)


def _attn_kernel(lam_ref, q_ref, k_ref, vt_ref, sw_ref, o_ref, qs_ref, m_ref, l_ref, acc_ref, *, tq, tk):
    qi = pl.program_id(2)
    d = DA_HEAD_DIM
    q = q_ref[...]
    lane = lax.broadcasted_iota(jnp.int32, q.shape, 1)
    comp1 = (lane % d) < (d // 2)
    zero = jnp.zeros_like(q)
    qs_ref[0:tq, :] = jnp.where(comp1, q, zero)
    qs_ref[tq:2 * tq, :] = jnp.where(comp1, zero, q)
    m_ref[...] = jnp.full(m_ref.shape, NEG_BIG, F32)
    l_ref[...] = jnp.zeros(l_ref.shape, F32)
    acc_ref[...] = jnp.zeros(acc_ref.shape, F32)

    n_diag = tq // tk
    n_full = qi * n_diag

    n_strips = 2 * tq // tk

    def strip_scores(k, si, on_diagonal):
        s = lax.dot_general(k, qs_ref[si * tk:(si + 1) * tk, :], (((1,), (1,)), ((), ())),
                            preferred_element_type=F32)
        if on_diagonal:
            kpos = lax.broadcasted_iota(jnp.int32, s.shape, 0)
            qpos = lax.broadcasted_iota(jnp.int32, s.shape, 1)
            s = jnp.where(kpos <= qpos, s, NEG_BIG)
        return s

    def strip_update(vt, si, s):
        lanes = slice(si * tk, (si + 1) * tk)
        m_old = m_ref[:, lanes]
        m_new = jnp.maximum(m_old, jnp.max(s, axis=0, keepdims=True))
        alpha = jnp.exp2(m_old - m_new)
        p = jnp.exp2(s - m_new)
        l_ref[:, lanes] = alpha * l_ref[:, lanes] + jnp.sum(p, axis=0, keepdims=True)
        acc_ref[:, lanes] = alpha * acc_ref[:, lanes] + jnp.dot(vt, p.astype(BF16), preferred_element_type=F32)
        m_ref[:, lanes] = m_new

    def kv_blocks(blocks):
        kv = [(k_ref[pl.ds(pl.multiple_of(j * tk, tk), tk), :], vt_ref[j]) for j, _ in blocks]
        work = [(b, si, dg) for b, (_, strips) in enumerate(blocks) for si, dg in strips]
        pending = [strip_scores(kv[b][0], si, dg) for b, si, dg in work[:ATTN_LOOKAHEAD]]
        for n, (b, si, _) in enumerate(work):
            if n + ATTN_LOOKAHEAD < len(work):
                nb, nsi, ndg = work[n + ATTN_LOOKAHEAD]
                pending.append(strip_scores(kv[nb][0], nsi, ndg))
            strip_update(kv[b][1], si, pending[n])

    all_strips = [(si, False) for si in range(n_strips)]

    def full_steps(i, c):
        kv_blocks([(i * ATTN_KV_PER_STEP + u, all_strips) for u in range(ATTN_KV_PER_STEP)])
        return c

    lax.fori_loop(0, n_full // ATTN_KV_PER_STEP, full_steps, 0)

    def full_step(j, c):
        kv_blocks([(j, all_strips)])
        return c

    lax.fori_loop(n_full // ATTN_KV_PER_STEP * ATTN_KV_PER_STEP, n_full, full_step, 0)

    kv_blocks([(n_full + c, [(si, si % n_diag == c) for si in range(n_strips) if si % n_diag >= c])
               for c in range(n_diag)])

    o = acc_ref[...] / l_ref[...]
    od = o[:, 0:tq] - lam_ref[0] * o[:, tq:2 * tq]
    ms = jnp.mean(od * od, axis=0, keepdims=True)
    on = (od * lax.rsqrt(ms + SUBLN_EPS) * sw_ref[...]) * (1.0 - LAM_INIT)
    o_ref[...] = on.T.astype(o_ref.dtype)


def _diff_attention(zqk, vt, lam, subln_w, *, batch, seq, n_heads, tq, tk):
    T = zqk.shape[0]
    hw = 2 * DA_HEAD_DIM
    nq = seq // tq
    nk = seq // tk
    sw_b = jnp.broadcast_to(subln_w.reshape(hw, 1), (hw, tq)).astype(F32)
    return pl.pallas_call(
        functools.partial(_attn_kernel, tq=tq, tk=tk),
        grid=(batch, n_heads, nq),
        in_specs=[
            pl.BlockSpec(memory_space=pltpu.SMEM),
            pl.BlockSpec((tq, hw), lambda b, h, i: (b * nq + i, h)),
            pl.BlockSpec((seq, hw), lambda b, h, i: (b, n_heads + h)),
            pl.BlockSpec((nk, hw, tk), lambda b, h, i: (b, h, 0)),
            pl.BlockSpec((hw, tq), lambda b, h, i: (0, 0)),
        ],
        out_specs=pl.BlockSpec((tq, hw), lambda b, h, i: (b * nq + i, h)),
        out_shape=jax.ShapeDtypeStruct((T, n_heads * hw), BF16),
        scratch_shapes=[
            pltpu.VMEM((2 * tq, hw), BF16),
            pltpu.VMEM((1, 2 * tq), F32),
            pltpu.VMEM((1, 2 * tq), F32),
            pltpu.VMEM((hw, 2 * tq), F32),
        ],
        compiler_params=_cparams(("arbitrary", "arbitrary", "arbitrary")),
        name="diff_attn",
    )(lam.reshape(1), zqk, zqk, vt, sw_b)


RW_CHUNK = 64
RW_PAIR = 2 * RW_HEAD_DIM
RW_INV_BLOCK = 16


def _bmm(a, b):
    return jnp.einsum("bij,bjk->bik", a.astype(BF16), b.astype(BF16), preferred_element_type=F32)


def _bmm_nt(a, b):
    return jnp.einsum("bik,bjk->bij", a.astype(BF16), b.astype(BF16), preferred_element_type=F32)


def _bmm_tn(a, b):
    return jnp.einsum("bti,btj->bij", a.astype(BF16), b.astype(BF16), preferred_element_type=F32)


def _mm_split(x, e):
    hi = x.astype(BF16)
    lo = (x - hi.astype(F32)).astype(BF16)
    return (jnp.dot(hi, e, preferred_element_type=F32) + jnp.dot(lo, e, preferred_element_type=F32))


def _sigmoid(x):
    return 1.0 / (1.0 + jnp.exp(-x))


def _softplus(x):
    return jnp.maximum(x, 0.0) + jnp.log(1.0 + jnp.exp(-jnp.abs(x)))


def _unit_lower_inverse(a, eye, diag_blk):
    ad = jnp.where(diag_blk, a, 0.0)
    ao = a - ad
    a2 = _bmm(ad, ad)
    a4 = _bmm(a2, a2)
    a8 = _bmm(a4, a4)
    td = eye + ad
    td = td + _bmm(td, a2)
    td = td + _bmm(td, a4)
    td = td + _bmm(td, a8)
    n1 = _bmm(td, ao)
    n2 = _bmm(n1, n1)
    x = td + _bmm(n2, td)
    return x + _bmm(n1, x)


def _rwkv_kernel(zr_ref, zk_ref, zv_ref, zwa_ref, zg_ref, mur_ref, muk_ref, muv_ref, muwa_ref, mug_ref,
                 vec_ref, ww_ref, wa_ref, gup_ref, o_ref,
                 cr_ref, ck_ref, cv_ref, cwa_ref, cg_ref, state_ref, *, L, PW):
    t = pl.program_id(2)
    C = RW_CHUNK
    PL = RW_PAIR
    NC = L // C

    @pl.when(t == 0)
    def _reset():
        for c_ref in (cr_ref, ck_ref, cv_ref, cwa_ref, cg_ref):
            c_ref[...] = jnp.zeros(c_ref.shape, F32)
        state_ref[...] = jnp.zeros(state_ref.shape, F32)

    def shift_mix(z_ref, mu_ref, c_ref):
        z = z_ref[...].astype(F32)
        row = lax.broadcasted_iota(jnp.int32, z.shape, 0)
        zprev = jnp.where(row == 0, c_ref[0:1, :], pltpu.roll(z, 1, 0))
        c_ref[0:1, :] = z[L - 1:L, :]
        return z + (zprev - z) * mu_ref[...]

    r = shift_mix(zr_ref, mur_ref, cr_ref)
    k = shift_mix(zk_ref, muk_ref, ck_ref)
    v = shift_mix(zv_ref, muv_ref, cv_ref)
    zwa = shift_mix(zwa_ref, muwa_ref, cwa_ref)
    zg = shift_mix(zg_ref, mug_ref, cg_ref)

    ri = lax.broadcasted_iota(jnp.int32, (PL, PL), 0)
    ci = lax.broadcasted_iota(jnp.int32, (PL, PL), 1)
    eye = (ri == ci).astype(F32)
    strict = ci < ri
    incl = ci <= ri
    diag_blk = (ri // RW_INV_BLOCK) == (ci // RW_INV_BLOCK)
    seg_w = 2 * PL if (PW * PL) % (2 * PL) == 0 else PL
    sr = lax.broadcasted_iota(jnp.int32, (seg_w, seg_w), 0)
    sc = lax.broadcasted_iota(jnp.int32, (seg_w, seg_w), 1)
    seg_ones = ((sr // RW_HEAD_DIM) == (sc // RW_HEAD_DIM)).astype(BF16)
    rc = lax.broadcasted_iota(jnp.int32, (C, C), 0)
    cc = lax.broadcasted_iota(jnp.int32, (C, C), 1)
    tri_incl = (cc <= rc).astype(BF16)
    head0 = lax.broadcasted_iota(jnp.int32, (C, PL), 1) < RW_HEAD_DIM
    lanes = [slice(pi * PL, (pi + 1) * PL) for pi in range(PW)]
    rows = [slice(c * C, (c + 1) * C) for c in range(NC)]

    def seg_sum(x):
        return jnp.concatenate([_mm_split(x[:, c * seg_w:(c + 1) * seg_w], seg_ones)
                                for c in range(PW * PL // seg_w)], axis=1)

    w0, a0, k_k, k_a, r_k, lnx_w, lnx_b = (vec_ref[i:i + 1, :] for i in range(7))
    w_pre = w0 + jnp.dot(jnp.tanh(zwa).astype(BF16), ww_ref[...], preferred_element_type=F32)
    logdec = -jnp.exp(-_softplus(-w_pre) - 0.5)
    a = _sigmoid(a0 + jnp.dot(zwa.astype(BF16), wa_ref[...], preferred_element_type=F32))
    g = jnp.dot(_sigmoid(zg).astype(BF16), gup_ref[...], preferred_element_type=F32)
    kk = k * k_k
    kkn = kk / jnp.maximum(jnp.sqrt(seg_sum(kk * kk)), 1e-12)
    kf = k * (1.0 + (a - 1.0) * k_a)
    a_s = -kkn
    b_s = kkn * a
    bonus = seg_sum(r * kf * r_k) * v

    ld_hi = logdec.astype(BF16)
    ld_lo = (logdec - ld_hi.astype(F32)).astype(BF16)
    cum_c = [jnp.dot(tri_incl, ld_hi[rs], preferred_element_type=F32)
             + jnp.dot(tri_incl, ld_lo[rs], preferred_element_type=F32) for rs in rows]
    tot_c = [cu[C - 1:C, :] for cu in cum_c]
    cum = jnp.concatenate(cum_c, axis=0)
    tot = jnp.concatenate([jnp.broadcast_to(tc, (C, tc.shape[1])) for tc in tot_c], axis=0)
    p_inv = jnp.exp(-cum)
    p_end = jnp.exp(tot - cum)

    def stack(x):
        out = []
        for rs in rows:
            for sl in lanes:
                blk = x[rs, sl]
                out.append(jnp.concatenate([jnp.where(head0, blk, 0.0), jnp.where(head0, 0.0, blk)], axis=0))
        return jnp.stack(out, axis=0).astype(BF16)

    rt_s = stack(r * jnp.exp(cum))
    at_s = stack(a_s * jnp.exp(cum - logdec))
    kt_s = stack(kf * p_inv)
    bt_s = stack(b_s * p_inv)
    kh_s = stack(kf * p_end)
    bh_s = stack(b_s * p_end)
    v_s = stack(v)

    m1 = _bmm_nt(jnp.concatenate([at_s, rt_s], axis=1), jnp.concatenate([bt_s, kt_s], axis=1))
    a_ab = jnp.where(strict, m1[:, 0:PL, 0:PL], 0.0)
    a_ak = jnp.where(strict, m1[:, 0:PL, PL:2 * PL], 0.0)
    a_rb = jnp.where(incl, m1[:, PL:2 * PL, 0:PL], 0.0)
    a_rk = jnp.where(incl, m1[:, PL:2 * PL, PL:2 * PL], 0.0)
    tinv = _unit_lower_inverse(a_ab, eye, diag_blk)
    wu = _bmm(tinv, jnp.concatenate([at_s, _bmm(a_ak, v_s).astype(BF16)], axis=2))

    s = state_ref[...]
    us, s0s = [], []
    for c in range(NC):
        sel = slice(c * PW, (c + 1) * PW)
        sb = s.astype(BF16)
        u = _bmm_nt(wu[sel, :, 0:PL], sb) + wu[sel, :, PL:2 * PL]
        us.append(u)
        s0s.append(sb)
        p_tot = jnp.stack([jnp.exp(tot_c[c][:, sl]) for sl in lanes], axis=0)
        s = s * p_tot + _bmm_tn(jnp.concatenate([u.astype(BF16), v_s[sel]], axis=1),
                                jnp.concatenate([bh_s[sel], kh_s[sel]], axis=1))
    state_ref[...] = s

    u_all = jnp.concatenate(us, axis=0)
    s0_all = jnp.concatenate(s0s, axis=0)
    y2 = _bmm_nt(rt_s, s0_all) + _bmm(a_rb, u_all) + _bmm(a_rk, v_s)
    y2 = y2[:, 0:C, :] + y2[:, C:2 * C, :]
    y = jnp.concatenate([jnp.concatenate([y2[c * PW + pi] for pi in range(PW)], axis=1) for c in range(NC)], axis=0)

    inv_n = 1.0 / RW_HEAD_DIM
    mean = seg_sum(y) * inv_n
    yc = y - mean
    var = seg_sum(yc * yc) * inv_n
    yn = yc * lax.rsqrt(var + RW_GN_EPS) * lnx_w + lnx_b
    o_ref[...] = ((yn + bonus) * g).astype(o_ref.dtype)


def _rwkv7(zrw, mu, w0, w_up, a0, a_up, g_up, k_k, k_a, r_k, lnx_w, lnx_b, *, batch, seq, n_heads, L, PW):
    T = zrw.shape[0]
    HC = n_heads * RW_HEAD_DIM
    PWL = PW * RW_PAIR
    assert HC % PWL == 0 and seq % L == 0 and L % RW_CHUNK == 0
    ng = HC // PWL
    nt = seq // L
    lora_w = RW_DECAY_RANK + RW_ICLR_RANK
    assert lora_w == V7X_LANES and RW_GATE_RANK == V7X_LANES
    vecs = jnp.stack([w0, a0, k_k, k_a, r_k.reshape(HC), lnx_w, lnx_b, jnp.zeros((HC,), F32)], axis=0)
    ww = jnp.concatenate([w_up, jnp.zeros((RW_ICLR_RANK, HC), F32)], axis=0).astype(BF16)
    wa = jnp.concatenate([jnp.zeros((RW_DECAY_RANK, HC), F32), a_up], axis=0).astype(BF16)
    gw = g_up.astype(BF16)
    mu2 = mu.reshape(1, -1)
    cb = HC // PWL
    wa_blk = 3 * HC // V7X_LANES
    zrow = lambda b, g, t: b * nt + t
    big = lambda sec: pl.BlockSpec((L, PWL), lambda b, g, t: (zrow(b, g, t), sec * cb + g))
    small = lambda off: pl.BlockSpec((L, V7X_LANES), lambda b, g, t: (zrow(b, g, t), wa_blk + off))
    mu_big = lambda sec: pl.BlockSpec((1, PWL), lambda b, g, t: (0, sec * cb + g))
    mu_small = lambda off: pl.BlockSpec((1, V7X_LANES), lambda b, g, t: (0, wa_blk + off))
    wspec = pl.BlockSpec((V7X_LANES, PWL), lambda b, g, t: (0, g))
    return pl.pallas_call(
        functools.partial(_rwkv_kernel, L=L, PW=PW),
        grid=(batch, ng, nt),
        in_specs=[big(0), big(1), big(2), small(0), small(1),
                  mu_big(0), mu_big(1), mu_big(2), mu_small(0), mu_small(1),
                  pl.BlockSpec((8, PWL), lambda b, g, t: (0, g)),
                  wspec, wspec, wspec],
        out_specs=pl.BlockSpec((L, PWL), lambda b, g, t: (zrow(b, g, t), g)),
        out_shape=jax.ShapeDtypeStruct((T, HC), BF16),
        scratch_shapes=[pltpu.VMEM((8, PWL), F32), pltpu.VMEM((8, PWL), F32), pltpu.VMEM((8, PWL), F32),
                        pltpu.VMEM((8, V7X_LANES), F32), pltpu.VMEM((8, V7X_LANES), F32),
                        pltpu.VMEM((PW, RW_PAIR, RW_PAIR), F32)],
        compiler_params=_cparams(("arbitrary", "arbitrary", "arbitrary")),
        name="rwkv7_chunked",
    )(zrw, zrw, zrw, zrw, zrw, mu2, mu2, mu2, mu2, mu2, vecs, ww, wa, gw)


def _route(lg):
    lane_i = lax.broadcasted_iota(jnp.int32, lg.shape, 1)
    lane = lane_i.astype(F32)
    is_g = lane_i < N_GROUPS
    mg = jnp.max(jnp.where(is_g, lg, NEG_BIG), axis=-1, keepdims=True)
    eg = jnp.exp(jnp.where(is_g, lg - mg, NEG_BIG))
    pg = eg / jnp.sum(eg, axis=-1, keepdims=True)
    p_g = jnp.max(pg, axis=-1, keepdims=True)
    g_sel = jnp.min(jnp.where(is_g & (pg == p_g), lane, 1e9), axis=-1, keepdims=True)
    grp = (jnp.right_shift(lane_i, 3) - 1).astype(F32)
    is_e = (lane_i >= N_GROUPS) & (lane_i < N_GROUPS + N_EXPERTS) & (grp == g_sel)
    me = jnp.max(jnp.where(is_e, lg, NEG_BIG), axis=-1, keepdims=True)
    ee = jnp.exp(jnp.where(is_e, lg - me, NEG_BIG))
    pe = ee / jnp.sum(ee, axis=-1, keepdims=True)
    p1 = jnp.max(jnp.where(is_e, pe, -1.0), axis=-1, keepdims=True)
    i1 = jnp.min(jnp.where(is_e & (pe == p1), lane, 1e9), axis=-1, keepdims=True)
    rest = is_e & (lane != i1)
    p2 = jnp.max(jnp.where(rest, pe, -1.0), axis=-1, keepdims=True)
    i2 = jnp.min(jnp.where(rest & (pe == p2), lane, 1e9), axis=-1, keepdims=True)
    den = p1 + p2
    eid = jnp.where(lane_i == 0, i1 - N_GROUPS, jnp.where(lane_i == 1, i2 - N_GROUPS, 0.0)).astype(jnp.int32)
    gate = jnp.where(lane_i == 0, p_g * p1 / den, jnp.where(lane_i == 1, p_g * p2 / den, 0.0))
    return eid, gate


def _to_row_tiles(x):
    return x.astype(BF16).reshape(x.shape[0], x.shape[1] // V7X_LANES, V7X_LANES)


def _from_row_tiles(x3):
    return x3.reshape(x3.shape[0], x3.shape[1] * x3.shape[2])


def _outproj_kernel(oda_ref, orw_ref, wt_ref, wb_ref, x_ref, n2_ref, rcat_ref, rb_ref,
                    h_ref, xn_ref, eid_ref, gate_ref, *, tm, eps):
    acc = (jnp.dot(oda_ref[...], wt_ref[...], preferred_element_type=F32)
           + jnp.dot(orw_ref[...], wb_ref[...], preferred_element_type=F32))
    h = x_ref[...] + acc
    h_ref[...] = h
    ms = jnp.mean(h * h, axis=-1, keepdims=True)
    xn = h * lax.rsqrt(ms + eps) * n2_ref[...]
    xn_ref[...] = _to_row_tiles(xn)
    hw = jnp.dot(xn.astype(BF16), rcat_ref[...], preferred_element_type=F32)
    lg = hw[:, 0:V7X_LANES] + hw[:, V7X_LANES:2 * V7X_LANES] + rb_ref[...]
    eid, gate = _route(lg)
    eid_ref[...] = eid
    gate_ref[...] = gate


def _outproj_route(oda, orw, w_out, x2, n2, wg, bg, we, be, *, tm):
    T, D = x2.shape
    w1 = oda.shape[1]
    w2 = orw.shape[1]
    nl = D // V7X_LANES
    wt = w_out[:w1].astype(BF16)
    wb = w_out[w1:].astype(BF16)
    pad = V7X_LANES - N_GROUPS - N_EXPERTS
    wr = jnp.concatenate([wg, we, jnp.zeros((D, pad), F32)], axis=1)
    rb = jnp.concatenate([bg, be, jnp.zeros((pad,), F32)]).reshape(1, V7X_LANES)
    rhi = wr.astype(BF16)
    rlo = (wr - rhi.astype(F32)).astype(BF16)
    rcat = jnp.concatenate([rhi, rlo], axis=1)
    const = lambda shape: pl.BlockSpec(shape, lambda i: (0, 0))
    return pl.pallas_call(
        functools.partial(_outproj_kernel, tm=tm, eps=NORM_EPS),
        grid=(T // tm,),
        in_specs=[pl.BlockSpec((tm, w1), lambda i: (i, 0)), pl.BlockSpec((tm, w2), lambda i: (i, 0)),
                  const((w1, D)), const((w2, D)), pl.BlockSpec((tm, D), lambda i: (i, 0)), const((1, D)),
                  const((D, 2 * V7X_LANES)), const((1, V7X_LANES))],
        out_specs=[pl.BlockSpec((tm, D), lambda i: (i, 0)),
                   pl.BlockSpec((tm, nl, V7X_LANES), lambda i: (i, 0, 0)),
                   pl.BlockSpec((tm, V7X_LANES), lambda i: (i, 0)),
                   pl.BlockSpec((tm, V7X_LANES), lambda i: (i, 0))],
        out_shape=[jax.ShapeDtypeStruct((T, D), F32),
                   jax.ShapeDtypeStruct((T, nl, V7X_LANES), BF16),
                   jax.ShapeDtypeStruct((T, V7X_LANES), jnp.int32),
                   jax.ShapeDtypeStruct((T, V7X_LANES), F32)],
        compiler_params=_cparams(("arbitrary",)),
        name="outproj_route",
    )(oda, orw, wt, wb, x2, n2.reshape(1, D), rcat, rb)


MOE_DMA_UNROLL = 8


def _moe_kernel(blk_e_ref, nxt_e_ref, par_ref, nused_ref, tok_ref, tokn_ref, dst_ref,
                xn_hbm, w1_hbm, w3_hbm, w2_hbm, ys_hbm,
                xbuf, ybuf, wf1, wf3, wf2, w1b, w3b, w2b, sem_in, sem_out, sem_w, *, bm, nl, n_slots):
    i = pl.program_id(0)
    n_used = nused_ref[0]
    slot = i % 2

    def gather(ids_ref, dst_slot):
        def body(r, c):
            pltpu.make_async_copy(xn_hbm.at[ids_ref[0, r]], xbuf.at[dst_slot, r], sem_in.at[dst_slot]).start()
            return c
        lax.fori_loop(0, bm, body, 0, unroll=MOE_DMA_UNROLL)

    def wait_gather(src_slot):
        pltpu.make_async_copy(xn_hbm.at[pl.ds(0, bm)], xbuf.at[src_slot], sem_in.at[src_slot]).wait()

    def scatter():
        def body(r, c):
            pltpu.make_async_copy(ybuf.at[r], ys_hbm.at[dst_ref[0, r]], sem_out).start()
            return c
        lax.fori_loop(0, bm, body, 0, unroll=MOE_DMA_UNROLL)

    def wait_scatter():
        pltpu.make_async_copy(ybuf, ys_hbm.at[pl.ds(0, bm)], sem_out).wait()

    def weight_copies(e, wslot):
        return (pltpu.make_async_copy(w1_hbm.at[e], wf1.at[wslot], sem_w.at[wslot]),
                pltpu.make_async_copy(w3_hbm.at[e], wf3.at[wslot], sem_w.at[wslot]),
                pltpu.make_async_copy(w2_hbm.at[e], wf2.at[wslot], sem_w.at[wslot]))

    @pl.when(i == 0)
    def _prologue():
        for cp in weight_copies(blk_e_ref[0], par_ref[0]):
            cp.start()
        gather(tok_ref, 0)
        ybuf[...] = jnp.zeros(ybuf.shape, ybuf.dtype)
        spare = pltpu.make_async_copy(ybuf, ys_hbm.at[pl.ds(n_slots, bm)], sem_out)
        spare.start()
        spare.wait()

    @pl.when(i + 1 < n_used)
    def _prefetch():
        gather(tokn_ref, 1 - slot)

    @pl.when(i < n_used)
    def _active():
        e = blk_e_ref[i]
        first = jnp.logical_or(i == 0, e != blk_e_ref[jnp.maximum(i - 1, 0)])

        @pl.when(first)
        def _new_expert():
            wslot = par_ref[i]
            for cp in weight_copies(e, wslot):
                cp.wait()
            nxt = nxt_e_ref[i]

            @pl.when(nxt >= 0)
            def _():
                for cp in weight_copies(nxt, 1 - wslot):
                    cp.start()

            w1b[...] = wf1[wslot].astype(BF16)
            w3b[...] = wf3[wslot].astype(BF16)
            w2b[...] = wf2[wslot].astype(BF16)

        wait_gather(slot)
        x = _from_row_tiles(xbuf[slot])
        h1 = jnp.dot(x, w1b[...], preferred_element_type=F32)
        h3 = jnp.dot(x, w3b[...], preferred_element_type=F32)
        hh = (h1 * _sigmoid(h1) * h3).astype(BF16)
        y = jnp.dot(hh, w2b[...], preferred_element_type=F32)

        @pl.when(i > 0)
        def _drain_prev():
            wait_scatter()

        ybuf[...] = _to_row_tiles(y)
        scatter()

        @pl.when(i == n_used - 1)
        def _drain_last():
            wait_scatter()


def _moe_ffn(xn_lines, plan, w1, w3, w2, *, bm, n_slots):
    blk_e, nxt_e, par, n_used, tok_p, dst_p = plan
    E, D, DE = w1.shape
    nl = D // V7X_LANES
    nb = blk_e.shape[0]
    ids = lambda imap: pl.BlockSpec((None, 1, bm), imap, memory_space=pltpu.SMEM)
    grid_spec = pltpu.PrefetchScalarGridSpec(
        num_scalar_prefetch=4,
        grid=(nb,),
        in_specs=[
            ids(lambda i, *_: (i, 0, 0)),
            ids(lambda i, *_: (jnp.minimum(i + 1, nb - 1), 0, 0)),
            ids(lambda i, *_: (i, 0, 0)),
            pl.BlockSpec(memory_space=pl.ANY),
            pl.BlockSpec(memory_space=pl.ANY),
            pl.BlockSpec(memory_space=pl.ANY),
            pl.BlockSpec(memory_space=pl.ANY),
        ],
        out_specs=pl.BlockSpec(memory_space=pl.ANY),
        scratch_shapes=[pltpu.VMEM((2, bm, nl, V7X_LANES), BF16), pltpu.VMEM((bm, nl, V7X_LANES), BF16),
                        pltpu.VMEM((2, D, DE), F32), pltpu.VMEM((2, D, DE), F32), pltpu.VMEM((2, DE, D), F32),
                        pltpu.VMEM((D, DE), BF16), pltpu.VMEM((D, DE), BF16), pltpu.VMEM((DE, D), BF16),
                        pltpu.SemaphoreType.DMA((2,)), pltpu.SemaphoreType.DMA, pltpu.SemaphoreType.DMA((2,))],
    )
    tok3 = tok_p.reshape(nb, 1, bm)
    ys = pl.pallas_call(
        functools.partial(_moe_kernel, bm=bm, nl=nl, n_slots=n_slots),
        grid_spec=grid_spec,
        out_shape=jax.ShapeDtypeStruct((n_slots + bm, nl, V7X_LANES), BF16),
        compiler_params=_cparams(("arbitrary",)),
        name="moe_ffn",
    )(blk_e, nxt_e, par, n_used, tok3, tok3, dst_p.reshape(nb, 1, bm), xn_lines, w1, w3, w2)
    return ys.reshape((n_slots + bm) // TOP_K, TOP_K, nl, V7X_LANES)


def _moe_plan(eid, *, bm):
    T = eid.shape[0]
    M = T * TOP_K
    E = N_EXPERTS
    i32 = jnp.int32
    e_flat = eid.reshape(M)
    order = jnp.argsort(e_flat).astype(i32)
    experts = jnp.arange(E, dtype=i32)
    counts = jnp.sum((e_flat[:, None] == experts[None, :]).astype(i32), axis=0)
    start = jnp.cumsum(counts) - counts
    padded = (counts + bm - 1) // bm * bm
    pend = jnp.cumsum(padded)
    pstart = pend - padded
    nb = M // bm + E
    blk = jnp.arange(nb, dtype=i32)
    blk_start = blk * bm
    n_used = (pend[-1] // bm).astype(i32)
    blk_e = jnp.minimum(jnp.sum((blk_start[:, None] >= pend[None, :]).astype(i32), axis=1), E - 1)
    blk_e = jnp.where(blk < n_used, blk_e, blk_e[jnp.maximum(n_used - 1, 0)])
    off = blk_start - pstart[blk_e]
    base = start[blk_e] + off
    valid = jnp.where(blk < n_used, jnp.clip(counts[blk_e] - off, 0, bm), 0)
    r = jnp.arange(bm, dtype=i32)
    slot_p = order[jnp.clip(base[:, None] + r[None, :], 0, M - 1)]
    tok_p = slot_p // TOP_K
    dst_p = jnp.where(r[None, :] < valid[:, None], slot_p, M + r[None, :])
    seg_first = jnp.concatenate([jnp.ones((1,), bool), blk_e[1:] != blk_e[:-1]])
    par = (jnp.cumsum(seg_first.astype(i32)) - 1) % 2
    later = (experts[None, :] > experts[:, None]) & (counts[None, :] > 0)
    nxt_of = jnp.min(jnp.where(later, experts[None, :], E), axis=1)
    nxt_e = jnp.where(nxt_of == E, -1, nxt_of)[blk_e]
    return blk_e, nxt_e.astype(i32), par.astype(i32), n_used.reshape(1), tok_p, dst_p


def _combine_kernel(h_ref, ys_ref, gate_ref, fw_ref, o_ref, *, tm, nl, eps):
    gate = gate_ref[...]
    acc = h_ref[...]
    for j in range(TOP_K):
        yj = _from_row_tiles(ys_ref[:, j, :, :]).astype(F32)
        acc = acc + yj * gate[:, j:j + 1]
    ms = jnp.mean(acc * acc, axis=-1, keepdims=True)
    o_ref[...] = acc * lax.rsqrt(ms + eps) * fw_ref[...]


def _combine(h, ys, gate, fw, *, tm):
    T, D = h.shape
    nl = D // V7X_LANES
    return pl.pallas_call(
        functools.partial(_combine_kernel, tm=tm, nl=nl, eps=NORM_EPS),
        grid=(T // tm,),
        in_specs=[pl.BlockSpec((tm, D), lambda i: (i, 0)),
                  pl.BlockSpec((tm, TOP_K, nl, V7X_LANES), lambda i: (i, 0, 0, 0)),
                  pl.BlockSpec((tm, V7X_LANES), lambda i: (i, 0)),
                  pl.BlockSpec((1, D), lambda i: (0, 0))],
        out_specs=pl.BlockSpec((tm, D), lambda i: (i, 0)),
        out_shape=jax.ShapeDtypeStruct((T, D), F32),
        compiler_params=_cparams(("arbitrary",)),
        name="combine_norm",
    )(h, ys, gate, fw.reshape(1, D))


def _tiles(T, seq, D):
    pick = lambda n, prefs: next(p for p in prefs if n % p == 0)
    return dict(
        tm_in=pick(seq, (512, 256, 128)),
        tm_rw=pick(seq, (1024, 512, 256, 128)),
        tn_da=1024,
        tq=pick(seq, (1024, 512, 256, 128)),
        tk=pick(seq, (256, 128)),
        rw_L=pick(seq, (256, 128, 64)),
        rw_PW=8,
        tm_out=pick(T, (256, 128)),
        bm=256,
        tm_cmb=pick(T, (256, 128)),
    )


def kernel(x, norm1_w, w_in, lam_q1, lam_k1, lam_q2, lam_k2, subln_w, rw_mu, rw_w0, rw_w_up, rw_a0, rw_a_up, rw_g_up, rw_k_k, rw_k_a, rw_r_k, rw_lnx_w, rw_lnx_b, w_out, norm2_w, router_group_w, router_group_b, router_expert_w, router_expert_b, moe_w1, moe_w3, moe_w2, final_norm_w):
    B, S, D = x.shape
    T = B * S
    depth = w_in.shape[0]
    rw_heads = rw_w0.shape[1] // RW_HEAD_DIM
    rw_cols = rw_mu.shape[1]
    da_cols = w_in.shape[2] - rw_cols
    da_width = da_cols // 3
    da_heads = da_width // (2 * DA_HEAD_DIM)
    tl = _tiles(T, S, D)
    cos, sin = _rope_tables(S)
    q_scale = DA_HEAD_DIM ** -0.5 * math.log2(math.e)
    colscale = jnp.concatenate([jnp.full((da_width,), q_scale, F32), jnp.ones((da_width,), F32)])

    h = x.reshape(T, D)
    for l in range(depth):
        assert l == 0, "lam_init is specialised to the first layer"
        w_qk = _rope_column_order(w_in[l][:, :2 * da_width] * colscale).astype(BF16)
        w_v = w_in[l][:, 2 * da_width:da_cols].T.astype(BF16)
        w_rw = w_in[l][:, da_cols:].astype(BF16)
        zqk, xn1 = _inproj(h, norm1_w[l], w_qk, seq=S, tm=tl["tm_in"], tn=tl["tn_da"], out_dtype=BF16,
                           mode="rope", cos=cos, sin=sin, emit_xn=True)
        vt = _inproj(xn1, None, w_v, seq=S, tm=tl["tm_in"], tn=tl["tn_da"], out_dtype=BF16,
                     mode="transposed", tkv=tl["tk"])
        zrw = _inproj(xn1, None, w_rw, seq=S, tm=tl["tm_rw"], tn=rw_cols // 2, out_dtype=BF16, w_outer=True)
        lam = (jnp.exp(jnp.sum(lam_q1[l] * lam_k1[l])) - jnp.exp(jnp.sum(lam_q2[l] * lam_k2[l])) + LAM_INIT)
        o_da = _diff_attention(zqk, vt, lam.astype(F32), subln_w[l], batch=B, seq=S, n_heads=da_heads,
                               tq=tl["tq"], tk=tl["tk"])
        o_rw = _rwkv7(zrw, rw_mu[l], rw_w0[l], rw_w_up[l], rw_a0[l], rw_a_up[l], rw_g_up[l], rw_k_k[l], rw_k_a[l],
                      rw_r_k[l], rw_lnx_w[l], rw_lnx_b[l], batch=B, seq=S, n_heads=rw_heads,
                      L=tl["rw_L"], PW=tl["rw_PW"])
        h, xn_lines, eid, gate = _outproj_route(o_da, o_rw, w_out[l], h, norm2_w[l], router_group_w[l],
                                                router_group_b[l], router_expert_w[l], router_expert_b[l],
                                                tm=tl["tm_out"])
        plan = _moe_plan(eid[:, :TOP_K], bm=tl["bm"])
        ys = _moe_ffn(xn_lines, plan, moe_w1[l], moe_w3[l], moe_w2[l], bm=tl["bm"], n_slots=T * TOP_K)
        assert depth == 1, "the final norm is fused into the last layer's combine"
        out = _combine(h, ys, gate, final_norm_w, tm=tl["tm_cmb"])
    return out.reshape(B, S, D)
```

```python
import functools
import math

import jax
import jax.numpy as jnp
from jax import lax
from jax.experimental import pallas as pl
from jax.experimental.pallas import tpu as pltpu

F32 = jnp.float32
BF16 = jnp.bfloat16

DA_HEAD_DIM = 64
RW_HEAD_DIM = 64
RW_DECAY_RANK = 64
RW_ICLR_RANK = 64
RW_GATE_RANK = 128
ROPE_THETA = 10000.0
N_GROUPS = 8
EXPERTS_PER_GROUP = 8
N_EXPERTS = N_GROUPS * EXPERTS_PER_GROUP
TOP_K = 2
NORM_EPS = 1e-6
SUBLN_EPS = 1e-5
RW_GN_EPS = 64e-5
LAM_INIT = 0.8 - 0.6 * math.exp(-0.3 * 0)

V7X_LANES = 128
V7X_VMEM_LIMIT = 56 * 1024 * 1024
NEG_BIG = -1e30
ATTN_KV_PER_STEP = 4
ATTN_LOOKAHEAD = 8


def _cparams(sem):
    return pltpu.CompilerParams(dimension_semantics=sem, vmem_limit_bytes=V7X_VMEM_LIMIT)


def _inproj_kernel(*refs, mode, tn, tkv, eps, prenormed, emit_xn):
    refs = list(refs)
    x_ref = refs.pop(0)
    nw_ref = None if prenormed else refs.pop(0)
    w_ref = refs.pop(0)
    cos_ref, sin_ref = (refs.pop(0), refs.pop(0)) if mode == "rope" else (None, None)
    o_ref = refs.pop(0)
    xno_ref = refs.pop(0) if emit_xn else None
    xn_ref = x_ref if prenormed else refs.pop(0)

    if not prenormed:
        @pl.when(pl.program_id(1) == 0)
        def _norm():
            x = x_ref[...]
            ms = jnp.mean(x * x, axis=-1, keepdims=True)
            xn_ref[...] = (x * lax.rsqrt(ms + eps) * nw_ref[...]).astype(BF16)
            if emit_xn:
                xno_ref[...] = xn_ref[...]

    if mode == "transposed":
        acc_t = lax.dot_general(w_ref[...], xn_ref[...], (((1,), (1,)), ((), ())), preferred_element_type=F32)
        for c in range(acc_t.shape[1] // tkv):
            o_ref[c] = acc_t[:, c * tkv:(c + 1) * tkv].astype(o_ref.dtype)
        return

    acc = jnp.dot(xn_ref[...], w_ref[...], preferred_element_type=F32)
    if mode == "rope":
        cos = cos_ref[...]
        sin = sin_ref[...]
        for c in range(tn // V7X_LANES):
            cols = slice(c * V7X_LANES, (c + 1) * V7X_LANES)
            blk = acc[:, cols]
            o_ref[:, cols] = (blk * cos + pltpu.roll(blk, V7X_LANES // 2, 1) * sin).astype(o_ref.dtype)
    else:
        o_ref[...] = acc.astype(o_ref.dtype)


def _inproj(x2, nw, w, *, seq, tm, tn, out_dtype, mode="plain", cos=None, sin=None, tkv=None,
            emit_xn=False, w_outer=False):
    T, D = x2.shape
    prenormed = nw is None
    assert not (w_outer and not prenormed) and not (emit_xn and prenormed)
    N = w.shape[0] if mode == "transposed" else w.shape[1]
    assert T % tm == 0 and N % tn == 0 and seq % tm == 0
    rc = (lambda a, b: (b, a)) if w_outer else (lambda a, b: (a, b))
    im = lambda f: (lambda a, b: f(*rc(a, b)))
    w_spec = (pl.BlockSpec((tn, D), im(lambda i, j: (j, 0))) if mode == "transposed"
              else pl.BlockSpec((D, tn), im(lambda i, j: (0, j))))
    in_specs = [pl.BlockSpec((tm, D), im(lambda i, j: (i, 0)))]
    args = [x2]
    if not prenormed:
        in_specs.append(pl.BlockSpec((1, D), im(lambda i, j: (0, 0))))
        args.append(nw.reshape(1, D))
    in_specs.append(w_spec)
    args.append(w)
    if mode == "rope":
        ns = seq // tm
        in_specs += [pl.BlockSpec((tm, V7X_LANES), im(lambda i, j: (i % ns, 0))),
                     pl.BlockSpec((tm, V7X_LANES), im(lambda i, j: (i % ns, 0)))]
        args += [cos, sin]
    if mode == "transposed":
        assert tm % tkv == 0
        out_specs = [pl.BlockSpec((tm // tkv, tn, tkv), im(lambda i, j: (i, j, 0)))]
        out_shape = [jax.ShapeDtypeStruct((T // tkv, N, tkv), out_dtype)]
    else:
        out_specs = [pl.BlockSpec((tm, tn), im(lambda i, j: (i, j)))]
        out_shape = [jax.ShapeDtypeStruct((T, N), out_dtype)]
    if emit_xn:
        out_specs.append(pl.BlockSpec((tm, D), im(lambda i, j: (i, 0))))
        out_shape.append(jax.ShapeDtypeStruct((T, D), BF16))
    grid = (N // tn, T // tm) if w_outer else (T // tm, N // tn)
    outs = pl.pallas_call(
        functools.partial(_inproj_kernel, mode=mode, tn=tn, tkv=tkv, eps=NORM_EPS,
                          prenormed=prenormed, emit_xn=emit_xn),
        grid=grid,
        in_specs=in_specs,
        out_specs=out_specs,
        out_shape=out_shape,
        scratch_shapes=[] if prenormed else [pltpu.VMEM((tm, D), BF16)],
        compiler_params=_cparams(("arbitrary", "arbitrary")),
        name="inproj_" + mode,
    )(*args)
    return outs if emit_xn else outs[0]


def _rope_column_order(w):
    rows, width = w.shape
    half = DA_HEAD_DIM // 2
    w5 = w.reshape(rows, width // V7X_LANES, 2, 2, half)
    return w5.transpose(0, 1, 3, 2, 4).reshape(rows, width)


def _rope_tables(seq):
    half = DA_HEAD_DIM // 2
    inv = ROPE_THETA ** (-jnp.arange(half, dtype=F32) / half)
    ang = jnp.arange(seq, dtype=F32)[:, None] * inv[None, :]
    cos = jnp.cos(ang)
    sin = jnp.sin(ang)
    return jnp.tile(cos, (1, 4)), jnp.concatenate([-sin, -sin, sin, sin], axis=-1)


def _attn_kernel(lam_ref, q_ref, k_ref, vt_ref, sw_ref, o_ref, qs_ref, m_ref, l_ref, acc_ref, *, tq, tk):
    qi = pl.program_id(2)
    d = DA_HEAD_DIM
    q = q_ref[...]
    lane = lax.broadcasted_iota(jnp.int32, q.shape, 1)
    comp1 = (lane % d) < (d // 2)
    zero = jnp.zeros_like(q)
    qs_ref[0:tq, :] = jnp.where(comp1, q, zero)
    qs_ref[tq:2 * tq, :] = jnp.where(comp1, zero, q)
    m_ref[...] = jnp.full(m_ref.shape, NEG_BIG, F32)
    l_ref[...] = jnp.zeros(l_ref.shape, F32)
    acc_ref[...] = jnp.zeros(acc_ref.shape, F32)

    n_diag = tq // tk
    n_full = qi * n_diag

    n_strips = 2 * tq // tk

    def strip_scores(k, si, on_diagonal):
        s = lax.dot_general(k, qs_ref[si * tk:(si + 1) * tk, :], (((1,), (1,)), ((), ())),
                            preferred_element_type=F32)
        if on_diagonal:
            kpos = lax.broadcasted_iota(jnp.int32, s.shape, 0)
            qpos = lax.broadcasted_iota(jnp.int32, s.shape, 1)
            s = jnp.where(kpos <= qpos, s, NEG_BIG)
        return s

    def strip_update(vt, si, s):
        lanes = slice(si * tk, (si + 1) * tk)
        m_old = m_ref[:, lanes]
        m_new = jnp.maximum(m_old, jnp.max(s, axis=0, keepdims=True))
        alpha = jnp.exp2(m_old - m_new)
        p = jnp.exp2(s - m_new)
        l_ref[:, lanes] = alpha * l_ref[:, lanes] + jnp.sum(p, axis=0, keepdims=True)
        acc_ref[:, lanes] = alpha * acc_ref[:, lanes] + jnp.dot(vt, p.astype(BF16), preferred_element_type=F32)
        m_ref[:, lanes] = m_new

    def kv_blocks(blocks):
        kv = [(k_ref[pl.ds(pl.multiple_of(j * tk, tk), tk), :], vt_ref[j]) for j, _ in blocks]
        work = [(b, si, dg) for b, (_, strips) in enumerate(blocks) for si, dg in strips]
        pending = [strip_scores(kv[b][0], si, dg) for b, si, dg in work[:ATTN_LOOKAHEAD]]
        for n, (b, si, _) in enumerate(work):
            if n + ATTN_LOOKAHEAD < len(work):
                nb, nsi, ndg = work[n + ATTN_LOOKAHEAD]
                pending.append(strip_scores(kv[nb][0], nsi, ndg))
            strip_update(kv[b][1], si, pending[n])

    all_strips = [(si, False) for si in range(n_strips)]

    def full_steps(i, c):
        kv_blocks([(i * ATTN_KV_PER_STEP + u, all_strips) for u in range(ATTN_KV_PER_STEP)])
        return c

    lax.fori_loop(0, n_full // ATTN_KV_PER_STEP, full_steps, 0)

    def full_step(j, c):
        kv_blocks([(j, all_strips)])
        return c

    lax.fori_loop(n_full // ATTN_KV_PER_STEP * ATTN_KV_PER_STEP, n_full, full_step, 0)

    kv_blocks([(n_full + c, [(si, si % n_diag == c) for si in range(n_strips) if si % n_diag >= c])
               for c in range(n_diag)])

    o = acc_ref[...] / l_ref[...]
    od = o[:, 0:tq] - lam_ref[0] * o[:, tq:2 * tq]
    ms = jnp.mean(od * od, axis=0, keepdims=True)
    on = (od * lax.rsqrt(ms + SUBLN_EPS) * sw_ref[...]) * (1.0 - LAM_INIT)
    o_ref[...] = on.T.astype(o_ref.dtype)


def _diff_attention(zqk, vt, lam, subln_w, *, batch, seq, n_heads, tq, tk):
    T = zqk.shape[0]
    hw = 2 * DA_HEAD_DIM
    nq = seq // tq
    nk = seq // tk
    sw_b = jnp.broadcast_to(subln_w.reshape(hw, 1), (hw, tq)).astype(F32)
    return pl.pallas_call(
        functools.partial(_attn_kernel, tq=tq, tk=tk),
        grid=(batch, n_heads, nq),
        in_specs=[
            pl.BlockSpec(memory_space=pltpu.SMEM),
            pl.BlockSpec((tq, hw), lambda b, h, i: (b * nq + i, h)),
            pl.BlockSpec((seq, hw), lambda b, h, i: (b, n_heads + h)),
            pl.BlockSpec((nk, hw, tk), lambda b, h, i: (b, h, 0)),
            pl.BlockSpec((hw, tq), lambda b, h, i: (0, 0)),
        ],
        out_specs=pl.BlockSpec((tq, hw), lambda b, h, i: (b * nq + i, h)),
        out_shape=jax.ShapeDtypeStruct((T, n_heads * hw), BF16),
        scratch_shapes=[
            pltpu.VMEM((2 * tq, hw), BF16),
            pltpu.VMEM((1, 2 * tq), F32),
            pltpu.VMEM((1, 2 * tq), F32),
            pltpu.VMEM((hw, 2 * tq), F32),
        ],
        compiler_params=_cparams(("arbitrary", "arbitrary", "arbitrary")),
        name="diff_attn",
    )(lam.reshape(1), zqk, zqk, vt, sw_b)


RW_CHUNK = 64
RW_PAIR = 2 * RW_HEAD_DIM
RW_INV_BLOCK = 16


def _bmm(a, b):
    return jnp.einsum("bij,bjk->bik", a.astype(BF16), b.astype(BF16), preferred_element_type=F32)


def _bmm_nt(a, b):
    return jnp.einsum("bik,bjk->bij", a.astype(BF16), b.astype(BF16), preferred_element_type=F32)


def _bmm_tn(a, b):
    return jnp.einsum("bti,btj->bij", a.astype(BF16), b.astype(BF16), preferred_element_type=F32)


def _mm_split(x, e):
    hi = x.astype(BF16)
    lo = (x - hi.astype(F32)).astype(BF16)
    return (jnp.dot(hi, e, preferred_element_type=F32) + jnp.dot(lo, e, preferred_element_type=F32))


def _sigmoid(x):
    return 1.0 / (1.0 + jnp.exp(-x))


def _softplus(x):
    return jnp.maximum(x, 0.0) + jnp.log(1.0 + jnp.exp(-jnp.abs(x)))


def _unit_lower_inverse(a, eye, diag_blk):
    ad = jnp.where(diag_blk, a, 0.0)
    ao = a - ad
    a2 = _bmm(ad, ad)
    a4 = _bmm(a2, a2)
    a8 = _bmm(a4, a4)
    td = eye + ad
    td = td + _bmm(td, a2)
    td = td + _bmm(td, a4)
    td = td + _bmm(td, a8)
    n1 = _bmm(td, ao)
    n2 = _bmm(n1, n1)
    x = td + _bmm(n2, td)
    return x + _bmm(n1, x)


def _rwkv_kernel(zr_ref, zk_ref, zv_ref, zwa_ref, zg_ref, mur_ref, muk_ref, muv_ref, muwa_ref, mug_ref,
                 vec_ref, ww_ref, wa_ref, gup_ref, o_ref,
                 cr_ref, ck_ref, cv_ref, cwa_ref, cg_ref, state_ref, *, L, PW):
    t = pl.program_id(2)
    C = RW_CHUNK
    PL = RW_PAIR
    NC = L // C

    @pl.when(t == 0)
    def _reset():
        for c_ref in (cr_ref, ck_ref, cv_ref, cwa_ref, cg_ref):
            c_ref[...] = jnp.zeros(c_ref.shape, F32)
        state_ref[...] = jnp.zeros(state_ref.shape, F32)

    def shift_mix(z_ref, mu_ref, c_ref):
        z = z_ref[...].astype(F32)
        row = lax.broadcasted_iota(jnp.int32, z.shape, 0)
        zprev = jnp.where(row == 0, c_ref[0:1, :], pltpu.roll(z, 1, 0))
        c_ref[0:1, :] = z[L - 1:L, :]
        return z + (zprev - z) * mu_ref[...]

    r = shift_mix(zr_ref, mur_ref, cr_ref)
    k = shift_mix(zk_ref, muk_ref, ck_ref)
    v = shift_mix(zv_ref, muv_ref, cv_ref)
    zwa = shift_mix(zwa_ref, muwa_ref, cwa_ref)
    zg = shift_mix(zg_ref, mug_ref, cg_ref)

    ri = lax.broadcasted_iota(jnp.int32, (PL, PL), 0)
    ci = lax.broadcasted_iota(jnp.int32, (PL, PL), 1)
    eye = (ri == ci).astype(F32)
    strict = ci < ri
    incl = ci <= ri
    diag_blk = (ri // RW_INV_BLOCK) == (ci // RW_INV_BLOCK)
    seg_w = 2 * PL if (PW * PL) % (2 * PL) == 0 else PL
    sr = lax.broadcasted_iota(jnp.int32, (seg_w, seg_w), 0)
    sc = lax.broadcasted_iota(jnp.int32, (seg_w, seg_w), 1)
    seg_ones = ((sr // RW_HEAD_DIM) == (sc // RW_HEAD_DIM)).astype(BF16)
    rc = lax.broadcasted_iota(jnp.int32, (C, C), 0)
    cc = lax.broadcasted_iota(jnp.int32, (C, C), 1)
    tri_incl = (cc <= rc).astype(BF16)
    head0 = lax.broadcasted_iota(jnp.int32, (C, PL), 1) < RW_HEAD_DIM
    lanes = [slice(pi * PL, (pi + 1) * PL) for pi in range(PW)]
    rows = [slice(c * C, (c + 1) * C) for c in range(NC)]

    def seg_sum(x):
        return jnp.concatenate([_mm_split(x[:, c * seg_w:(c + 1) * seg_w], seg_ones)
                                for c in range(PW * PL // seg_w)], axis=1)

    w0, a0, k_k, k_a, r_k, lnx_w, lnx_b = (vec_ref[i:i + 1, :] for i in range(7))
    w_pre = w0 + jnp.dot(jnp.tanh(zwa).astype(BF16), ww_ref[...], preferred_element_type=F32)
    logdec = -jnp.exp(-_softplus(-w_pre) - 0.5)
    a = _sigmoid(a0 + jnp.dot(zwa.astype(BF16), wa_ref[...], preferred_element_type=F32))
    g = jnp.dot(_sigmoid(zg).astype(BF16), gup_ref[...], preferred_element_type=F32)
    kk = k * k_k
    kkn = kk / jnp.maximum(jnp.sqrt(seg_sum(kk * kk)), 1e-12)
    kf = k * (1.0 + (a - 1.0) * k_a)
    a_s = -kkn
    b_s = kkn * a
    bonus = seg_sum(r * kf * r_k) * v

    ld_hi = logdec.astype(BF16)
    ld_lo = (logdec - ld_hi.astype(F32)).astype(BF16)
    cum_c = [jnp.dot(tri_incl, ld_hi[rs], preferred_element_type=F32)
             + jnp.dot(tri_incl, ld_lo[rs], preferred_element_type=F32) for rs in rows]
    tot_c = [cu[C - 1:C, :] for cu in cum_c]
    cum = jnp.concatenate(cum_c, axis=0)
    tot = jnp.concatenate([jnp.broadcast_to(tc, (C, tc.shape[1])) for tc in tot_c], axis=0)
    p_inv = jnp.exp(-cum)
    p_end = jnp.exp(tot - cum)

    def stack(x):
        out = []
        for rs in rows:
            for sl in lanes:
                blk = x[rs, sl]
                out.append(jnp.concatenate([jnp.where(head0, blk, 0.0), jnp.where(head0, 0.0, blk)], axis=0))
        return jnp.stack(out, axis=0).astype(BF16)

    rt_s = stack(r * jnp.exp(cum))
    at_s = stack(a_s * jnp.exp(cum - logdec))
    kt_s = stack(kf * p_inv)
    bt_s = stack(b_s * p_inv)
    kh_s = stack(kf * p_end)
    bh_s = stack(b_s * p_end)
    v_s = stack(v)

    m1 = _bmm_nt(jnp.concatenate([at_s, rt_s], axis=1), jnp.concatenate([bt_s, kt_s], axis=1))
    a_ab = jnp.where(strict, m1[:, 0:PL, 0:PL], 0.0)
    a_ak = jnp.where(strict, m1[:, 0:PL, PL:2 * PL], 0.0)
    a_rb = jnp.where(incl, m1[:, PL:2 * PL, 0:PL], 0.0)
    a_rk = jnp.where(incl, m1[:, PL:2 * PL, PL:2 * PL], 0.0)
    tinv = _unit_lower_inverse(a_ab, eye, diag_blk)
    wu = _bmm(tinv, jnp.concatenate([at_s, _bmm(a_ak, v_s).astype(BF16)], axis=2))

    s = state_ref[...]
    us, s0s = [], []
    for c in range(NC):
        sel = slice(c * PW, (c + 1) * PW)
        sb = s.astype(BF16)
        u = _bmm_nt(wu[sel, :, 0:PL], sb) + wu[sel, :, PL:2 * PL]
        us.append(u)
        s0s.append(sb)
        p_tot = jnp.stack([jnp.exp(tot_c[c][:, sl]) for sl in lanes], axis=0)
        s = s * p_tot + _bmm_tn(jnp.concatenate([u.astype(BF16), v_s[sel]], axis=1),
                                jnp.concatenate([bh_s[sel], kh_s[sel]], axis=1))
    state_ref[...] = s

    u_all = jnp.concatenate(us, axis=0)
    s0_all = jnp.concatenate(s0s, axis=0)
    y2 = _bmm_nt(rt_s, s0_all) + _bmm(a_rb, u_all) + _bmm(a_rk, v_s)
    y2 = y2[:, 0:C, :] + y2[:, C:2 * C, :]
    y = jnp.concatenate([jnp.concatenate([y2[c * PW + pi] for pi in range(PW)], axis=1) for c in range(NC)], axis=0)

    inv_n = 1.0 / RW_HEAD_DIM
    mean = seg_sum(y) * inv_n
    yc = y - mean
    var = seg_sum(yc * yc) * inv_n
    yn = yc * lax.rsqrt(var + RW_GN_EPS) * lnx_w + lnx_b
    o_ref[...] = ((yn + bonus) * g).astype(o_ref.dtype)


def _rwkv7(zrw, mu, w0, w_up, a0, a_up, g_up, k_k, k_a, r_k, lnx_w, lnx_b, *, batch, seq, n_heads, L, PW):
    T = zrw.shape[0]
    HC = n_heads * RW_HEAD_DIM
    PWL = PW * RW_PAIR
    assert HC % PWL == 0 and seq % L == 0 and L % RW_CHUNK == 0
    ng = HC // PWL
    nt = seq // L
    lora_w = RW_DECAY_RANK + RW_ICLR_RANK
    assert lora_w == V7X_LANES and RW_GATE_RANK == V7X_LANES
    vecs = jnp.stack([w0, a0, k_k, k_a, r_k.reshape(HC), lnx_w, lnx_b, jnp.zeros((HC,), F32)], axis=0)
    ww = jnp.concatenate([w_up, jnp.zeros((RW_ICLR_RANK, HC), F32)], axis=0).astype(BF16)
    wa = jnp.concatenate([jnp.zeros((RW_DECAY_RANK, HC), F32), a_up], axis=0).astype(BF16)
    gw = g_up.astype(BF16)
    mu2 = mu.reshape(1, -1)
    cb = HC // PWL
    wa_blk = 3 * HC // V7X_LANES
    zrow = lambda b, g, t: b * nt + t
    big = lambda sec: pl.BlockSpec((L, PWL), lambda b, g, t: (zrow(b, g, t), sec * cb + g))
    small = lambda off: pl.BlockSpec((L, V7X_LANES), lambda b, g, t: (zrow(b, g, t), wa_blk + off))
    mu_big = lambda sec: pl.BlockSpec((1, PWL), lambda b, g, t: (0, sec * cb + g))
    mu_small = lambda off: pl.BlockSpec((1, V7X_LANES), lambda b, g, t: (0, wa_blk + off))
    wspec = pl.BlockSpec((V7X_LANES, PWL), lambda b, g, t: (0, g))
    return pl.pallas_call(
        functools.partial(_rwkv_kernel, L=L, PW=PW),
        grid=(batch, ng, nt),
        in_specs=[big(0), big(1), big(2), small(0), small(1),
                  mu_big(0), mu_big(1), mu_big(2), mu_small(0), mu_small(1),
                  pl.BlockSpec((8, PWL), lambda b, g, t: (0, g)),
                  wspec, wspec, wspec],
        out_specs=pl.BlockSpec((L, PWL), lambda b, g, t: (zrow(b, g, t), g)),
        out_shape=jax.ShapeDtypeStruct((T, HC), BF16),
        scratch_shapes=[pltpu.VMEM((8, PWL), F32), pltpu.VMEM((8, PWL), F32), pltpu.VMEM((8, PWL), F32),
                        pltpu.VMEM((8, V7X_LANES), F32), pltpu.VMEM((8, V7X_LANES), F32),
                        pltpu.VMEM((PW, RW_PAIR, RW_PAIR), F32)],
        compiler_params=_cparams(("arbitrary", "arbitrary", "arbitrary")),
        name="rwkv7_chunked",
    )(zrw, zrw, zrw, zrw, zrw, mu2, mu2, mu2, mu2, mu2, vecs, ww, wa, gw)


def _route(lg):
    lane_i = lax.broadcasted_iota(jnp.int32, lg.shape, 1)
    lane = lane_i.astype(F32)
    is_g = lane_i < N_GROUPS
    mg = jnp.max(jnp.where(is_g, lg, NEG_BIG), axis=-1, keepdims=True)
    eg = jnp.exp(jnp.where(is_g, lg - mg, NEG_BIG))
    pg = eg / jnp.sum(eg, axis=-1, keepdims=True)
    p_g = jnp.max(pg, axis=-1, keepdims=True)
    g_sel = jnp.min(jnp.where(is_g & (pg == p_g), lane, 1e9), axis=-1, keepdims=True)
    grp = (jnp.right_shift(lane_i, 3) - 1).astype(F32)
    is_e = (lane_i >= N_GROUPS) & (lane_i < N_GROUPS + N_EXPERTS) & (grp == g_sel)
    me = jnp.max(jnp.where(is_e, lg, NEG_BIG), axis=-1, keepdims=True)
    ee = jnp.exp(jnp.where(is_e, lg - me, NEG_BIG))
    pe = ee / jnp.sum(ee, axis=-1, keepdims=True)
    p1 = jnp.max(jnp.where(is_e, pe, -1.0), axis=-1, keepdims=True)
    i1 = jnp.min(jnp.where(is_e & (pe == p1), lane, 1e9), axis=-1, keepdims=True)
    rest = is_e & (lane != i1)
    p2 = jnp.max(jnp.where(rest, pe, -1.0), axis=-1, keepdims=True)
    i2 = jnp.min(jnp.where(rest & (pe == p2), lane, 1e9), axis=-1, keepdims=True)
    den = p1 + p2
    eid = jnp.where(lane_i == 0, i1 - N_GROUPS, jnp.where(lane_i == 1, i2 - N_GROUPS, 0.0)).astype(jnp.int32)
    gate = jnp.where(lane_i == 0, p_g * p1 / den, jnp.where(lane_i == 1, p_g * p2 / den, 0.0))
    return eid, gate


def _to_row_tiles(x):
    return x.astype(BF16).reshape(x.shape[0], x.shape[1] // V7X_LANES, V7X_LANES)


def _from_row_tiles(x3):
    return x3.reshape(x3.shape[0], x3.shape[1] * x3.shape[2])


def _outproj_kernel(oda_ref, orw_ref, wt_hbm, wb_hbm, x_ref, n2_ref, rcat_ref, rb_ref,
                    h_ref, xn_ref, eid_ref, gate_ref, wt_ref, wb_ref, sem, *, tm, eps):
    @pl.when(pl.program_id(0) == 0)
    def _load_weights():
        copies = (pltpu.make_async_copy(wt_hbm, wt_ref, sem), pltpu.make_async_copy(wb_hbm, wb_ref, sem))
        for cp in copies:
            cp.start()
        for cp in copies:
            cp.wait()

    acc = (jnp.dot(oda_ref[...], wt_ref[...], preferred_element_type=F32)
           + jnp.dot(orw_ref[...], wb_ref[...], preferred_element_type=F32))
    h = x_ref[...] + acc
    h_ref[...] = h
    ms = jnp.mean(h * h, axis=-1, keepdims=True)
    xn = h * lax.rsqrt(ms + eps) * n2_ref[...]
    xn_ref[...] = _to_row_tiles(xn)
    hw = jnp.dot(xn.astype(BF16), rcat_ref[...], preferred_element_type=F32)
    lg = hw[:, 0:V7X_LANES] + hw[:, V7X_LANES:2 * V7X_LANES] + rb_ref[...]
    eid, gate = _route(lg)
    eid_ref[...] = eid
    gate_ref[...] = gate


def _outproj_route(oda, orw, w_out, x2, n2, wg, bg, we, be, *, tm):
    T, D = x2.shape
    w1 = oda.shape[1]
    w2 = orw.shape[1]
    nl = D // V7X_LANES
    wt = w_out[:w1].astype(BF16)
    wb = w_out[w1:].astype(BF16)
    pad = V7X_LANES - N_GROUPS - N_EXPERTS
    wr = jnp.concatenate([wg, we, jnp.zeros((D, pad), F32)], axis=1)
    rb = jnp.concatenate([bg, be, jnp.zeros((pad,), F32)]).reshape(1, V7X_LANES)
    rhi = wr.astype(BF16)
    rlo = (wr - rhi.astype(F32)).astype(BF16)
    rcat = jnp.concatenate([rhi, rlo], axis=1)
    const = lambda shape: pl.BlockSpec(shape, lambda i: (0, 0))
    return pl.pallas_call(
        functools.partial(_outproj_kernel, tm=tm, eps=NORM_EPS),
        grid=(T // tm,),
        in_specs=[pl.BlockSpec((tm, w1), lambda i: (i, 0)), pl.BlockSpec((tm, w2), lambda i: (i, 0)),
                  pl.BlockSpec(memory_space=pl.ANY), pl.BlockSpec(memory_space=pl.ANY),
                  pl.BlockSpec((tm, D), lambda i: (i, 0)), const((1, D)),
                  const((D, 2 * V7X_LANES)), const((1, V7X_LANES))],
        out_specs=[pl.BlockSpec((tm, D), lambda i: (i, 0)),
                   pl.BlockSpec((tm, nl, V7X_LANES), lambda i: (i, 0, 0)),
                   pl.BlockSpec((tm, V7X_LANES), lambda i: (i, 0)),
                   pl.BlockSpec((tm, V7X_LANES), lambda i: (i, 0))],
        out_shape=[jax.ShapeDtypeStruct((T, D), F32),
                   jax.ShapeDtypeStruct((T, nl, V7X_LANES), BF16),
                   jax.ShapeDtypeStruct((T, V7X_LANES), jnp.int32),
                   jax.ShapeDtypeStruct((T, V7X_LANES), F32)],
        scratch_shapes=[pltpu.VMEM((w1, D), BF16), pltpu.VMEM((w2, D), BF16), pltpu.SemaphoreType.DMA],
        compiler_params=_cparams(("arbitrary",)),
        name="outproj_route",
    )(oda, orw, wt, wb, x2, n2.reshape(1, D), rcat, rb)


MOE_DMA_UNROLL = 8


def _moe_kernel(blk_e_ref, nxt_e_ref, par_ref, nused_ref, tok_ref, tokn_ref, dst_ref,
                xn_hbm, w1_hbm, w3_hbm, w2_hbm, ys_hbm,
                xbuf, ybuf, wf1, wf3, wf2, w1b, w3b, w2b, sem_in, sem_out, sem_w, *, bm, nl, n_slots):
    i = pl.program_id(0)
    n_used = nused_ref[0]
    slot = i % 2

    def gather(ids_ref, dst_slot):
        def body(r, c):
            pltpu.make_async_copy(xn_hbm.at[ids_ref[0, r]], xbuf.at[dst_slot, r], sem_in.at[dst_slot]).start()
            return c
        lax.fori_loop(0, bm, body, 0, unroll=MOE_DMA_UNROLL)

    def wait_gather(src_slot):
        pltpu.make_async_copy(xn_hbm.at[pl.ds(0, bm)], xbuf.at[src_slot], sem_in.at[src_slot]).wait()

    def scatter():
        def body(r, c):
            pltpu.make_async_copy(ybuf.at[r], ys_hbm.at[dst_ref[0, r]], sem_out).start()
            return c
        lax.fori_loop(0, bm, body, 0, unroll=MOE_DMA_UNROLL)

    def wait_scatter():
        pltpu.make_async_copy(ybuf, ys_hbm.at[pl.ds(0, bm)], sem_out).wait()

    def weight_copies(e, wslot):
        return (pltpu.make_async_copy(w1_hbm.at[e], wf1.at[wslot], sem_w.at[wslot]),
                pltpu.make_async_copy(w3_hbm.at[e], wf3.at[wslot], sem_w.at[wslot]),
                pltpu.make_async_copy(w2_hbm.at[e], wf2.at[wslot], sem_w.at[wslot]))

    @pl.when(i == 0)
    def _prologue():
        for cp in weight_copies(blk_e_ref[0], par_ref[0]):
            cp.start()
        gather(tok_ref, 0)
        ybuf[...] = jnp.zeros(ybuf.shape, ybuf.dtype)
        spare = pltpu.make_async_copy(ybuf, ys_hbm.at[pl.ds(n_slots, bm)], sem_out)
        spare.start()
        spare.wait()

    @pl.when(i + 1 < n_used)
    def _prefetch():
        gather(tokn_ref, 1 - slot)

    @pl.when(i < n_used)
    def _active():
        e = blk_e_ref[i]
        first = jnp.logical_or(i == 0, e != blk_e_ref[jnp.maximum(i - 1, 0)])

        @pl.when(first)
        def _new_expert():
            wslot = par_ref[i]
            for cp in weight_copies(e, wslot):
                cp.wait()
            nxt = nxt_e_ref[i]

            @pl.when(nxt >= 0)
            def _():
                for cp in weight_copies(nxt, 1 - wslot):
                    cp.start()

            w1b[...] = wf1[wslot].astype(BF16)
            w3b[...] = wf3[wslot].astype(BF16)
            w2b[...] = wf2[wslot].astype(BF16)

        wait_gather(slot)
        x = _from_row_tiles(xbuf[slot])
        h1 = jnp.dot(x, w1b[...], preferred_element_type=F32)
        h3 = jnp.dot(x, w3b[...], preferred_element_type=F32)
        hh = (h1 * _sigmoid(h1) * h3).astype(BF16)
        y = jnp.dot(hh, w2b[...], preferred_element_type=F32)

        @pl.when(i > 0)
        def _drain_prev():
            wait_scatter()

        ybuf[...] = _to_row_tiles(y)
        scatter()

        @pl.when(i == n_used - 1)
        def _drain_last():
            wait_scatter()


def _moe_ffn(xn_lines, plan, w1, w3, w2, *, bm, n_slots):
    blk_e, nxt_e, par, n_used, tok_p, dst_p = plan
    E, D, DE = w1.shape
    nl = D // V7X_LANES
    nb = blk_e.shape[0]
    ids = lambda imap: pl.BlockSpec((None, 1, bm), imap, memory_space=pltpu.SMEM)
    grid_spec = pltpu.PrefetchScalarGridSpec(
        num_scalar_prefetch=4,
        grid=(nb,),
        in_specs=[
            ids(lambda i, *_: (i, 0, 0)),
            ids(lambda i, *_: (jnp.minimum(i + 1, nb - 1), 0, 0)),
            ids(lambda i, *_: (i, 0, 0)),
            pl.BlockSpec(memory_space=pl.ANY),
            pl.BlockSpec(memory_space=pl.ANY),
            pl.BlockSpec(memory_space=pl.ANY),
            pl.BlockSpec(memory_space=pl.ANY),
        ],
        out_specs=pl.BlockSpec(memory_space=pl.ANY),
        scratch_shapes=[pltpu.VMEM((2, bm, nl, V7X_LANES), BF16), pltpu.VMEM((bm, nl, V7X_LANES), BF16),
                        pltpu.VMEM((2, D, DE), F32), pltpu.VMEM((2, D, DE), F32), pltpu.VMEM((2, DE, D), F32),
                        pltpu.VMEM((D, DE), BF16), pltpu.VMEM((D, DE), BF16), pltpu.VMEM((DE, D), BF16),
                        pltpu.SemaphoreType.DMA((2,)), pltpu.SemaphoreType.DMA, pltpu.SemaphoreType.DMA((2,))],
    )
    tok3 = tok_p.reshape(nb, 1, bm)
    ys = pl.pallas_call(
        functools.partial(_moe_kernel, bm=bm, nl=nl, n_slots=n_slots),
        grid_spec=grid_spec,
        out_shape=jax.ShapeDtypeStruct((n_slots + bm, nl, V7X_LANES), BF16),
        compiler_params=_cparams(("arbitrary",)),
        name="moe_ffn",
    )(blk_e, nxt_e, par, n_used, tok3, tok3, dst_p.reshape(nb, 1, bm), xn_lines, w1, w3, w2)
    return ys.reshape((n_slots + bm) // TOP_K, TOP_K, nl, V7X_LANES)


def _moe_plan(eid, *, bm):
    T = eid.shape[0]
    M = T * TOP_K
    E = N_EXPERTS
    i32 = jnp.int32
    e_flat = eid.reshape(M)
    order = jnp.argsort(e_flat).astype(i32)
    experts = jnp.arange(E, dtype=i32)
    counts = jnp.sum((e_flat[:, None] == experts[None, :]).astype(i32), axis=0)
    start = jnp.cumsum(counts) - counts
    padded = (counts + bm - 1) // bm * bm
    pend = jnp.cumsum(padded)
    pstart = pend - padded
    nb = M // bm + E
    blk = jnp.arange(nb, dtype=i32)
    blk_start = blk * bm
    n_used = (pend[-1] // bm).astype(i32)
    blk_e = jnp.minimum(jnp.sum((blk_start[:, None] >= pend[None, :]).astype(i32), axis=1), E - 1)
    blk_e = jnp.where(blk < n_used, blk_e, blk_e[jnp.maximum(n_used - 1, 0)])
    off = blk_start - pstart[blk_e]
    base = start[blk_e] + off
    valid = jnp.where(blk < n_used, jnp.clip(counts[blk_e] - off, 0, bm), 0)
    r = jnp.arange(bm, dtype=i32)
    slot_p = order[jnp.clip(base[:, None] + r[None, :], 0, M - 1)]
    tok_p = slot_p // TOP_K
    dst_p = jnp.where(r[None, :] < valid[:, None], slot_p, M + r[None, :])
    seg_first = jnp.concatenate([jnp.ones((1,), bool), blk_e[1:] != blk_e[:-1]])
    par = (jnp.cumsum(seg_first.astype(i32)) - 1) % 2
    later = (experts[None, :] > experts[:, None]) & (counts[None, :] > 0)
    nxt_of = jnp.min(jnp.where(later, experts[None, :], E), axis=1)
    nxt_e = jnp.where(nxt_of == E, -1, nxt_of)[blk_e]
    return blk_e, nxt_e.astype(i32), par.astype(i32), n_used.reshape(1), tok_p, dst_p


def _combine_kernel(h_ref, ys_ref, gate_ref, fw_ref, o_ref, *, tm, nl, eps):
    gate = gate_ref[...]
    acc = h_ref[...]
    for j in range(TOP_K):
        yj = _from_row_tiles(ys_ref[:, j, :, :]).astype(F32)
        acc = acc + yj * gate[:, j:j + 1]
    ms = jnp.mean(acc * acc, axis=-1, keepdims=True)
    o_ref[...] = acc * lax.rsqrt(ms + eps) * fw_ref[...]


def _combine(h, ys, gate, fw, *, tm):
    T, D = h.shape
    nl = D // V7X_LANES
    return pl.pallas_call(
        functools.partial(_combine_kernel, tm=tm, nl=nl, eps=NORM_EPS),
        grid=(T // tm,),
        in_specs=[pl.BlockSpec((tm, D), lambda i: (i, 0)),
                  pl.BlockSpec((tm, TOP_K, nl, V7X_LANES), lambda i: (i, 0, 0, 0)),
                  pl.BlockSpec((tm, V7X_LANES), lambda i: (i, 0)),
                  pl.BlockSpec((1, D), lambda i: (0, 0))],
        out_specs=pl.BlockSpec((tm, D), lambda i: (i, 0)),
        out_shape=jax.ShapeDtypeStruct((T, D), F32),
        compiler_params=_cparams(("arbitrary",)),
        name="combine_norm",
    )(h, ys, gate, fw.reshape(1, D))


def _tiles(T, seq, D):
    pick = lambda n, prefs: next(p for p in prefs if n % p == 0)
    return dict(
        tm_in=pick(seq, (512, 256, 128)),
        tm_rw=pick(seq, (1024, 512, 256, 128)),
        tn_da=1024,
        tq=pick(seq, (1024, 512, 256, 128)),
        tk=pick(seq, (256, 128)),
        rw_L=pick(seq, (256, 128, 64)),
        rw_PW=8,
        tm_out=pick(T, (512, 256, 128)),
        bm=256,
        tm_cmb=pick(T, (512, 256, 128)),
    )


def kernel(x, norm1_w, w_in, lam_q1, lam_k1, lam_q2, lam_k2, subln_w, rw_mu, rw_w0, rw_w_up, rw_a0, rw_a_up, rw_g_up, rw_k_k, rw_k_a, rw_r_k, rw_lnx_w, rw_lnx_b, w_out, norm2_w, router_group_w, router_group_b, router_expert_w, router_expert_b, moe_w1, moe_w3, moe_w2, final_norm_w):
    B, S, D = x.shape
    T = B * S
    depth = w_in.shape[0]
    rw_heads = rw_w0.shape[1] // RW_HEAD_DIM
    rw_cols = rw_mu.shape[1]
    da_cols = w_in.shape[2] - rw_cols
    da_width = da_cols // 3
    da_heads = da_width // (2 * DA_HEAD_DIM)
    tl = _tiles(T, S, D)
    cos, sin = _rope_tables(S)
    q_scale = DA_HEAD_DIM ** -0.5 * math.log2(math.e)
    colscale = jnp.concatenate([jnp.full((da_width,), q_scale, F32), jnp.ones((da_width,), F32)])

    h = x.reshape(T, D)
    for l in range(depth):
        assert l == 0, "lam_init is specialised to the first layer"
        w_qk = _rope_column_order(w_in[l][:, :2 * da_width] * colscale).astype(BF16)
        w_v = w_in[l][:, 2 * da_width:da_cols].T.astype(BF16)
        w_rw = w_in[l][:, da_cols:].astype(BF16)
        zqk, xn1 = _inproj(h, norm1_w[l], w_qk, seq=S, tm=tl["tm_in"], tn=tl["tn_da"], out_dtype=BF16,
                           mode="rope", cos=cos, sin=sin, emit_xn=True)
        vt = _inproj(xn1, None, w_v, seq=S, tm=tl["tm_in"], tn=tl["tn_da"], out_dtype=BF16,
                     mode="transposed", tkv=tl["tk"])
        zrw = _inproj(xn1, None, w_rw, seq=S, tm=tl["tm_rw"], tn=rw_cols // 2, out_dtype=BF16, w_outer=True)
        lam = (jnp.exp(jnp.sum(lam_q1[l] * lam_k1[l])) - jnp.exp(jnp.sum(lam_q2[l] * lam_k2[l])) + LAM_INIT)
        o_da = _diff_attention(zqk, vt, lam.astype(F32), subln_w[l], batch=B, seq=S, n_heads=da_heads,
                               tq=tl["tq"], tk=tl["tk"])
        o_rw = _rwkv7(zrw, rw_mu[l], rw_w0[l], rw_w_up[l], rw_a0[l], rw_a_up[l], rw_g_up[l], rw_k_k[l], rw_k_a[l],
                      rw_r_k[l], rw_lnx_w[l], rw_lnx_b[l], batch=B, seq=S, n_heads=rw_heads,
                      L=tl["rw_L"], PW=tl["rw_PW"])
        h, xn_lines, eid, gate = _outproj_route(o_da, o_rw, w_out[l], h, norm2_w[l], router_group_w[l],
                                                router_group_b[l], router_expert_w[l], router_expert_b[l],
                                                tm=tl["tm_out"])
        plan = _moe_plan(eid[:, :TOP_K], bm=tl["bm"])
        ys = _moe_ffn(xn_lines, plan, moe_w1[l], moe_w3[l], moe_w2[l], bm=tl["bm"], n_slots=T * TOP_K)
        assert depth == 1, "the final norm is fused into the last layer's combine"
        out = _combine(h, ys, gate, final_norm_w, tm=tl["tm_cmb"])
    return out.reshape(B, S, D)
```

```python
import functools
import math

import jax
import jax.numpy as jnp
from jax import lax
from jax.experimental import pallas as pl
from jax.experimental.pallas import tpu as pltpu

F32 = jnp.float32
BF16 = jnp.bfloat16

DA_HEAD_DIM = 64
RW_HEAD_DIM = 64
RW_DECAY_RANK = 64
RW_ICLR_RANK = 64
RW_GATE_RANK = 128
ROPE_THETA = 10000.0
N_GROUPS = 8
EXPERTS_PER_GROUP = 8
N_EXPERTS = N_GROUPS * EXPERTS_PER_GROUP
TOP_K = 2
NORM_EPS = 1e-6
SUBLN_EPS = 1e-5
RW_GN_EPS = 64e-5
LAM_INIT = 0.8 - 0.6 * math.exp(-0.3 * 0)

V7X_LANES = 128
V7X_VMEM_LIMIT = 56 * 1024 * 1024
NEG_BIG = -1e30
ATTN_KV_PER_STEP = 4
ATTN_LOOKAHEAD = 8


def _cparams(sem):
    return pltpu.CompilerParams(dimension_semantics=sem, vmem_limit_bytes=V7X_VMEM_LIMIT)


def _inproj_kernel(*refs, mode, tn, tkv, eps, prenormed, emit_xn):
    refs = list(refs)
    x_ref = refs.pop(0)
    nw_ref = None if prenormed else refs.pop(0)
    w_ref = refs.pop(0)
    cos_ref, sin_ref = (refs.pop(0), refs.pop(0)) if mode == "rope" else (None, None)
    o_ref = refs.pop(0)
    xno_ref = refs.pop(0) if emit_xn else None
    xn_ref = x_ref if prenormed else refs.pop(0)

    if not prenormed:
        @pl.when(pl.program_id(1) == 0)
        def _norm():
            x = x_ref[...]
            ms = jnp.mean(x * x, axis=-1, keepdims=True)
            xn_ref[...] = (x * lax.rsqrt(ms + eps) * nw_ref[...]).astype(BF16)
            if emit_xn:
                xno_ref[...] = xn_ref[...]

    if mode == "transposed":
        acc_t = lax.dot_general(w_ref[...], xn_ref[...], (((1,), (1,)), ((), ())), preferred_element_type=F32)
        for c in range(acc_t.shape[1] // tkv):
            o_ref[c] = acc_t[:, c * tkv:(c + 1) * tkv].astype(o_ref.dtype)
        return

    acc = jnp.dot(xn_ref[...], w_ref[...], preferred_element_type=F32)
    if mode == "rope":
        cos = cos_ref[...]
        sin = sin_ref[...]
        for c in range(tn // V7X_LANES):
            cols = slice(c * V7X_LANES, (c + 1) * V7X_LANES)
            blk = acc[:, cols]
            o_ref[:, cols] = (blk * cos + pltpu.roll(blk, V7X_LANES // 2, 1) * sin).astype(o_ref.dtype)
    else:
        o_ref[...] = acc.astype(o_ref.dtype)


def _inproj(x2, nw, w, *, seq, tm, tn, out_dtype, mode="plain", cos=None, sin=None, tkv=None,
            emit_xn=False, w_outer=False):
    T, D = x2.shape
    prenormed = nw is None
    assert not (w_outer and not prenormed) and not (emit_xn and prenormed)
    N = w.shape[0] if mode == "transposed" else w.shape[1]
    assert T % tm == 0 and N % tn == 0 and seq % tm == 0
    rc = (lambda a, b: (b, a)) if w_outer else (lambda a, b: (a, b))
    im = lambda f: (lambda a, b: f(*rc(a, b)))
    w_spec = (pl.BlockSpec((tn, D), im(lambda i, j: (j, 0))) if mode == "transposed"
              else pl.BlockSpec((D, tn), im(lambda i, j: (0, j))))
    in_specs = [pl.BlockSpec((tm, D), im(lambda i, j: (i, 0)))]
    args = [x2]
    if not prenormed:
        in_specs.append(pl.BlockSpec((1, D), im(lambda i, j: (0, 0))))
        args.append(nw.reshape(1, D))
    in_specs.append(w_spec)
    args.append(w)
    if mode == "rope":
        ns = seq // tm
        in_specs += [pl.BlockSpec((tm, V7X_LANES), im(lambda i, j: (i % ns, 0))),
                     pl.BlockSpec((tm, V7X_LANES), im(lambda i, j: (i % ns, 0)))]
        args += [cos, sin]
    if mode == "transposed":
        assert tm % tkv == 0
        out_specs = [pl.BlockSpec((tm // tkv, tn, tkv), im(lambda i, j: (i, j, 0)))]
        out_shape = [jax.ShapeDtypeStruct((T // tkv, N, tkv), out_dtype)]
    else:
        out_specs = [pl.BlockSpec((tm, tn), im(lambda i, j: (i, j)))]
        out_shape = [jax.ShapeDtypeStruct((T, N), out_dtype)]
    if emit_xn:
        out_specs.append(pl.BlockSpec((tm, D), im(lambda i, j: (i, 0))))
        out_shape.append(jax.ShapeDtypeStruct((T, D), BF16))
    grid = (N // tn, T // tm) if w_outer else (T // tm, N // tn)
    outs = pl.pallas_call(
        functools.partial(_inproj_kernel, mode=mode, tn=tn, tkv=tkv, eps=NORM_EPS,
                          prenormed=prenormed, emit_xn=emit_xn),
        grid=grid,
        in_specs=in_specs,
        out_specs=out_specs,
        out_shape=out_shape,
        scratch_shapes=[] if prenormed else [pltpu.VMEM((tm, D), BF16)],
        compiler_params=_cparams(("arbitrary", "arbitrary")),
        name="inproj_" + mode,
    )(*args)
    return outs if emit_xn else outs[0]


def _rope_column_order(w):
    rows, width = w.shape
    half = DA_HEAD_DIM // 2
    w5 = w.reshape(rows, width // V7X_LANES, 2, 2, half)
    return w5.transpose(0, 1, 3, 2, 4).reshape(rows, width)


def _rope_tables(seq):
    half = DA_HEAD_DIM // 2
    inv = ROPE_THETA ** (-jnp.arange(half, dtype=F32) / half)
    ang = jnp.arange(seq, dtype=F32)[:, None] * inv[None, :]
    cos = jnp.cos(ang)
    sin = jnp.sin(ang)
    return jnp.tile(cos, (1, 4)), jnp.concatenate([-sin, -sin, sin, sin], axis=-1)


def _attn_kernel(lam_ref, q_ref, k_ref, vt_ref, sw_ref, o_ref, qs_ref, m_ref, l_ref, acc_ref, *, tq, tk):
    qi = pl.program_id(2)
    d = DA_HEAD_DIM
    q = q_ref[...]
    lane = lax.broadcasted_iota(jnp.int32, q.shape, 1)
    comp1 = (lane % d) < (d // 2)
    zero = jnp.zeros_like(q)
    qs_ref[0:tq, :] = jnp.where(comp1, q, zero)
    qs_ref[tq:2 * tq, :] = jnp.where(comp1, zero, q)
    m_ref[...] = jnp.full(m_ref.shape, NEG_BIG, F32)
    l_ref[...] = jnp.zeros(l_ref.shape, F32)
    acc_ref[...] = jnp.zeros(acc_ref.shape, F32)

    n_diag = tq // tk
    n_full = qi * n_diag

    n_strips = 2 * tq // tk

    def strip_scores(k, si, on_diagonal):
        s = lax.dot_general(k, qs_ref[si * tk:(si + 1) * tk, :], (((1,), (1,)), ((), ())),
                            preferred_element_type=F32)
        if on_diagonal:
            kpos = lax.broadcasted_iota(jnp.int32, s.shape, 0)
            qpos = lax.broadcasted_iota(jnp.int32, s.shape, 1)
            s = jnp.where(kpos <= qpos, s, NEG_BIG)
        return s

    def strip_update(vt, si, s):
        lanes = slice(si * tk, (si + 1) * tk)
        m_old = m_ref[:, lanes]
        m_new = jnp.maximum(m_old, jnp.max(s, axis=0, keepdims=True))
        alpha = jnp.exp2(m_old - m_new)
        p = jnp.exp2(s - m_new)
        l_ref[:, lanes] = alpha * l_ref[:, lanes] + jnp.sum(p, axis=0, keepdims=True)
        acc_ref[:, lanes] = alpha * acc_ref[:, lanes] + jnp.dot(vt, p.astype(BF16), preferred_element_type=F32)
        m_ref[:, lanes] = m_new

    def kv_blocks(blocks):
        kv = [(k_ref[pl.ds(pl.multiple_of(j * tk, tk), tk), :], vt_ref[j]) for j, _ in blocks]
        work = [(b, si, dg) for b, (_, strips) in enumerate(blocks) for si, dg in strips]
        pending = [strip_scores(kv[b][0], si, dg) for b, si, dg in work[:ATTN_LOOKAHEAD]]
        for n, (b, si, _) in enumerate(work):
            if n + ATTN_LOOKAHEAD < len(work):
                nb, nsi, ndg = work[n + ATTN_LOOKAHEAD]
                pending.append(strip_scores(kv[nb][0], nsi, ndg))
            strip_update(kv[b][1], si, pending[n])

    all_strips = [(si, False) for si in range(n_strips)]

    def full_steps(i, c):
        kv_blocks([(i * ATTN_KV_PER_STEP + u, all_strips) for u in range(ATTN_KV_PER_STEP)])
        return c

    lax.fori_loop(0, n_full // ATTN_KV_PER_STEP, full_steps, 0)

    def full_step(j, c):
        kv_blocks([(j, all_strips)])
        return c

    lax.fori_loop(n_full // ATTN_KV_PER_STEP * ATTN_KV_PER_STEP, n_full, full_step, 0)

    kv_blocks([(n_full + c, [(si, si % n_diag == c) for si in range(n_strips) if si % n_diag >= c])
               for c in range(n_diag)])

    o = acc_ref[...] / l_ref[...]
    od = o[:, 0:tq] - lam_ref[0] * o[:, tq:2 * tq]
    ms = jnp.mean(od * od, axis=0, keepdims=True)
    on = (od * lax.rsqrt(ms + SUBLN_EPS) * sw_ref[...]) * (1.0 - LAM_INIT)
    o_ref[...] = on.T.astype(o_ref.dtype)


def _diff_attention(zqk, vt, lam, subln_w, *, batch, seq, n_heads, tq, tk):
    T = zqk.shape[0]
    hw = 2 * DA_HEAD_DIM
    nq = seq // tq
    nk = seq // tk
    sw_b = jnp.broadcast_to(subln_w.reshape(hw, 1), (hw, tq)).astype(F32)
    return pl.pallas_call(
        functools.partial(_attn_kernel, tq=tq, tk=tk),
        grid=(batch, n_heads, nq),
        in_specs=[
            pl.BlockSpec(memory_space=pltpu.SMEM),
            pl.BlockSpec((tq, hw), lambda b, h, i: (b * nq + i, h)),
            pl.BlockSpec((seq, hw), lambda b, h, i: (b, n_heads + h)),
            pl.BlockSpec((nk, hw, tk), lambda b, h, i: (b, h, 0)),
            pl.BlockSpec((hw, tq), lambda b, h, i: (0, 0)),
        ],
        out_specs=pl.BlockSpec((tq, hw), lambda b, h, i: (b * nq + i, h)),
        out_shape=jax.ShapeDtypeStruct((T, n_heads * hw), BF16),
        scratch_shapes=[
            pltpu.VMEM((2 * tq, hw), BF16),
            pltpu.VMEM((1, 2 * tq), F32),
            pltpu.VMEM((1, 2 * tq), F32),
            pltpu.VMEM((hw, 2 * tq), F32),
        ],
        compiler_params=_cparams(("arbitrary", "arbitrary", "arbitrary")),
        name="diff_attn",
    )(lam.reshape(1), zqk, zqk, vt, sw_b)


RW_CHUNK = 64
RW_PAIR = 2 * RW_HEAD_DIM
RW_INV_BLOCK = 16


def _bmm(a, b):
    return jnp.einsum("bij,bjk->bik", a.astype(BF16), b.astype(BF16), preferred_element_type=F32)


def _bmm_nt(a, b):
    return jnp.einsum("bik,bjk->bij", a.astype(BF16), b.astype(BF16), preferred_element_type=F32)


def _bmm_tn(a, b):
    return jnp.einsum("bti,btj->bij", a.astype(BF16), b.astype(BF16), preferred_element_type=F32)


def _mm_split(x, e):
    hi = x.astype(BF16)
    lo = (x - hi.astype(F32)).astype(BF16)
    return (jnp.dot(hi, e, preferred_element_type=F32) + jnp.dot(lo, e, preferred_element_type=F32))


def _sigmoid(x):
    return 1.0 / (1.0 + jnp.exp(-x))


def _softplus(x):
    return jnp.maximum(x, 0.0) + jnp.log(1.0 + jnp.exp(-jnp.abs(x)))


def _unit_lower_inverse(a, eye, diag_blk):
    ad = jnp.where(diag_blk, a, 0.0)
    ao = a - ad
    a2 = _bmm(ad, ad)
    a4 = _bmm(a2, a2)
    a8 = _bmm(a4, a4)
    td = eye + ad
    td = td + _bmm(td, a2)
    td = td + _bmm(td, a4)
    td = td + _bmm(td, a8)
    n1 = _bmm(td, ao)
    n2 = _bmm(n1, n1)
    x = td + _bmm(n2, td)
    return x + _bmm(n1, x)


def _rwkv_kernel(zr_ref, zk_ref, zv_ref, zwa_ref, zg_ref, mur_ref, muk_ref, muv_ref, muwa_ref, mug_ref,
                 vec_ref, ww_ref, wa_ref, gup_ref, o_ref,
                 cr_ref, ck_ref, cv_ref, cwa_ref, cg_ref, state_ref, *, L, PW):
    t = pl.program_id(2)
    C = RW_CHUNK
    PL = RW_PAIR
    NC = L // C

    @pl.when(t == 0)
    def _reset():
        for c_ref in (cr_ref, ck_ref, cv_ref, cwa_ref, cg_ref):
            c_ref[...] = jnp.zeros(c_ref.shape, F32)
        state_ref[...] = jnp.zeros(state_ref.shape, F32)

    def shift_mix(z_ref, mu_ref, c_ref):
        z = z_ref[...].astype(F32)
        row = lax.broadcasted_iota(jnp.int32, z.shape, 0)
        zprev = jnp.where(row == 0, c_ref[0:1, :], pltpu.roll(z, 1, 0))
        c_ref[0:1, :] = z[L - 1:L, :]
        return z + (zprev - z) * mu_ref[...]

    r = shift_mix(zr_ref, mur_ref, cr_ref)
    k = shift_mix(zk_ref, muk_ref, ck_ref)
    v = shift_mix(zv_ref, muv_ref, cv_ref)
    zwa = shift_mix(zwa_ref, muwa_ref, cwa_ref)
    zg = shift_mix(zg_ref, mug_ref, cg_ref)

    ri = lax.broadcasted_iota(jnp.int32, (PL, PL), 0)
    ci = lax.broadcasted_iota(jnp.int32, (PL, PL), 1)
    eye = (ri == ci).astype(F32)
    strict = ci < ri
    incl = ci <= ri
    diag_blk = (ri // RW_INV_BLOCK) == (ci // RW_INV_BLOCK)
    seg_w = 2 * PL if (PW * PL) % (2 * PL) == 0 else PL
    sr = lax.broadcasted_iota(jnp.int32, (seg_w, seg_w), 0)
    sc = lax.broadcasted_iota(jnp.int32, (seg_w, seg_w), 1)
    seg_ones = ((sr // RW_HEAD_DIM) == (sc // RW_HEAD_DIM)).astype(BF16)
    rc = lax.broadcasted_iota(jnp.int32, (C, C), 0)
    cc = lax.broadcasted_iota(jnp.int32, (C, C), 1)
    tri_incl = (cc <= rc).astype(BF16)
    head0 = lax.broadcasted_iota(jnp.int32, (C, PL), 1) < RW_HEAD_DIM
    lanes = [slice(pi * PL, (pi + 1) * PL) for pi in range(PW)]
    rows = [slice(c * C, (c + 1) * C) for c in range(NC)]

    def seg_sum(x):
        return jnp.concatenate([_mm_split(x[:, c * seg_w:(c + 1) * seg_w], seg_ones)
                                for c in range(PW * PL // seg_w)], axis=1)

    w0, a0, k_k, k_a, r_k, lnx_w, lnx_b = (vec_ref[i:i + 1, :] for i in range(7))
    w_pre = w0 + jnp.dot(jnp.tanh(zwa).astype(BF16), ww_ref[...], preferred_element_type=F32)
    logdec = -jnp.exp(-_softplus(-w_pre) - 0.5)
    a = _sigmoid(a0 + jnp.dot(zwa.astype(BF16), wa_ref[...], preferred_element_type=F32))
    g = jnp.dot(_sigmoid(zg).astype(BF16), gup_ref[...], preferred_element_type=F32)
    kk = k * k_k
    kkn = kk / jnp.maximum(jnp.sqrt(seg_sum(kk * kk)), 1e-12)
    kf = k * (1.0 + (a - 1.0) * k_a)
    a_s = -kkn
    b_s = kkn * a
    bonus = seg_sum(r * kf * r_k) * v

    ld_hi = logdec.astype(BF16)
    ld_lo = (logdec - ld_hi.astype(F32)).astype(BF16)
    cum_c = [jnp.dot(tri_incl, ld_hi[rs], preferred_element_type=F32)
             + jnp.dot(tri_incl, ld_lo[rs], preferred_element_type=F32) for rs in rows]
    tot_c = [cu[C - 1:C, :] for cu in cum_c]
    cum = jnp.concatenate(cum_c, axis=0)
    tot = jnp.concatenate([jnp.broadcast_to(tc, (C, tc.shape[1])) for tc in tot_c], axis=0)
    p_inv = jnp.exp(-cum)
    p_end = jnp.exp(tot - cum)

    def stack(x):
        out = []
        for rs in rows:
            for sl in lanes:
                blk = x[rs, sl]
                out.append(jnp.concatenate([jnp.where(head0, blk, 0.0), jnp.where(head0, 0.0, blk)], axis=0))
        return jnp.stack(out, axis=0).astype(BF16)

    rt_s = stack(r * jnp.exp(cum))
    at_s = stack(a_s * jnp.exp(cum - logdec))
    kt_s = stack(kf * p_inv)
    bt_s = stack(b_s * p_inv)
    kh_s = stack(kf * p_end)
    bh_s = stack(b_s * p_end)
    v_s = stack(v)

    m1 = _bmm_nt(jnp.concatenate([at_s, rt_s], axis=1), jnp.concatenate([bt_s, kt_s], axis=1))
    a_ab = jnp.where(strict, m1[:, 0:PL, 0:PL], 0.0)
    a_ak = jnp.where(strict, m1[:, 0:PL, PL:2 * PL], 0.0)
    a_rb = jnp.where(incl, m1[:, PL:2 * PL, 0:PL], 0.0)
    a_rk = jnp.where(incl, m1[:, PL:2 * PL, PL:2 * PL], 0.0)
    tinv = _unit_lower_inverse(a_ab, eye, diag_blk)
    wu = _bmm(tinv, jnp.concatenate([at_s, _bmm(a_ak, v_s).astype(BF16)], axis=2))

    s = state_ref[...]
    us, s0s = [], []
    for c in range(NC):
        sel = slice(c * PW, (c + 1) * PW)
        sb = s.astype(BF16)
        u = _bmm_nt(wu[sel, :, 0:PL], sb) + wu[sel, :, PL:2 * PL]
        us.append(u)
        s0s.append(sb)
        p_tot = jnp.stack([jnp.exp(tot_c[c][:, sl]) for sl in lanes], axis=0)
        s = s * p_tot + _bmm_tn(jnp.concatenate([u.astype(BF16), v_s[sel]], axis=1),
                                jnp.concatenate([bh_s[sel], kh_s[sel]], axis=1))
    state_ref[...] = s

    u_all = jnp.concatenate(us, axis=0)
    s0_all = jnp.concatenate(s0s, axis=0)
    y2 = _bmm_nt(rt_s, s0_all) + _bmm(a_rb, u_all) + _bmm(a_rk, v_s)
    y2 = y2[:, 0:C, :] + y2[:, C:2 * C, :]
    y = jnp.concatenate([jnp.concatenate([y2[c * PW + pi] for pi in range(PW)], axis=1) for c in range(NC)], axis=0)

    inv_n = 1.0 / RW_HEAD_DIM
    mean = seg_sum(y) * inv_n
    yc = y - mean
    var = seg_sum(yc * yc) * inv_n
    yn = yc * lax.rsqrt(var + RW_GN_EPS) * lnx_w + lnx_b
    o_ref[...] = ((yn + bonus) * g).astype(o_ref.dtype)


def _rwkv7(zrw, mu, w0, w_up, a0, a_up, g_up, k_k, k_a, r_k, lnx_w, lnx_b, *, batch, seq, n_heads, L, PW):
    T = zrw.shape[0]
    HC = n_heads * RW_HEAD_DIM
    PWL = PW * RW_PAIR
    assert HC % PWL == 0 and seq % L == 0 and L % RW_CHUNK == 0
    ng = HC // PWL
    nt = seq // L
    lora_w = RW_DECAY_RANK + RW_ICLR_RANK
    assert lora_w == V7X_LANES and RW_GATE_RANK == V7X_LANES
    vecs = jnp.stack([w0, a0, k_k, k_a, r_k.reshape(HC), lnx_w, lnx_b, jnp.zeros((HC,), F32)], axis=0)
    ww = jnp.concatenate([w_up, jnp.zeros((RW_ICLR_RANK, HC), F32)], axis=0).astype(BF16)
    wa = jnp.concatenate([jnp.zeros((RW_DECAY_RANK, HC), F32), a_up], axis=0).astype(BF16)
    gw = g_up.astype(BF16)
    mu2 = mu.reshape(1, -1)
    cb = HC // PWL
    wa_blk = 3 * HC // V7X_LANES
    zrow = lambda b, g, t: b * nt + t
    big = lambda sec: pl.BlockSpec((L, PWL), lambda b, g, t: (zrow(b, g, t), sec * cb + g))
    small = lambda off: pl.BlockSpec((L, V7X_LANES), lambda b, g, t: (zrow(b, g, t), wa_blk + off))
    mu_big = lambda sec: pl.BlockSpec((1, PWL), lambda b, g, t: (0, sec * cb + g))
    mu_small = lambda off: pl.BlockSpec((1, V7X_LANES), lambda b, g, t: (0, wa_blk + off))
    wspec = pl.BlockSpec((V7X_LANES, PWL), lambda b, g, t: (0, g))
    return pl.pallas_call(
        functools.partial(_rwkv_kernel, L=L, PW=PW),
        grid=(batch, ng, nt),
        in_specs=[big(0), big(1), big(2), small(0), small(1),
                  mu_big(0), mu_big(1), mu_big(2), mu_small(0), mu_small(1),
                  pl.BlockSpec((8, PWL), lambda b, g, t: (0, g)),
                  wspec, wspec, wspec],
        out_specs=pl.BlockSpec((L, PWL), lambda b, g, t: (zrow(b, g, t), g)),
        out_shape=jax.ShapeDtypeStruct((T, HC), BF16),
        scratch_shapes=[pltpu.VMEM((8, PWL), F32), pltpu.VMEM((8, PWL), F32), pltpu.VMEM((8, PWL), F32),
                        pltpu.VMEM((8, V7X_LANES), F32), pltpu.VMEM((8, V7X_LANES), F32),
                        pltpu.VMEM((PW, RW_PAIR, RW_PAIR), F32)],
        compiler_params=_cparams(("arbitrary", "arbitrary", "arbitrary")),
        name="rwkv7_chunked",
    )(zrw, zrw, zrw, zrw, zrw, mu2, mu2, mu2, mu2, mu2, vecs, ww, wa, gw)


def _route(lg):
    lane_i = lax.broadcasted_iota(jnp.int32, lg.shape, 1)
    lane = lane_i.astype(F32)
    is_g = lane_i < N_GROUPS
    mg = jnp.max(jnp.where(is_g, lg, NEG_BIG), axis=-1, keepdims=True)
    eg = jnp.exp(jnp.where(is_g, lg - mg, NEG_BIG))
    pg = eg / jnp.sum(eg, axis=-1, keepdims=True)
    p_g = jnp.max(pg, axis=-1, keepdims=True)
    g_sel = jnp.min(jnp.where(is_g & (pg == p_g), lane, 1e9), axis=-1, keepdims=True)
    grp = (jnp.right_shift(lane_i, 3) - 1).astype(F32)
    is_e = (lane_i >= N_GROUPS) & (lane_i < N_GROUPS + N_EXPERTS) & (grp == g_sel)
    me = jnp.max(jnp.where(is_e, lg, NEG_BIG), axis=-1, keepdims=True)
    ee = jnp.exp(jnp.where(is_e, lg - me, NEG_BIG))
    pe = ee / jnp.sum(ee, axis=-1, keepdims=True)
    p1 = jnp.max(jnp.where(is_e, pe, -1.0), axis=-1, keepdims=True)
    i1 = jnp.min(jnp.where(is_e & (pe == p1), lane, 1e9), axis=-1, keepdims=True)
    rest = is_e & (lane != i1)
    p2 = jnp.max(jnp.where(rest, pe, -1.0), axis=-1, keepdims=True)
    i2 = jnp.min(jnp.where(rest & (pe == p2), lane, 1e9), axis=-1, keepdims=True)
    den = p1 + p2
    eid = jnp.where(lane_i == 0, i1 - N_GROUPS, jnp.where(lane_i == 1, i2 - N_GROUPS, 0.0)).astype(jnp.int32)
    gate = jnp.where(lane_i == 0, p_g * p1 / den, jnp.where(lane_i == 1, p_g * p2 / den, 0.0))
    return eid, gate


def _to_row_tiles(x):
    return x.astype(BF16).reshape(x.shape[0], x.shape[1] // V7X_LANES, V7X_LANES)


def _from_row_tiles(x3):
    return x3.reshape(x3.shape[0], x3.shape[1] * x3.shape[2])


def _outproj_kernel(oda_ref, orw_ref, wt_hbm, wb_hbm, x_ref, n2_ref, rcat_ref, rb_ref,
                    h_ref, xn_ref, eid_ref, gate_ref, wt_ref, wb_ref, sem, *, tm, eps):
    @pl.when(pl.program_id(0) == 0)
    def _load_weights():
        copies = (pltpu.make_async_copy(wt_hbm, wt_ref, sem), pltpu.make_async_copy(wb_hbm, wb_ref, sem))
        for cp in copies:
            cp.start()
        for cp in copies:
            cp.wait()

    acc = (jnp.dot(oda_ref[...], wt_ref[...], preferred_element_type=F32)
           + jnp.dot(orw_ref[...], wb_ref[...], preferred_element_type=F32))
    h = x_ref[...] + acc
    h_ref[...] = h
    ms = jnp.mean(h * h, axis=-1, keepdims=True)
    xn = h * lax.rsqrt(ms + eps) * n2_ref[...]
    xn_ref[...] = _to_row_tiles(xn)
    hw = jnp.dot(xn.astype(BF16), rcat_ref[...], preferred_element_type=F32)
    lg = hw[:, 0:V7X_LANES] + hw[:, V7X_LANES:2 * V7X_LANES] + rb_ref[...]
    eid, gate = _route(lg)
    eid_ref[...] = eid
    gate_ref[...] = gate


def _outproj_route(oda, orw, w_out, x2, n2, wg, bg, we, be, *, tm):
    T, D = x2.shape
    w1 = oda.shape[1]
    w2 = orw.shape[1]
    nl = D // V7X_LANES
    wt = w_out[:w1].astype(BF16)
    wb = w_out[w1:].astype(BF16)
    pad = V7X_LANES - N_GROUPS - N_EXPERTS
    wr = jnp.concatenate([wg, we, jnp.zeros((D, pad), F32)], axis=1)
    rb = jnp.concatenate([bg, be, jnp.zeros((pad,), F32)]).reshape(1, V7X_LANES)
    rhi = wr.astype(BF16)
    rlo = (wr - rhi.astype(F32)).astype(BF16)
    rcat = jnp.concatenate([rhi, rlo], axis=1)
    const = lambda shape: pl.BlockSpec(shape, lambda i: (0, 0))
    return pl.pallas_call(
        functools.partial(_outproj_kernel, tm=tm, eps=NORM_EPS),
        grid=(T // tm,),
        in_specs=[pl.BlockSpec((tm, w1), lambda i: (i, 0)), pl.BlockSpec((tm, w2), lambda i: (i, 0)),
                  pl.BlockSpec(memory_space=pl.ANY), pl.BlockSpec(memory_space=pl.ANY),
                  pl.BlockSpec((tm, D), lambda i: (i, 0)), const((1, D)),
                  const((D, 2 * V7X_LANES)), const((1, V7X_LANES))],
        out_specs=[pl.BlockSpec((tm, D), lambda i: (i, 0)),
                   pl.BlockSpec((tm, nl, V7X_LANES), lambda i: (i, 0, 0)),
                   pl.BlockSpec((tm, V7X_LANES), lambda i: (i, 0)),
                   pl.BlockSpec((tm, V7X_LANES), lambda i: (i, 0))],
        out_shape=[jax.ShapeDtypeStruct((T, D), F32),
                   jax.ShapeDtypeStruct((T, nl, V7X_LANES), BF16),
                   jax.ShapeDtypeStruct((T, V7X_LANES), jnp.int32),
                   jax.ShapeDtypeStruct((T, V7X_LANES), F32)],
        scratch_shapes=[pltpu.VMEM((w1, D), BF16), pltpu.VMEM((w2, D), BF16), pltpu.SemaphoreType.DMA],
        compiler_params=_cparams(("arbitrary",)),
        name="outproj_route",
    )(oda, orw, wt, wb, x2, n2.reshape(1, D), rcat, rb)


MOE_DMA_UNROLL = 8


def _moe_kernel(blk_e_ref, nxt_e_ref, par_ref, nused_ref, tok_ref, tokn_ref, dst_ref,
                xn_hbm, w1_hbm, w3_hbm, w2_hbm, ys_hbm,
                xbuf, ybuf, wf1, wf3, wf2, w1b, w3b, w2b, sem_in, sem_out, sem_w, *, bm, nl, n_slots):
    i = pl.program_id(0)
    n_used = nused_ref[0]
    slot = i % 2

    def gather(ids_ref, dst_slot):
        def body(r, c):
            pltpu.make_async_copy(xn_hbm.at[ids_ref[0, r]], xbuf.at[dst_slot, r], sem_in.at[dst_slot]).start()
            return c
        lax.fori_loop(0, bm, body, 0, unroll=MOE_DMA_UNROLL)

    def wait_gather(src_slot):
        pltpu.make_async_copy(xn_hbm.at[pl.ds(0, bm)], xbuf.at[src_slot], sem_in.at[src_slot]).wait()

    def scatter():
        def body(r, c):
            pltpu.make_async_copy(ybuf.at[r], ys_hbm.at[dst_ref[0, r]], sem_out).start()
            return c
        lax.fori_loop(0, bm, body, 0, unroll=MOE_DMA_UNROLL)

    def wait_scatter():
        pltpu.make_async_copy(ybuf, ys_hbm.at[pl.ds(0, bm)], sem_out).wait()

    def weight_copies(e, wslot):
        return (pltpu.make_async_copy(w1_hbm.at[e], wf1.at[wslot], sem_w.at[wslot]),
                pltpu.make_async_copy(w3_hbm.at[e], wf3.at[wslot], sem_w.at[wslot]),
                pltpu.make_async_copy(w2_hbm.at[e], wf2.at[wslot], sem_w.at[wslot]))

    @pl.when(i == 0)
    def _prologue():
        for cp in weight_copies(blk_e_ref[0], par_ref[0]):
            cp.start()
        gather(tok_ref, 0)
        ybuf[...] = jnp.zeros(ybuf.shape, ybuf.dtype)
        spare = pltpu.make_async_copy(ybuf, ys_hbm.at[pl.ds(n_slots, bm)], sem_out)
        spare.start()
        spare.wait()

    @pl.when(i + 1 < n_used)
    def _prefetch():
        gather(tokn_ref, 1 - slot)

    @pl.when(i < n_used)
    def _active():
        e = blk_e_ref[i]
        first = jnp.logical_or(i == 0, e != blk_e_ref[jnp.maximum(i - 1, 0)])

        @pl.when(first)
        def _new_expert():
            wslot = par_ref[i]
            for cp in weight_copies(e, wslot):
                cp.wait()
            nxt = nxt_e_ref[i]

            @pl.when(nxt >= 0)
            def _():
                for cp in weight_copies(nxt, 1 - wslot):
                    cp.start()

            w1b[...] = wf1[wslot].astype(BF16)
            w3b[...] = wf3[wslot].astype(BF16)
            w2b[...] = wf2[wslot].astype(BF16)

        wait_gather(slot)
        x = _from_row_tiles(xbuf[slot])
        h1 = jnp.dot(x, w1b[...], preferred_element_type=F32)
        h3 = jnp.dot(x, w3b[...], preferred_element_type=F32)
        hh = (h1 * _sigmoid(h1) * h3).astype(BF16)
        y = jnp.dot(hh, w2b[...], preferred_element_type=F32)

        @pl.when(i > 0)
        def _drain_prev():
            wait_scatter()

        ybuf[...] = _to_row_tiles(y)
        scatter()

        @pl.when(i == n_used - 1)
        def _drain_last():
            wait_scatter()


def _moe_ffn(xn_rows, plan, w1, w3, w2, *, bm, n_slots):
    blk_e, nxt_e, par, n_used, tok_p, dst_p = plan
    E, D, DE = w1.shape
    nl = D // V7X_LANES
    nb = blk_e.shape[0]
    ids = lambda imap: pl.BlockSpec((None, 1, bm), imap, memory_space=pltpu.SMEM)
    grid_spec = pltpu.PrefetchScalarGridSpec(
        num_scalar_prefetch=4,
        grid=(nb,),
        in_specs=[
            ids(lambda i, *_: (i, 0, 0)),
            ids(lambda i, *_: (jnp.minimum(i + 1, nb - 1), 0, 0)),
            ids(lambda i, *_: (i, 0, 0)),
            pl.BlockSpec(memory_space=pl.ANY),
            pl.BlockSpec(memory_space=pl.ANY),
            pl.BlockSpec(memory_space=pl.ANY),
            pl.BlockSpec(memory_space=pl.ANY),
        ],
        out_specs=pl.BlockSpec(memory_space=pl.ANY),
        scratch_shapes=[pltpu.VMEM((2, bm, nl, V7X_LANES), BF16), pltpu.VMEM((bm, nl, V7X_LANES), BF16),
                        pltpu.VMEM((2, D, DE), F32), pltpu.VMEM((2, D, DE), F32), pltpu.VMEM((2, DE, D), F32),
                        pltpu.VMEM((D, DE), BF16), pltpu.VMEM((D, DE), BF16), pltpu.VMEM((DE, D), BF16),
                        pltpu.SemaphoreType.DMA((2,)), pltpu.SemaphoreType.DMA, pltpu.SemaphoreType.DMA((2,))],
    )
    tok3 = tok_p.reshape(nb, 1, bm)
    ys = pl.pallas_call(
        functools.partial(_moe_kernel, bm=bm, nl=nl, n_slots=n_slots),
        grid_spec=grid_spec,
        out_shape=jax.ShapeDtypeStruct((n_slots + bm, nl, V7X_LANES), BF16),
        compiler_params=_cparams(("arbitrary",)),
        name="moe_ffn",
    )(blk_e, nxt_e, par, n_used, tok3, tok3, dst_p.reshape(nb, 1, bm), xn_rows, w1, w3, w2)
    return ys.reshape((n_slots + bm) // TOP_K, TOP_K, nl, V7X_LANES)


def _moe_plan(eid, *, bm):
    T = eid.shape[0]
    M = T * TOP_K
    E = N_EXPERTS
    i32 = jnp.int32
    e_flat = eid.reshape(M)
    order = jnp.argsort(e_flat).astype(i32)
    experts = jnp.arange(E, dtype=i32)
    counts = jnp.sum((e_flat[:, None] == experts[None, :]).astype(i32), axis=0)
    start = jnp.cumsum(counts) - counts
    padded = (counts + bm - 1) // bm * bm
    pend = jnp.cumsum(padded)
    pstart = pend - padded
    nb = M // bm + E
    blk = jnp.arange(nb, dtype=i32)
    blk_start = blk * bm
    n_used = (pend[-1] // bm).astype(i32)
    blk_e = jnp.minimum(jnp.sum((blk_start[:, None] >= pend[None, :]).astype(i32), axis=1), E - 1)
    blk_e = jnp.where(blk < n_used, blk_e, blk_e[jnp.maximum(n_used - 1, 0)])
    off = blk_start - pstart[blk_e]
    base = start[blk_e] + off
    valid = jnp.where(blk < n_used, jnp.clip(counts[blk_e] - off, 0, bm), 0)
    r = jnp.arange(bm, dtype=i32)
    slot_p = order[jnp.clip(base[:, None] + r[None, :], 0, M - 1)]
    tok_p = slot_p // TOP_K
    dst_p = jnp.where(r[None, :] < valid[:, None], slot_p, M + r[None, :])
    seg_first = jnp.concatenate([jnp.ones((1,), bool), blk_e[1:] != blk_e[:-1]])
    par = (jnp.cumsum(seg_first.astype(i32)) - 1) % 2
    later = (experts[None, :] > experts[:, None]) & (counts[None, :] > 0)
    nxt_of = jnp.min(jnp.where(later, experts[None, :], E), axis=1)
    nxt_e = jnp.where(nxt_of == E, -1, nxt_of)[blk_e]
    return blk_e, nxt_e.astype(i32), par.astype(i32), n_used.reshape(1), tok_p, dst_p


def _combine_kernel(h_ref, ys_ref, gate_ref, fw_ref, o_ref, *, tm, nl, eps):
    gate = gate_ref[...]
    acc = h_ref[...]
    for j in range(TOP_K):
        yj = _from_row_tiles(ys_ref[:, j, :, :]).astype(F32)
        acc = acc + yj * gate[:, j:j + 1]
    ms = jnp.mean(acc * acc, axis=-1, keepdims=True)
    o_ref[...] = acc * lax.rsqrt(ms + eps) * fw_ref[...]


def _combine(h, ys, gate, fw, *, tm):
    T, D = h.shape
    nl = D // V7X_LANES
    return pl.pallas_call(
        functools.partial(_combine_kernel, tm=tm, nl=nl, eps=NORM_EPS),
        grid=(T // tm,),
        in_specs=[pl.BlockSpec((tm, D), lambda i: (i, 0)),
                  pl.BlockSpec((tm, TOP_K, nl, V7X_LANES), lambda i: (i, 0, 0, 0)),
                  pl.BlockSpec((tm, V7X_LANES), lambda i: (i, 0)),
                  pl.BlockSpec((1, D), lambda i: (0, 0))],
        out_specs=pl.BlockSpec((tm, D), lambda i: (i, 0)),
        out_shape=jax.ShapeDtypeStruct((T, D), F32),
        compiler_params=_cparams(("arbitrary",)),
        name="combine_norm",
    )(h, ys, gate, fw.reshape(1, D))


def _tiles(T, seq, D):
    pick = lambda n, prefs: next(p for p in prefs if n % p == 0)
    return dict(
        tm_in=pick(seq, (512, 256, 128)),
        tm_rw=pick(seq, (1024, 512, 256, 128)),
        tn_da=1024,
        tq=pick(seq, (1024, 512, 256, 128)),
        tk=pick(seq, (256, 128)),
        rw_L=pick(seq, (256, 128, 64)),
        rw_PW=8,
        tm_out=pick(T, (512, 256, 128)),
        bm=256,
        tm_cmb=pick(T, (512, 256, 128)),
    )


def kernel(x, norm1_w, w_in, lam_q1, lam_k1, lam_q2, lam_k2, subln_w, rw_mu, rw_w0, rw_w_up, rw_a0, rw_a_up, rw_g_up, rw_k_k, rw_k_a, rw_r_k, rw_lnx_w, rw_lnx_b, w_out, norm2_w, router_group_w, router_group_b, router_expert_w, router_expert_b, moe_w1, moe_w3, moe_w2, final_norm_w):
    B, S, D = x.shape
    T = B * S
    depth = w_in.shape[0]
    rw_heads = rw_w0.shape[1] // RW_HEAD_DIM
    rw_cols = rw_mu.shape[1]
    da_cols = w_in.shape[2] - rw_cols
    da_width = da_cols // 3
    da_heads = da_width // (2 * DA_HEAD_DIM)
    tl = _tiles(T, S, D)
    cos, sin = _rope_tables(S)
    q_scale = DA_HEAD_DIM ** -0.5 * math.log2(math.e)
    colscale = jnp.concatenate([jnp.full((da_width,), q_scale, F32), jnp.ones((da_width,), F32)])

    h = x.reshape(T, D)
    for l in range(depth):
        assert l == 0, "lam_init is specialised to the first layer"
        w_qk = _rope_column_order(w_in[l][:, :2 * da_width] * colscale).astype(BF16)
        w_v = w_in[l][:, 2 * da_width:da_cols].T.astype(BF16)
        w_rw = w_in[l][:, da_cols:].astype(BF16)
        zqk, xn1 = _inproj(h, norm1_w[l], w_qk, seq=S, tm=tl["tm_in"], tn=tl["tn_da"], out_dtype=BF16,
                           mode="rope", cos=cos, sin=sin, emit_xn=True)
        vt = _inproj(xn1, None, w_v, seq=S, tm=tl["tm_in"], tn=tl["tn_da"], out_dtype=BF16,
                     mode="transposed", tkv=tl["tk"])
        zrw = _inproj(xn1, None, w_rw, seq=S, tm=tl["tm_rw"], tn=rw_cols // 2, out_dtype=BF16, w_outer=True)
        lam = (jnp.exp(jnp.sum(lam_q1[l] * lam_k1[l])) - jnp.exp(jnp.sum(lam_q2[l] * lam_k2[l])) + LAM_INIT)
        o_da = _diff_attention(zqk, vt, lam.astype(F32), subln_w[l], batch=B, seq=S, n_heads=da_heads,
                               tq=tl["tq"], tk=tl["tk"])
        o_rw = _rwkv7(zrw, rw_mu[l], rw_w0[l], rw_w_up[l], rw_a0[l], rw_a_up[l], rw_g_up[l], rw_k_k[l], rw_k_a[l],
                      rw_r_k[l], rw_lnx_w[l], rw_lnx_b[l], batch=B, seq=S, n_heads=rw_heads,
                      L=tl["rw_L"], PW=tl["rw_PW"])
        h, xn_rows, eid, gate = _outproj_route(o_da, o_rw, w_out[l], h, norm2_w[l], router_group_w[l],
                                                router_group_b[l], router_expert_w[l], router_expert_b[l],
                                                tm=tl["tm_out"])
        plan = _moe_plan(eid[:, :TOP_K], bm=tl["bm"])
        ys = _moe_ffn(xn_rows, plan, moe_w1[l], moe_w3[l], moe_w2[l], bm=tl["bm"], n_slots=T * TOP_K)
        assert depth == 1, "the final norm is fused into the last layer's combine"
        out = _combine(h, ys, gate, final_norm_w, tm=tl["tm_cmb"])
    return out.reshape(B, S, D)
```
